```python
import math
import jax, jax.numpy as jnp
from jax import lax
import numpy as np

D_MODEL = 1024
BATCH = 8
SEQ = 2048
DEPTH = 2

N_A = DEPTH // 2
N_B = DEPTH - N_A
HEAD_DIM = 64
CHUNK = 128
N_SG = 12
SG_WIDTH = N_SG * HEAD_DIM
DIL_PAIRS = ((128, 1), (512, 4), (2048, 16))
HEADS_PER_DIL = 4
N_DIL_HEADS = HEADS_PER_DIL * len(DIL_PAIRS)
DIL_Q_WIDTH = N_DIL_HEADS * HEAD_DIM
DIL_OUT_WIDTH = HEADS_PER_DIL * HEAD_DIM
N_MEM = 256
MEM_HEADS = 4
MEM_WIDTH = MEM_HEADS * HEAD_DIM
N_EXPERTS = 16
N_EXPERT_GROUPS = 4
EXPERTS_PER_GROUP = N_EXPERTS // N_EXPERT_GROUPS
TOP_K = 2
D_EXPERT = 1024
MOE_BLOCK = 128
ALPHA = (2 * DEPTH) ** 0.25
BETA = (8 * DEPTH) ** -0.25
LN_EPS = 1e-5
ATT_SCALE = 1.0 / math.sqrt(HEAD_DIM)

kernel_name = "yoco_gmlp_dilated_moe_trunk"


def layer_norm(x, g, b):
    xf = x.astype(jnp.float32)
    mu = jnp.mean(xf, -1, keepdims=True)
    var = jnp.mean(jnp.square(xf - mu), -1, keepdims=True)
    return ((xf - mu) * lax.rsqrt(var + LN_EPS)).astype(x.dtype) * g + b


def alibi_slopes():
    return 2.0 ** (-8.0 * jnp.arange(1, N_DIL_HEADS + 1, dtype=jnp.float32) / N_DIL_HEADS)


def spatial_gating(u, v, w_s, b_s):
    B, S, G, Dh = v.shape
    vc = v.reshape(B, S // CHUNK, CHUNK, G, Dh)
    causal = jnp.tril(jnp.ones((CHUNK, CHUNK), bool))
    w = jnp.where(causal[None], w_s, 0).astype(v.dtype)
    s = jnp.einsum('gpq,bcqgd->bcpgd', w, vc) + b_s.T[None, None, :, :, None].astype(v.dtype)
    return u * s.reshape(B, S, G, Dh)


def dilated_branch(q, k, v, window, dilation, slopes):
    B, S, H, Dh = q.shape
    steps = window // dilation
    L = S // dilation
    N = B * dilation

    def by_residue(t):
        return t.reshape(B, L, dilation, H, Dh).transpose(0, 2, 3, 1, 4).reshape(N, H, L, Dh)

    qr, kr, vr = by_residue(q), by_residue(k), by_residue(v)
    nb = -(-L // CHUNK)
    pad = nb * CHUNK - L
    qr = jnp.pad(qr, ((0, 0), (0, 0), (0, pad), (0, 0)))
    kr = jnp.pad(kr, ((0, 0), (0, 0), (CHUNK, pad), (0, 0)))
    vr = jnp.pad(vr, ((0, 0), (0, 0), (CHUNK, pad), (0, 0)))
    qb = qr.reshape(N, H, nb, CHUNK, Dh)
    kb = kr.reshape(N, H, nb + 1, CHUNK, Dh)
    vb = vr.reshape(N, H, nb + 1, CHUNK, Dh)
    kband = jnp.concatenate([kb[:, :, :-1], kb[:, :, 1:]], axis=3)
    vband = jnp.concatenate([vb[:, :, :-1], vb[:, :, 1:]], axis=3)
    s = jnp.einsum('nhbqd,nhbkd->nhbqk', qb, kband).astype(jnp.float32) * ATT_SCALE
    iq = jnp.arange(CHUNK)[:, None]
    jk = jnp.arange(2 * CHUNK)[None, :]
    rel = CHUNK + iq - jk
    key_pos = (jnp.arange(nb)[:, None, None] - 1) * CHUNK + jk[None]
    valid = (rel >= 0)[None] & (rel <= steps)[None] & (key_pos >= 0)
    bias = -slopes[:, None, None, None] * (dilation * rel).astype(jnp.float32)[None, None]
    s = jnp.where(valid[None, None], s + bias[None], -jnp.inf)
    m = jnp.max(s, -1, keepdims=True)
    p = jnp.exp(s - m)
    l = jnp.sum(p, -1, keepdims=True)
    o = jnp.einsum('nhbqk,nhbkd->nhbqd', (p / l).astype(v.dtype), vband)
    lse = (m + jnp.log(l))[..., 0]
    o = o.reshape(N, H, nb * CHUNK, Dh)[:, :, :L]
    lse = lse.reshape(N, H, nb * CHUNK)[:, :, :L]
    o = o.reshape(B, dilation, H, L, Dh).transpose(0, 3, 1, 2, 4).reshape(B, S, H, Dh)
    lse = lse.reshape(B, dilation, H, L).transpose(0, 3, 1, 2).reshape(B, S, H)
    return o, lse


def dilated_mixture(q, k, v, slopes):
    outs, lses = [], []
    for g, (window, dilation) in enumerate(DIL_PAIRS):
        hs = slice(g * HEADS_PER_DIL, (g + 1) * HEADS_PER_DIL)
        o, lse = dilated_branch(q[:, :, hs], k[:, :, hs], v[:, :, hs], window, dilation, slopes[hs])
        outs.append(o)
        lses.append(lse)
    O = jnp.stack(outs, 0)
    wts = jax.nn.softmax(jnp.stack(lses, 0), 0)
    return jnp.sum(wts[..., None].astype(O.dtype) * O, 0)


def memory_attention(q, mk, mv):
    s = jnp.einsum('bshd,bmhd->bhsm', q, mk).astype(jnp.float32) * ATT_SCALE
    p = jax.nn.softmax(s, -1).astype(mv.dtype)
    return jnp.einsum('bhsm,bmhd->bshd', p, mv)


def route(h, w_router, b_router):
    T = h.shape[0]
    probs = jax.nn.softmax((h @ w_router).astype(jnp.float32), -1)
    sel = probs + b_router.astype(jnp.float32)
    grp = sel.reshape(T, N_EXPERT_GROUPS, EXPERTS_PER_GROUP)
    grp_score = jnp.sum(lax.top_k(grp, TOP_K)[0], -1)
    g_idx = jnp.argmax(grp_score, -1).astype(jnp.int32)
    in_grp = jnp.take_along_axis(grp, g_idx[:, None, None], axis=1)[:, 0]
    _, local = lax.top_k(in_grp, TOP_K)
    expert = g_idx[:, None] * EXPERTS_PER_GROUP + local.astype(jnp.int32)
    gate = jnp.take_along_axis(probs, expert, 1)
    gate = gate / jnp.sum(gate, -1, keepdims=True)
    return expert, gate


def moe_ffn(h, expert, gate, w_gate, w_up, w_down):
    T, D = h.shape
    n_slots = T * TOP_K
    e_flat = expert.reshape(-1)
    g_flat = gate.reshape(-1)
    tok = jnp.arange(n_slots, dtype=jnp.int32) // TOP_K
    order = jnp.argsort(e_flat)
    e_s, tok_s, g_s = e_flat[order], tok[order], g_flat[order]
    counts = jnp.bincount(e_flat, length=N_EXPERTS)
    starts = jnp.cumsum(counts) - counts
    padded = (counts + MOE_BLOCK - 1) // MOE_BLOCK * MOE_BLOCK
    pad_ends = jnp.cumsum(padded)
    pad_starts = pad_ends - padded
    dest = pad_starts[e_s] + (jnp.arange(n_slots) - starts[e_s])
    n_pad = n_slots + N_EXPERTS * MOE_BLOCK
    n_blocks = n_pad // MOE_BLOCK
    slot_tok = jnp.zeros((n_pad,), jnp.int32).at[dest].set(tok_s)
    slot_gate = jnp.zeros((n_pad,), jnp.float32).at[dest].set(g_s)
    block_start = jnp.arange(n_blocks) * MOE_BLOCK
    block_expert = jnp.minimum(jnp.searchsorted(pad_ends, block_start, side='right'), N_EXPERTS - 1)
    xb = h[slot_tok].reshape(n_blocks, MOE_BLOCK, D)

    def expert_block(args):
        xblk, e = args
        return (jax.nn.silu(xblk @ w_gate[e]) * (xblk @ w_up[e])) @ w_down[e]

    yb = lax.map(expert_block, (xb, block_expert))
    y = yb.reshape(n_pad, D) * slot_gate[:, None].astype(h.dtype)
    return jnp.zeros_like(h).at[slot_tok].add(y)


def _normal(k, shape, scale):
    return jax.random.normal(k, shape, jnp.float32) * scale


def setup_inputs(seed: int = 0) -> dict:
    key = jax.random.key(seed)
    ks = jax.random.split(key, 24)
    D = D_MODEL
    return {
        "x": _normal(ks[0], (BATCH, SEQ, D), 1.0),
        "mem": _normal(ks[1], (BATCH, N_MEM, D), 1.0),
        "w_in_a": _normal(ks[2], (N_A, D, 2 * SG_WIDTH + MEM_WIDTH), D ** -0.5),
        "w_out_a": _normal(ks[3], (N_A, SG_WIDTH + MEM_WIDTH, D), (SG_WIDTH + MEM_WIDTH) ** -0.5 * BETA),
        "sg_ln_g": 1.0 + _normal(ks[4], (N_A, SG_WIDTH), 0.02),
        "sg_ln_b": _normal(ks[5], (N_A, SG_WIDTH), 0.02),
        "sg_w": _normal(ks[6], (N_A, N_SG, CHUNK, CHUNK), 0.5 * CHUNK ** -0.5),
        "sg_b": 1.0 + _normal(ks[7], (N_A, N_SG, CHUNK), 0.1),
        "w_in_b": _normal(ks[8], (N_B, D, DIL_Q_WIDTH + MEM_WIDTH), D ** -0.5),
        "w_out_b": _normal(ks[9], (N_B, DIL_OUT_WIDTH + MEM_WIDTH, D), (DIL_OUT_WIDTH + MEM_WIDTH) ** -0.5 * BETA),
        "w_k_shared": _normal(ks[10], (D, DIL_Q_WIDTH), D ** -0.5),
        "w_v_shared": _normal(ks[11], (D, DIL_Q_WIDTH), D ** -0.5 * BETA),
        "w_mem_k": _normal(ks[12], (DEPTH, D, MEM_WIDTH), D ** -0.5),
        "w_mem_v": _normal(ks[13], (DEPTH, D, MEM_WIDTH), D ** -0.5 * BETA),
        "ln1_g": 1.0 + _normal(ks[14], (DEPTH, D), 0.02),
        "ln1_b": _normal(ks[15], (DEPTH, D), 0.02),
        "ln2_g": 1.0 + _normal(ks[16], (DEPTH, D), 0.02),
        "ln2_b": _normal(ks[17], (DEPTH, D), 0.02),
        "w_router": _normal(ks[18], (D, N_EXPERTS), D ** -0.5),
        "b_router": _normal(ks[19], (N_EXPERTS,), 0.01),
        "w_gate": _normal(ks[20], (DEPTH, N_EXPERTS, D, D_EXPERT), D ** -0.5),
        "w_up": _normal(ks[21], (DEPTH, N_EXPERTS, D, D_EXPERT), D ** -0.5),
        "w_down": _normal(ks[22], (DEPTH, N_EXPERTS, D_EXPERT, D), D_EXPERT ** -0.5 * BETA),
    }


def reference(x, mem, w_in_a, w_out_a, sg_ln_g, sg_ln_b, sg_w, sg_b, w_in_b, w_out_b,
              w_k_shared, w_v_shared, w_mem_k, w_mem_v, ln1_g, ln1_b, ln2_g, ln2_b,
              w_router, b_router, w_gate, w_up, w_down):
    B, S, D = x.shape
    M = mem.shape[1]
    slopes = alibi_slopes()
    k_sh = None
    v_sh = None
    for layer in range(DEPTH):
        mk = (mem @ w_mem_k[layer]).reshape(B, M, MEM_HEADS, HEAD_DIM)
        mv = (mem @ w_mem_v[layer]).reshape(B, M, MEM_HEADS, HEAD_DIM)
        if layer < N_A:
            a = layer
            proj = x @ w_in_a[a]
            u = jax.nn.gelu(proj[..., :SG_WIDTH], approximate=False)
            gv = jax.nn.gelu(proj[..., SG_WIDTH:2 * SG_WIDTH], approximate=False)
            qm = proj[..., 2 * SG_WIDTH:].reshape(B, S, MEM_HEADS, HEAD_DIM)
            gv = layer_norm(gv, sg_ln_g[a], sg_ln_b[a])
            mix = spatial_gating(u.reshape(B, S, N_SG, HEAD_DIM), gv.reshape(B, S, N_SG, HEAD_DIM),
                                 sg_w[a], sg_b[a]).reshape(B, S, SG_WIDTH)
            mo = memory_attention(qm, mk, mv).reshape(B, S, MEM_WIDTH)
            att = jnp.concatenate([mix, mo], -1) @ w_out_a[a]
        else:
            bl = layer - N_A
            if bl == 0:
                k_sh = (x @ w_k_shared).reshape(B, S, N_DIL_HEADS, HEAD_DIM)
                v_sh = (x @ w_v_shared).reshape(B, S, N_DIL_HEADS, HEAD_DIM)
            proj = x @ w_in_b[bl]
            q = proj[..., :DIL_Q_WIDTH].reshape(B, S, N_DIL_HEADS, HEAD_DIM)
            qm = proj[..., DIL_Q_WIDTH:].reshape(B, S, MEM_HEADS, HEAD_DIM)
            mix = dilated_mixture(q, k_sh, v_sh, slopes).reshape(B, S, DIL_OUT_WIDTH)
            mo = memory_attention(qm, mk, mv).reshape(B, S, MEM_WIDTH)
            att = jnp.concatenate([mix, mo], -1) @ w_out_b[bl]
        x = layer_norm(ALPHA * x + att, ln1_g[layer], ln1_b[layer])
        h = x.reshape(B * S, D)
        expert, gate = route(h, w_router, b_router)
        ffn = moe_ffn(h, expert, gate, w_gate[layer], w_up[layer], w_down[layer]).reshape(B, S, D)
        x = layer_norm(ALPHA * x + ffn, ln2_g[layer], ln2_b[layer])
    return x
```

```python
import functools
import math

import jax
import jax.numpy as jnp
from jax import lax
from jax.experimental import pallas as pl
from jax.experimental.pallas import tpu as pltpu

D_MODEL = 1024
HEAD_DIM = 64
CHUNK = 128
N_SG = 12
SG_WIDTH = N_SG * HEAD_DIM
DIL_PAIRS = ((128, 1), (512, 4), (2048, 16))
HEADS_PER_DIL = 4
N_DIL_HEADS = HEADS_PER_DIL * len(DIL_PAIRS)
DIL_Q_WIDTH = N_DIL_HEADS * HEAD_DIM
DIL_OUT_WIDTH = HEADS_PER_DIL * HEAD_DIM
MEM_HEADS = 4
MEM_WIDTH = MEM_HEADS * HEAD_DIM
N_EXPERTS = 16
N_EXPERT_GROUPS = 4
EXPERTS_PER_GROUP = N_EXPERTS // N_EXPERT_GROUPS
TOP_K = 2
DEPTH = 2
ALPHA = (2 * DEPTH) ** 0.25
LN_EPS = 1e-5
ATT_SCALE = 1.0 / math.sqrt(HEAD_DIM)

LANES = 128
SUBLANES = 8
N_CHUNKS = D_MODEL // LANES

TS_MIX = 512
TM_ROUTE = 512
TM_DISPATCH = 512
TM_COMBINE = 256
TB_MOE = 256
VMEM_LIMIT = 56 * 1024 * 1024

F32 = jnp.float32
BF16 = jnp.bfloat16


def _dot(a, b):
    return jnp.dot(a, b, preferred_element_type=F32)


def _dot_nt(a, b):
    return lax.dot_general(a, b, (((1,), (1,)), ((), ())), preferred_element_type=F32)


def _layer_norm(x, g, b):
    mu = jnp.mean(x, -1, keepdims=True)
    xc = x - mu
    var = jnp.mean(xc * xc, -1, keepdims=True)
    return xc * lax.rsqrt(var + LN_EPS) * g + b


def _gelu(x):
    return 0.5 * x * (1.0 + lax.erf(x * (1.0 / math.sqrt(2.0))))


def _from_tiles(ref, n):
    return jnp.concatenate(
        [ref[pl.ds(c, n, stride=N_CHUNKS), :] for c in range(N_CHUNKS)], -1)


def _to_tiles(ref, val, n):
    for c in range(N_CHUNKS):
        ref[pl.ds(c, n, stride=N_CHUNKS), :] = val[:, c * LANES:(c + 1) * LANES]


def _from_chunks(ref):
    return jnp.concatenate([ref[c] for c in range(N_CHUNKS)], -1)


def _mem_attention(q, mk, mv):
    head = lax.broadcasted_iota(jnp.int32, (1, MEM_WIDTH), 1) // HEAD_DIM
    out = jnp.zeros(q.shape, F32)
    for h in range(MEM_HEADS):
        sel = head == h
        s = _dot_nt(q, jnp.where(sel, mk, jnp.zeros_like(mk)))
        m = jnp.max(s, -1, keepdims=True)
        p = jnp.exp(s - m)
        l = jnp.sum(p, -1, keepdims=True)
        out = out + _dot((p * (1.0 / l)).astype(BF16), jnp.where(sel, mv, jnp.zeros_like(mv)))
    return out


def _memkv_kernel(mem_ref, wk_ref, wv_ref, mk_ref, mv_ref):
    m = mem_ref[...].astype(BF16)
    mk_ref[...] = _dot(m, wk_ref[...].astype(BF16)).astype(BF16)
    mv_ref[...] = _dot(m, wv_ref[...].astype(BF16)).astype(BF16)


def _memkv(mem2d, w_mem_k, w_mem_v):
    n = mem2d.shape[0]
    return pl.pallas_call(
        _memkv_kernel,
        grid=(DEPTH,),
        in_specs=[
            pl.BlockSpec((n, D_MODEL), lambda l: (0, 0)),
            pl.BlockSpec((None, D_MODEL, MEM_WIDTH), lambda l: (l, 0, 0)),
            pl.BlockSpec((None, D_MODEL, MEM_WIDTH), lambda l: (l, 0, 0)),
        ],
        out_specs=[
            pl.BlockSpec((None, n, MEM_WIDTH), lambda l: (l, 0, 0)),
            pl.BlockSpec((None, n, MEM_WIDTH), lambda l: (l, 0, 0)),
        ],
        out_shape=[jax.ShapeDtypeStruct((DEPTH, n, MEM_WIDTH), BF16)] * 2,
        compiler_params=pltpu.CompilerParams(vmem_limit_bytes=VMEM_LIMIT),
        name="memkv",
    )(mem2d, w_mem_k, w_mem_v)


def _mixer_a_kernel(x_ref, win_ref, wout_ref, sgg_ref, sgb_ref, sgw_ref, sgbias_ref,
                    mk_ref, mv_ref, g1_ref, b1_ref, h_ref):
    ts = x_ref.shape[0]
    x = x_ref[...]
    proj = _dot(x.astype(BF16), win_ref[...])
    u = _gelu(proj[:, :SG_WIDTH])
    gv = _gelu(proj[:, SG_WIDTH:2 * SG_WIDTH])
    gv = _layer_norm(gv, sgg_ref[...], sgb_ref[...]).astype(BF16)

    row = lax.broadcasted_iota(jnp.int32, (CHUNK, CHUNK), 0)
    col = lax.broadcasted_iota(jnp.int32, (CHUNK, CHUNK), 1)
    ws = [jnp.where(row >= col, sgw_ref[g], 0.0).astype(BF16) for g in range(N_SG)]
    low_half = col < HEAD_DIM
    rows = []
    for c in range(ts // CHUNK):
        vc = gv[c * CHUNK:(c + 1) * CHUNK]
        parts = []
        for j in range(N_SG // 2):
            vp = vc[:, j * LANES:(j + 1) * LANES]
            parts.append(jnp.where(low_half, _dot(ws[2 * j], vp), _dot(ws[2 * j + 1], vp)))
        rows.append(jnp.concatenate(parts, -1) + sgbias_ref[...])
    mix = u * jnp.concatenate(rows, 0)

    qm = (proj[:, 2 * SG_WIDTH:] * ATT_SCALE).astype(BF16)
    mo = _mem_attention(qm, mk_ref[...], mv_ref[...])
    att = _dot(jnp.concatenate([mix, mo], -1).astype(BF16), wout_ref[...])
    h = _layer_norm(ALPHA * x + att, g1_ref[...], b1_ref[...])
    _to_tiles(h_ref, h, ts)


def _mixer_a(x, w_in, w_out, sg_g, sg_b, sg_w, sg_bias, mk, mv, g1, b1):
    b, s, d = x.shape
    nts = s // TS_MIX
    m = mk.shape[0] // b
    full = lambda *shape: pl.BlockSpec(shape, lambda i, j: (0,) * len(shape))
    return pl.pallas_call(
        _mixer_a_kernel,
        grid=(b, nts),
        in_specs=[
            pl.BlockSpec((None, TS_MIX, d), lambda i, j: (i, j, 0)),
            full(d, 2 * SG_WIDTH + MEM_WIDTH),
            full(SG_WIDTH + MEM_WIDTH, d),
            full(1, SG_WIDTH),
            full(1, SG_WIDTH),
            full(N_SG, CHUNK, CHUNK),
            full(CHUNK, SG_WIDTH),
            pl.BlockSpec((m, MEM_WIDTH), lambda i, j: (i, 0)),
            pl.BlockSpec((m, MEM_WIDTH), lambda i, j: (i, 0)),
            full(1, d),
            full(1, d),
        ],
        out_specs=pl.BlockSpec((TS_MIX * N_CHUNKS, LANES), lambda i, j: (i * nts + j, 0)),
        out_shape=jax.ShapeDtypeStruct((b * s * N_CHUNKS, LANES), F32),
        compiler_params=pltpu.CompilerParams(vmem_limit_bytes=VMEM_LIMIT),
        name="mixer_a",
    )(x, w_in, w_out, sg_g, sg_b, sg_w, sg_bias, mk, mv, g1, b1)


def _router_kernel(h_ref, wr_ref, br_ref, eidx_ref, gate_ref, rank_ref, counts_ref, base_ref):
    tm = TM_ROUTE

    @pl.when(pl.program_id(0) == 0)
    def _():
        base_ref[...] = jnp.zeros_like(base_ref)

    h = _from_tiles(h_ref, tm)
    w = wr_ref[...]
    h_hi = h.astype(BF16)
    h_lo = (h - h_hi.astype(F32)).astype(BF16)
    w_hi = w.astype(BF16)
    w_lo = (w - w_hi.astype(F32)).astype(BF16)
    logits = _dot(h_hi, w_hi) + (_dot(h_hi, w_lo) + _dot(h_lo, w_hi))

    ex = jnp.exp(logits - jnp.max(logits, -1, keepdims=True))
    probs = ex / jnp.sum(ex, -1, keepdims=True)
    sel = probs + br_ref[...]
    lane = lax.broadcasted_iota(jnp.int32, (tm, N_EXPERTS), 1)
    group = lane // EXPERTS_PER_GROUP

    def top2(mask):
        v = jnp.where(mask, sel, -jnp.inf)
        m1 = jnp.max(v, -1, keepdims=True)
        i1 = jnp.min(jnp.where(v == m1, lane, N_EXPERTS), -1, keepdims=True)
        v2 = jnp.where(lane == i1, -jnp.inf, v)
        m2 = jnp.max(v2, -1, keepdims=True)
        i2 = jnp.min(jnp.where(v2 == m2, lane, N_EXPERTS), -1, keepdims=True)
        return m1, i1, m2, i2

    scores = []
    for g in range(N_EXPERT_GROUPS):
        m1, _, m2, _ = top2(group == g)
        scores.append(m1 + m2)
    best = functools.reduce(jnp.maximum, scores)
    g_idx = jnp.full((tm, 1), N_EXPERT_GROUPS - 1, jnp.int32)
    for g in reversed(range(N_EXPERT_GROUPS - 1)):
        g_idx = jnp.where(scores[g] == best, g, g_idx)
    _, e0, _, e1 = top2(group == g_idx)

    hot0 = lane == e0
    hot1 = lane == e1
    p0 = jnp.sum(jnp.where(hot0, probs, 0.0), -1, keepdims=True)
    p1 = jnp.sum(jnp.where(hot1, probs, 0.0), -1, keepdims=True)
    psum = p0 + p1

    hot = jnp.where(hot0 | hot1, 1.0, 0.0)
    earlier = (lax.broadcasted_iota(jnp.int32, (tm, tm), 0)
               > lax.broadcasted_iota(jnp.int32, (tm, tm), 1))
    before = _dot(jnp.where(earlier, 1.0, 0.0).astype(BF16), hot.astype(BF16)) + base_ref[...]
    r0 = jnp.sum(jnp.where(hot0, before, 0.0), -1, keepdims=True)
    r1 = jnp.sum(jnp.where(hot1, before, 0.0), -1, keepdims=True)

    k_is0 = lax.broadcasted_iota(jnp.int32, (tm, TOP_K), 1) == 0
    eidx_ref[...] = jnp.where(k_is0, e0, e1)
    gate_ref[...] = jnp.where(k_is0, p0 / psum, p1 / psum)
    rank_ref[...] = jnp.where(k_is0, r0, r1).astype(jnp.int32)
    total = base_ref[...] + jnp.sum(hot, 0, keepdims=True)
    base_ref[...] = total
    counts_ref[...] = total.astype(jnp.int32)


def _router(h_tiles, w_router, b_router):
    t = h_tiles.shape[0] // N_CHUNKS
    tm = TM_ROUTE
    return pl.pallas_call(
        _router_kernel,
        grid=(t // tm,),
        in_specs=[
            pl.BlockSpec((tm * N_CHUNKS, LANES), lambda i: (i, 0)),
            pl.BlockSpec((D_MODEL, N_EXPERTS), lambda i: (0, 0)),
            pl.BlockSpec((1, N_EXPERTS), lambda i: (0, 0)),
        ],
        out_specs=[
            pl.BlockSpec((tm, TOP_K), lambda i: (i, 0)),
            pl.BlockSpec((tm, TOP_K), lambda i: (i, 0)),
            pl.BlockSpec((tm, TOP_K), lambda i: (i, 0)),
            pl.BlockSpec((1, N_EXPERTS), lambda i: (0, 0)),
        ],
        out_shape=[
            jax.ShapeDtypeStruct((t, TOP_K), jnp.int32),
            jax.ShapeDtypeStruct((t, TOP_K), F32),
            jax.ShapeDtypeStruct((t, TOP_K), jnp.int32),
            jax.ShapeDtypeStruct((1, N_EXPERTS), jnp.int32),
        ],
        scratch_shapes=[pltpu.VMEM((1, N_EXPERTS), F32)],
        compiler_params=pltpu.CompilerParams(
            dimension_semantics=("arbitrary",), vmem_limit_bytes=VMEM_LIMIT),
        name="router",
    )(h_tiles, w_router, b_router)


def _row_copy(src_ref, src_tok, dst_ref, dst_tok, sem):
    return pltpu.make_async_copy(
        src_ref.at[pl.ds(pl.multiple_of(src_tok * N_CHUNKS, N_CHUNKS), N_CHUNKS)],
        dst_ref.at[pl.ds(pl.multiple_of(dst_tok * N_CHUNKS, N_CHUNKS), N_CHUNKS)],
        sem)


def _dispatch_kernel(e_ref, r_ref, start_ref, end_ref, cnt_ref, h_ref, xs_ref,
                     zero_ref, sem, zsem):
    i = pl.program_id(0)
    n_slots = TM_DISPATCH * TOP_K

    @pl.when(i == 0)
    def _():
        zero_ref[...] = jnp.zeros_like(zero_ref)
        for e in range(N_EXPERTS):
            @pl.when(cnt_ref[e] > 0)
            def _():
                first = pl.multiple_of((end_ref[e] - TB_MOE) * N_CHUNKS, N_CHUNKS)
                cp = pltpu.make_async_copy(
                    zero_ref, xs_ref.at[pl.ds(first, TB_MOE * N_CHUNKS)], zsem)
                cp.start()
                cp.wait()

        def zero_block(blk, carry):
            first = pl.multiple_of(blk * (TB_MOE * N_CHUNKS), TB_MOE * N_CHUNKS)
            cp = pltpu.make_async_copy(
                zero_ref, xs_ref.at[pl.ds(first, TB_MOE * N_CHUNKS)], zsem)
            cp.start()
            cp.wait()
            return carry

        n_blocks = xs_ref.shape[0] // (TB_MOE * N_CHUNKS)
        lax.fori_loop(end_ref[N_EXPERTS - 1] // TB_MOE, n_blocks, zero_block, 0)

    def issue(n, carry):
        dst = start_ref[e_ref[0, n]] + r_ref[0, n]
        _row_copy(h_ref, i * TM_DISPATCH + n // TOP_K, xs_ref, dst, sem).start()
        return carry

    lax.fori_loop(0, n_slots, issue, 0)

    def drain(n, carry):
        _row_copy(h_ref, 0, xs_ref, 0, sem).wait()
        return carry

    lax.fori_loop(0, n_slots, drain, 0)


def _dispatch(eflat, rflat, pad_starts, pad_ends, counts, h_tiles, n_rows):
    t = h_tiles.shape[0] // N_CHUNKS
    nt = t // TM_DISPATCH
    n_slots = TM_DISPATCH * TOP_K
    smem_vec = pl.BlockSpec(memory_space=pltpu.SMEM)
    slot_spec = pl.BlockSpec((None, 1, n_slots), lambda i: (i, 0, 0), memory_space=pltpu.SMEM)
    return pl.pallas_call(
        _dispatch_kernel,
        grid=(nt,),
        in_specs=[slot_spec, slot_spec, smem_vec, smem_vec, smem_vec,
                  pl.BlockSpec(memory_space=pl.ANY)],
        out_specs=pl.BlockSpec(memory_space=pl.ANY),
        out_shape=jax.ShapeDtypeStruct((n_rows * N_CHUNKS, LANES), F32),
        scratch_shapes=[
            pltpu.VMEM((TB_MOE * N_CHUNKS, LANES), F32),
            pltpu.SemaphoreType.DMA(()),
            pltpu.SemaphoreType.DMA(()),
        ],
        compiler_params=pltpu.CompilerParams(
            dimension_semantics=("arbitrary",), has_side_effects=True),
        name="dispatch",
    )(eflat.reshape(nt, 1, n_slots), rflat.reshape(nt, 1, n_slots),
      pad_starts, pad_ends, counts, h_tiles)


def _experts_kernel(be_ref, nused_ref, xs_ref, wg_ref, wu_ref, wd_ref, ys_ref,
                    wg_bf, wu_bf, wd_bf):
    i = pl.program_id(0)
    nused = nused_ref[0]

    @pl.when(i < nused)
    def _():
        prev = be_ref[jnp.maximum(i - 1, 0)]

        @pl.when((i == 0) | (be_ref[i] != prev))
        def _():
            wg_bf[...] = wg_ref[...].astype(BF16)
            wu_bf[...] = wu_ref[...].astype(BF16)
            wd_bf[...] = wd_ref[...].astype(BF16)

        x = _from_tiles(xs_ref, TB_MOE).astype(BF16)
        hg = _dot(x, wg_bf[...])
        hu = _dot(x, wu_bf[...])
        act = (hg * jax.nn.sigmoid(hg) * hu).astype(BF16)
        _to_tiles(ys_ref, _dot(act, wd_bf[...]), TB_MOE)

    @pl.when(i >= nused)
    def _():
        ys_ref[...] = jnp.zeros_like(ys_ref)


def _experts(block_expert, nused, xs, w_gate, w_up, w_down):
    n_blocks = xs.shape[0] // (TB_MOE * N_CHUNKS)
    d, de = w_gate.shape[1], w_gate.shape[2]

    def x_map(i, be, nu):
        return (jnp.minimum(i, nu[0] - 1), 0)

    def w_map(i, be, nu):
        return (be[jnp.minimum(i, nu[0] - 1)], 0, 0)

    return pl.pallas_call(
        _experts_kernel,
        grid_spec=pltpu.PrefetchScalarGridSpec(
            num_scalar_prefetch=2,
            grid=(n_blocks,),
            in_specs=[
                pl.BlockSpec((TB_MOE * N_CHUNKS, LANES), x_map),
                pl.BlockSpec((None, d, de), w_map),
                pl.BlockSpec((None, d, de), w_map),
                pl.BlockSpec((None, de, d), w_map),
            ],
            out_specs=pl.BlockSpec((TB_MOE * N_CHUNKS, LANES), lambda i, be, nu: (i, 0)),
            scratch_shapes=[
                pltpu.VMEM((d, de), BF16),
                pltpu.VMEM((d, de), BF16),
                pltpu.VMEM((de, d), BF16),
            ],
        ),
        out_shape=jax.ShapeDtypeStruct(xs.shape, F32),
        compiler_params=pltpu.CompilerParams(
            dimension_semantics=("arbitrary",), vmem_limit_bytes=VMEM_LIMIT),
        name="experts",
    )(block_expert, nused, xs, w_gate, w_up, w_down)


def _combine_kernel(e_ref, r_ref, start_ref, h_ref, gate_ref, g2_ref, b2_ref, ys_ref,
                    out_ref, buf, sem, *, chunk_major):
    tm = TM_COMBINE
    n_slots = tm * TOP_K

    def issue(n, carry):
        src = start_ref[e_ref[0, n]] + r_ref[0, n]
        _row_copy(ys_ref, src, buf.at[n % TOP_K], n // TOP_K, sem).start()
        return carry

    lax.fori_loop(0, n_slots, issue, 0)

    def drain(n, carry):
        _row_copy(ys_ref, 0, buf.at[0], 0, sem).wait()
        return carry

    lax.fori_loop(0, n_slots, drain, 0)

    gate = gate_ref[...]
    ffn = (_from_tiles(buf.at[0], tm) * gate[:, 0:1]
           + _from_tiles(buf.at[1], tm) * gate[:, 1:2])
    out = _layer_norm(ALPHA * _from_tiles(h_ref, tm) + ffn, g2_ref[...], b2_ref[...])
    if chunk_major:
        for c in range(N_CHUNKS):
            out_ref[c] = out[:, c * LANES:(c + 1) * LANES]
    else:
        out_ref[...] = out


def _combine(eflat, rflat, pad_starts, h_tiles, gate, g2, b2, ys, batch, seq, chunk_major):
    tm = TM_COMBINE
    t = batch * seq
    nt = t // tm
    nts = seq // tm
    n_slots = tm * TOP_K
    slot_spec = pl.BlockSpec((None, 1, n_slots), lambda i: (i, 0, 0), memory_space=pltpu.SMEM)
    if chunk_major:
        out_spec = pl.BlockSpec((None, N_CHUNKS, tm, LANES), lambda i: (i // nts, 0, i % nts, 0))
        out_shape = jax.ShapeDtypeStruct((batch, N_CHUNKS, seq, LANES), F32)
    else:
        out_spec = pl.BlockSpec((None, tm, D_MODEL), lambda i: (i // nts, i % nts, 0))
        out_shape = jax.ShapeDtypeStruct((batch, seq, D_MODEL), F32)
    return pl.pallas_call(
        functools.partial(_combine_kernel, chunk_major=chunk_major),
        grid=(nt,),
        in_specs=[
            slot_spec, slot_spec,
            pl.BlockSpec(memory_space=pltpu.SMEM),
            pl.BlockSpec((tm * N_CHUNKS, LANES), lambda i: (i, 0)),
            pl.BlockSpec((tm, TOP_K), lambda i: (i, 0)),
            pl.BlockSpec((1, D_MODEL), lambda i: (0, 0)),
            pl.BlockSpec((1, D_MODEL), lambda i: (0, 0)),
            pl.BlockSpec(memory_space=pl.ANY),
        ],
        out_specs=out_spec,
        out_shape=out_shape,
        scratch_shapes=[
            pltpu.VMEM((TOP_K, tm * N_CHUNKS, LANES), F32),
            pltpu.SemaphoreType.DMA(()),
        ],
        compiler_params=pltpu.CompilerParams(
            dimension_semantics=("arbitrary",), vmem_limit_bytes=VMEM_LIMIT),
        name="combine",
    )(eflat.reshape(nt, 1, n_slots), rflat.reshape(nt, 1, n_slots), pad_starts,
      h_tiles, gate, g2, b2, ys)


def _moe(h_tiles, batch, seq, w_router, b_router, w_gate, w_up, w_down, g2, b2, chunk_major):
    t = batch * seq
    eidx, gate, rank, counts = _router(h_tiles, w_router, b_router)
    counts = counts[0]
    padded = (counts + TB_MOE - 1) // TB_MOE * TB_MOE
    pad_ends = jnp.cumsum(padded).astype(jnp.int32)
    pad_starts = pad_ends - padded
    n_rows = t * TOP_K + N_EXPERTS * TB_MOE
    n_blocks = n_rows // TB_MOE
    block_start = jnp.arange(n_blocks, dtype=jnp.int32) * TB_MOE
    block_expert = jnp.minimum(
        jnp.searchsorted(pad_ends, block_start, side="right"), N_EXPERTS - 1).astype(jnp.int32)
    nused = (pad_ends[-1:] // TB_MOE).astype(jnp.int32)
    eflat = eidx.reshape(-1)
    rflat = rank.reshape(-1)
    xs = _dispatch(eflat, rflat, pad_starts, pad_ends, counts, h_tiles, n_rows)
    ys = _experts(block_expert, nused, xs, w_gate, w_up, w_down)
    return _combine(eflat, rflat, pad_starts, h_tiles, gate, g2, b2, ys, batch, seq, chunk_major)


def _residue_rows(x_ref, dilation):
    n = x_ref.shape[1]
    if dilation == 1:
        return _from_chunks(x_ref)
    per = n // dilation
    return jnp.concatenate(
        [jnp.concatenate([x_ref[c, pl.ds(r, per, stride=dilation), :] for r in range(dilation)], 0)
         for c in range(N_CHUNKS)], -1)


def _proj_b_kernel(x_ref, wqkv_ref, wqm_ref, qkv0_ref, qkv1_ref, qkv2_ref, qm_ref):
    ts = x_ref.shape[1]
    qscale = jnp.where(
        lax.broadcasted_iota(jnp.int32, (1, 3 * DIL_OUT_WIDTH), 1) < DIL_OUT_WIDTH, ATT_SCALE, 1.0)
    for g, (out_ref, (_, dilation)) in enumerate(zip((qkv0_ref, qkv1_ref, qkv2_ref), DIL_PAIRS)):
        xb = _residue_rows(x_ref, dilation).astype(BF16)
        qkv = (_dot(xb, wqkv_ref[g]) * qscale).astype(BF16)
        per = ts // dilation
        for r in range(dilation):
            out_ref[r] = qkv[r * per:(r + 1) * per]
    qm = _dot(_from_chunks(x_ref).astype(BF16), wqm_ref[...])
    qm_ref[...] = (qm * ATT_SCALE).astype(BF16)


def _proj_b(x_cm, wqkv, wqm):
    b, _, s, _ = x_cm.shape
    ts = TS_MIX
    width = 3 * DIL_OUT_WIDTH
    out_specs, out_shapes = [], []
    for _, dilation in DIL_PAIRS:
        out_specs.append(pl.BlockSpec((None, dilation, ts // dilation, width),
                                      lambda i, j: (i, 0, j, 0)))
        out_shapes.append(jax.ShapeDtypeStruct((b, dilation, s // dilation, width), BF16))
    out_specs.append(pl.BlockSpec((None, ts, MEM_WIDTH), lambda i, j: (i, j, 0)))
    out_shapes.append(jax.ShapeDtypeStruct((b, s, MEM_WIDTH), BF16))
    return pl.pallas_call(
        _proj_b_kernel,
        grid=(b, s // ts),
        in_specs=[
            pl.BlockSpec((None, N_CHUNKS, ts, LANES), lambda i, j: (i, 0, j, 0)),
            pl.BlockSpec(wqkv.shape, lambda i, j: (0, 0, 0)),
            pl.BlockSpec(wqm.shape, lambda i, j: (0, 0)),
        ],
        out_specs=out_specs,
        out_shape=out_shapes,
        compiler_params=pltpu.CompilerParams(vmem_limit_bytes=VMEM_LIMIT),
        name="proj_b",
    )(x_cm, wqkv, wqm)


def _dil_attn_kernel(q_ref, kp_ref, kc_ref, vp_ref, vc_ref, o_ref, lse_ref, *, dilation, slopes):
    jb = pl.program_id(2)
    q = q_ref[...]
    kband = jnp.concatenate([kp_ref[...], kc_ref[...]], 0)
    vband = jnp.concatenate([vp_ref[...], vc_ref[...]], 0)
    iq = lax.broadcasted_iota(jnp.int32, (CHUNK, 2 * CHUNK), 0)
    jk = lax.broadcasted_iota(jnp.int32, (CHUNK, 2 * CHUNK), 1)
    rel = CHUNK + iq - jk
    valid = (rel >= 0) & (rel <= CHUNK) & ((jk >= CHUNK) | (jb > 0))
    dist = (dilation * rel).astype(F32)
    head = lax.broadcasted_iota(jnp.int32, (1, DIL_OUT_WIDTH), 1) // HEAD_DIM
    out = jnp.zeros((CHUNK, DIL_OUT_WIDTH), F32)
    lse = jnp.zeros((CHUNK, DIL_OUT_WIDTH), F32)
    for h in range(HEADS_PER_DIL):
        sel = head == h
        s = _dot_nt(jnp.where(sel, q, jnp.zeros_like(q)), kband)
        s = jnp.where(valid, s - slopes[h] * dist, -jnp.inf)
        m = jnp.max(s, -1, keepdims=True)
        p = jnp.exp(s - m)
        l = jnp.sum(p, -1, keepdims=True)
        out = out + _dot((p * (1.0 / l)).astype(BF16), jnp.where(sel, vband, jnp.zeros_like(vband)))
        lse = jnp.where(sel, m + jnp.log(l), lse)
    o_ref[...] = out
    lse_ref[...] = lse


def _dil_attn(qkv, group):
    b, dilation, length, _ = qkv.shape
    slopes = tuple(2.0 ** (-8.0 * (group * HEADS_PER_DIL + h + 1) / N_DIL_HEADS)
                   for h in range(HEADS_PER_DIL))
    blk = (None, None, CHUNK, DIL_OUT_WIDTH)
    cur = lambda col: pl.BlockSpec(blk, lambda i, r, j: (i, r, j, col))
    prev = lambda col: pl.BlockSpec(blk, lambda i, r, j: (i, r, jnp.maximum(j - 1, 0), col))
    out_spec = pl.BlockSpec(blk, lambda i, r, j: (i, r, j, 0))
    out_shape = jax.ShapeDtypeStruct((b, dilation, length, DIL_OUT_WIDTH), F32)
    return pl.pallas_call(
        functools.partial(_dil_attn_kernel, dilation=dilation, slopes=slopes),
        grid=(b, dilation, length // CHUNK),
        in_specs=[cur(0), prev(1), cur(1), prev(2), cur(2)],
        out_specs=[out_spec, out_spec],
        out_shape=[out_shape, out_shape],
        compiler_params=pltpu.CompilerParams(vmem_limit_bytes=VMEM_LIMIT),
        name=f"dil_attn_{group}",
    )(qkv, qkv, qkv, qkv, qkv)


def _token_order(src_ref, scratch, dilation):
    if dilation == 1:
        return src_ref[0]
    per = src_ref.shape[1]
    halves = DIL_OUT_WIDTH // LANES
    for r in range(dilation):
        v = src_ref[r]
        for c in range(halves):
            scratch[c, pl.ds(r, per, stride=dilation), :] = v[:, c * LANES:(c + 1) * LANES]
    return jnp.concatenate([scratch[c] for c in range(halves)], -1)


def _mixer_b_kernel(x_ref, qm_ref, o0_ref, l0_ref, o1_ref, l1_ref, o2_ref, l2_ref,
                    mk_ref, mv_ref, wout_ref, g1_ref, b1_ref, h_ref,
                    so1, sl1, so2, sl2):
    ts = x_ref.shape[1]
    outs = [o0_ref[0], _token_order(o1_ref, so1, DIL_PAIRS[1][1]),
            _token_order(o2_ref, so2, DIL_PAIRS[2][1])]
    lses = [l0_ref[0], _token_order(l1_ref, sl1, DIL_PAIRS[1][1]),
            _token_order(l2_ref, sl2, DIL_PAIRS[2][1])]
    top = functools.reduce(jnp.maximum, lses)
    es = [jnp.exp(l - top) for l in lses]
    inv = 1.0 / functools.reduce(jnp.add, es)
    mix = functools.reduce(jnp.add, [e * inv * o for e, o in zip(es, outs)])
    mo = _mem_attention(qm_ref[...], mk_ref[...], mv_ref[...])
    att = _dot(jnp.concatenate([mix, mo], -1).astype(BF16), wout_ref[...])
    h = _layer_norm(ALPHA * _from_chunks(x_ref) + att, g1_ref[...], b1_ref[...])
    _to_tiles(h_ref, h, ts)


def _mixer_b(x_cm, qm, attn, mk, mv, w_out, g1, b1):
    b, _, s, _ = x_cm.shape
    ts = TS_MIX
    nts = s // ts
    m = mk.shape[0] // b
    in_specs = [
        pl.BlockSpec((None, N_CHUNKS, ts, LANES), lambda i, j: (i, 0, j, 0)),
        pl.BlockSpec((None, ts, MEM_WIDTH), lambda i, j: (i, j, 0)),
    ]
    args = [x_cm, qm]
    for (o, lse), (_, dilation) in zip(attn, DIL_PAIRS):
        spec = pl.BlockSpec((None, dilation, ts // dilation, DIL_OUT_WIDTH),
                            lambda i, j: (i, 0, j, 0))
        in_specs += [spec, spec]
        args += [o, lse]
    in_specs += [
        pl.BlockSpec((m, MEM_WIDTH), lambda i, j: (i, 0)),
        pl.BlockSpec((m, MEM_WIDTH), lambda i, j: (i, 0)),
        pl.BlockSpec(w_out.shape, lambda i, j: (0, 0)),
        pl.BlockSpec((1, D_MODEL), lambda i, j: (0, 0)),
        pl.BlockSpec((1, D_MODEL), lambda i, j: (0, 0)),
    ]
    args += [mk, mv, w_out, g1, b1]
    halves = DIL_OUT_WIDTH // LANES
    return pl.pallas_call(
        _mixer_b_kernel,
        grid=(b, nts),
        in_specs=in_specs,
        out_specs=pl.BlockSpec((ts * N_CHUNKS, LANES), lambda i, j: (i * nts + j, 0)),
        out_shape=jax.ShapeDtypeStruct((b * s * N_CHUNKS, LANES), F32),
        scratch_shapes=[pltpu.VMEM((halves, ts, LANES), F32)] * 4,
        compiler_params=pltpu.CompilerParams(vmem_limit_bytes=VMEM_LIMIT),
        name="mixer_b",
    )(*args)


def kernel(x, mem, w_in_a, w_out_a, sg_ln_g, sg_ln_b, sg_w, sg_b, w_in_b, w_out_b, w_k_shared,
           w_v_shared, w_mem_k, w_mem_v, ln1_g, ln1_b, ln2_g, ln2_b, w_router, b_router,
           w_gate, w_up, w_down):
    batch, seq, d = x.shape
    mk, mv = _memkv(mem.reshape(-1, d), w_mem_k, w_mem_v)
    row = lambda v: v.reshape(1, -1)

    sg_bias = jnp.repeat(sg_b[0].T, HEAD_DIM, axis=1)
    h = _mixer_a(x, w_in_a[0].astype(BF16), w_out_a[0].astype(BF16), row(sg_ln_g[0]),
                 row(sg_ln_b[0]), sg_w[0], sg_bias, mk[0], mv[0], row(ln1_g[0]), row(ln1_b[0]))
    x_cm = _moe(h, batch, seq, w_router, row(b_router), w_gate[0], w_up[0], w_down[0],
                row(ln2_g[0]), row(ln2_b[0]), chunk_major=True)

    wqkv = jnp.stack([
        jnp.concatenate([w[:, g * DIL_OUT_WIDTH:(g + 1) * DIL_OUT_WIDTH]
                         for w in (w_in_b[0], w_k_shared, w_v_shared)], -1)
        for g in range(len(DIL_PAIRS))]).astype(BF16)
    wqm = w_in_b[0][:, DIL_Q_WIDTH:].astype(BF16)
    qkv0, qkv1, qkv2, qm = _proj_b(x_cm, wqkv, wqm)
    attn = [_dil_attn(qkv, g) for g, qkv in enumerate((qkv0, qkv1, qkv2))]
    h = _mixer_b(x_cm, qm, attn, mk[1], mv[1], w_out_b[0].astype(BF16),
                 row(ln1_g[1]), row(ln1_b[1]))
    return _moe(h, batch, seq, w_router, row(b_router), w_gate[1], w_up[1], w_down[1],
                row(ln2_g[1]), row(ln2_b[1]), chunk_major=False)
```

```python
import functools
import math

import jax
import jax.numpy as jnp
from jax import lax
from jax.experimental import pallas as pl
from jax.experimental.pallas import tpu as pltpu

D_MODEL = 1024
HEAD_DIM = 64
CHUNK = 128
N_SG = 12
SG_WIDTH = N_SG * HEAD_DIM
DIL_PAIRS = ((128, 1), (512, 4), (2048, 16))
HEADS_PER_DIL = 4
N_DIL_HEADS = HEADS_PER_DIL * len(DIL_PAIRS)
DIL_Q_WIDTH = N_DIL_HEADS * HEAD_DIM
DIL_OUT_WIDTH = HEADS_PER_DIL * HEAD_DIM
MEM_HEADS = 4
MEM_WIDTH = MEM_HEADS * HEAD_DIM
N_EXPERTS = 16
N_EXPERT_GROUPS = 4
EXPERTS_PER_GROUP = N_EXPERTS // N_EXPERT_GROUPS
TOP_K = 2
DEPTH = 2
ALPHA = (2 * DEPTH) ** 0.25
LN_EPS = 1e-5
ATT_SCALE = 1.0 / math.sqrt(HEAD_DIM)

LANES = 128
SUBLANES = 8
N_CHUNKS = D_MODEL // LANES

TS_MIX = 512
TM_ROUTE = 512
TM_DISPATCH = 512
TM_COMBINE = 256
TB_MOE = 256
VMEM_LIMIT = 56 * 1024 * 1024

F32 = jnp.float32
BF16 = jnp.bfloat16


def _dot(a, b):
    return jnp.dot(a, b, preferred_element_type=F32)


def _dot_nt(a, b):
    return lax.dot_general(a, b, (((1,), (1,)), ((), ())), preferred_element_type=F32)


def _layer_norm(x, g, b):
    mu = jnp.mean(x, -1, keepdims=True)
    xc = x - mu
    var = jnp.mean(xc * xc, -1, keepdims=True)
    return xc * lax.rsqrt(var + LN_EPS) * g + b


def _gelu(x):
    return 0.5 * x * (1.0 + lax.erf(x * (1.0 / math.sqrt(2.0))))


def _from_tiles(ref, n):
    return jnp.concatenate(
        [ref[pl.ds(c, n, stride=N_CHUNKS), :] for c in range(N_CHUNKS)], -1)


def _to_tiles(ref, val, n):
    for c in range(N_CHUNKS):
        ref[pl.ds(c, n, stride=N_CHUNKS), :] = val[:, c * LANES:(c + 1) * LANES]


def _from_chunks(ref):
    return jnp.concatenate([ref[c] for c in range(N_CHUNKS)], -1)


def _mem_attention(q, mk, mv):
    head = lax.broadcasted_iota(jnp.int32, (1, MEM_WIDTH), 1) // HEAD_DIM
    out = jnp.zeros(q.shape, F32)
    for h in range(MEM_HEADS):
        sel = head == h
        s = _dot_nt(q, jnp.where(sel, mk, jnp.zeros_like(mk)))
        m = jnp.max(s, -1, keepdims=True)
        p = jnp.exp(s - m)
        l = jnp.sum(p, -1, keepdims=True)
        out = out + _dot((p * (1.0 / l)).astype(BF16), jnp.where(sel, mv, jnp.zeros_like(mv)))
    return out


def _memkv_kernel(mem_ref, wk_ref, wv_ref, mk_ref, mv_ref):
    m = mem_ref[...].astype(BF16)
    mk_ref[...] = _dot(m, wk_ref[...].astype(BF16)).astype(BF16)
    mv_ref[...] = _dot(m, wv_ref[...].astype(BF16)).astype(BF16)


def _memkv(mem2d, w_mem_k, w_mem_v):
    n = mem2d.shape[0]
    return pl.pallas_call(
        _memkv_kernel,
        grid=(DEPTH,),
        in_specs=[
            pl.BlockSpec((n, D_MODEL), lambda l: (0, 0)),
            pl.BlockSpec((None, D_MODEL, MEM_WIDTH), lambda l: (l, 0, 0)),
            pl.BlockSpec((None, D_MODEL, MEM_WIDTH), lambda l: (l, 0, 0)),
        ],
        out_specs=[
            pl.BlockSpec((None, n, MEM_WIDTH), lambda l: (l, 0, 0)),
            pl.BlockSpec((None, n, MEM_WIDTH), lambda l: (l, 0, 0)),
        ],
        out_shape=[jax.ShapeDtypeStruct((DEPTH, n, MEM_WIDTH), BF16)] * 2,
        compiler_params=pltpu.CompilerParams(vmem_limit_bytes=VMEM_LIMIT),
        name="memkv",
    )(mem2d, w_mem_k, w_mem_v)


def _mixer_a_kernel(x_ref, win_ref, wout_ref, sgg_ref, sgb_ref, sgw_ref, sgbias_ref,
                    mk_ref, mv_ref, g1_ref, b1_ref, h_ref):
    ts = x_ref.shape[0]
    x = x_ref[...]
    proj = _dot(x.astype(BF16), win_ref[...])
    u = _gelu(proj[:, :SG_WIDTH])
    gv = _gelu(proj[:, SG_WIDTH:2 * SG_WIDTH])
    gv = _layer_norm(gv, sgg_ref[...], sgb_ref[...]).astype(BF16)

    row = lax.broadcasted_iota(jnp.int32, (CHUNK, CHUNK), 0)
    col = lax.broadcasted_iota(jnp.int32, (CHUNK, CHUNK), 1)
    ws = [jnp.where(row >= col, sgw_ref[g], 0.0).astype(BF16) for g in range(N_SG)]
    low_half = col < HEAD_DIM
    rows = []
    for c in range(ts // CHUNK):
        vc = gv[c * CHUNK:(c + 1) * CHUNK]
        parts = []
        for j in range(N_SG // 2):
            vp = vc[:, j * LANES:(j + 1) * LANES]
            parts.append(jnp.where(low_half, _dot(ws[2 * j], vp), _dot(ws[2 * j + 1], vp)))
        rows.append(jnp.concatenate(parts, -1) + sgbias_ref[...])
    mix = u * jnp.concatenate(rows, 0)

    qm = (proj[:, 2 * SG_WIDTH:] * ATT_SCALE).astype(BF16)
    mo = _mem_attention(qm, mk_ref[...], mv_ref[...])
    att = _dot(jnp.concatenate([mix, mo], -1).astype(BF16), wout_ref[...])
    h = _layer_norm(ALPHA * x + att, g1_ref[...], b1_ref[...])
    _to_tiles(h_ref, h, ts)


def _mixer_a(x, w_in, w_out, sg_g, sg_b, sg_w, sg_bias, mk, mv, g1, b1):
    b, s, d = x.shape
    nts = s // TS_MIX
    m = mk.shape[0] // b
    full = lambda *shape: pl.BlockSpec(shape, lambda i, j: (0,) * len(shape))
    return pl.pallas_call(
        _mixer_a_kernel,
        grid=(b, nts),
        in_specs=[
            pl.BlockSpec((None, TS_MIX, d), lambda i, j: (i, j, 0)),
            full(d, 2 * SG_WIDTH + MEM_WIDTH),
            full(SG_WIDTH + MEM_WIDTH, d),
            full(1, SG_WIDTH),
            full(1, SG_WIDTH),
            full(N_SG, CHUNK, CHUNK),
            full(CHUNK, SG_WIDTH),
            pl.BlockSpec((m, MEM_WIDTH), lambda i, j: (i, 0)),
            pl.BlockSpec((m, MEM_WIDTH), lambda i, j: (i, 0)),
            full(1, d),
            full(1, d),
        ],
        out_specs=pl.BlockSpec((TS_MIX * N_CHUNKS, LANES), lambda i, j: (i * nts + j, 0)),
        out_shape=jax.ShapeDtypeStruct((b * s * N_CHUNKS, LANES), F32),
        compiler_params=pltpu.CompilerParams(vmem_limit_bytes=VMEM_LIMIT),
        name="mixer_a",
    )(x, w_in, w_out, sg_g, sg_b, sg_w, sg_bias, mk, mv, g1, b1)


def _router_kernel(h_ref, wr_ref, br_ref, eidx_ref, gate_ref, rank_ref, counts_ref, base_ref):
    tm = TM_ROUTE

    @pl.when(pl.program_id(0) == 0)
    def _():
        base_ref[...] = jnp.zeros_like(base_ref)

    h = _from_tiles(h_ref, tm)
    w = wr_ref[...]
    h_hi = h.astype(BF16)
    h_lo = (h - h_hi.astype(F32)).astype(BF16)
    w_hi = w.astype(BF16)
    w_lo = (w - w_hi.astype(F32)).astype(BF16)
    logits = _dot(h_hi, w_hi) + (_dot(h_hi, w_lo) + _dot(h_lo, w_hi))

    ex = jnp.exp(logits - jnp.max(logits, -1, keepdims=True))
    probs = ex / jnp.sum(ex, -1, keepdims=True)
    sel = probs + br_ref[...]
    lane = lax.broadcasted_iota(jnp.int32, (tm, N_EXPERTS), 1)
    group = lane // EXPERTS_PER_GROUP

    def top2(mask):
        v = jnp.where(mask, sel, -jnp.inf)
        m1 = jnp.max(v, -1, keepdims=True)
        i1 = jnp.min(jnp.where(v == m1, lane, N_EXPERTS), -1, keepdims=True)
        v2 = jnp.where(lane == i1, -jnp.inf, v)
        m2 = jnp.max(v2, -1, keepdims=True)
        i2 = jnp.min(jnp.where(v2 == m2, lane, N_EXPERTS), -1, keepdims=True)
        return m1, i1, m2, i2

    scores = []
    for g in range(N_EXPERT_GROUPS):
        m1, _, m2, _ = top2(group == g)
        scores.append(m1 + m2)
    best = functools.reduce(jnp.maximum, scores)
    g_idx = jnp.full((tm, 1), N_EXPERT_GROUPS - 1, jnp.int32)
    for g in reversed(range(N_EXPERT_GROUPS - 1)):
        g_idx = jnp.where(scores[g] == best, g, g_idx)
    _, e0, _, e1 = top2(group == g_idx)

    hot0 = lane == e0
    hot1 = lane == e1
    p0 = jnp.sum(jnp.where(hot0, probs, 0.0), -1, keepdims=True)
    p1 = jnp.sum(jnp.where(hot1, probs, 0.0), -1, keepdims=True)
    psum = p0 + p1

    hot = jnp.where(hot0 | hot1, 1.0, 0.0)
    earlier = (lax.broadcasted_iota(jnp.int32, (tm, tm), 0)
               > lax.broadcasted_iota(jnp.int32, (tm, tm), 1))
    before = _dot(jnp.where(earlier, 1.0, 0.0).astype(BF16), hot.astype(BF16)) + base_ref[...]
    r0 = jnp.sum(jnp.where(hot0, before, 0.0), -1, keepdims=True)
    r1 = jnp.sum(jnp.where(hot1, before, 0.0), -1, keepdims=True)

    k_is0 = lax.broadcasted_iota(jnp.int32, (tm, TOP_K), 1) == 0
    eidx_ref[...] = jnp.where(k_is0, e0, e1)
    gate_ref[...] = jnp.where(k_is0, p0 / psum, p1 / psum)
    rank_ref[...] = jnp.where(k_is0, r0, r1).astype(jnp.int32)
    total = base_ref[...] + jnp.sum(hot, 0, keepdims=True)
    base_ref[...] = total
    counts_ref[...] = total.astype(jnp.int32)


def _router(h_tiles, w_router, b_router):
    t = h_tiles.shape[0] // N_CHUNKS
    tm = TM_ROUTE
    return pl.pallas_call(
        _router_kernel,
        grid=(t // tm,),
        in_specs=[
            pl.BlockSpec((tm * N_CHUNKS, LANES), lambda i: (i, 0)),
            pl.BlockSpec((D_MODEL, N_EXPERTS), lambda i: (0, 0)),
            pl.BlockSpec((1, N_EXPERTS), lambda i: (0, 0)),
        ],
        out_specs=[
            pl.BlockSpec((tm, TOP_K), lambda i: (i, 0)),
            pl.BlockSpec((tm, TOP_K), lambda i: (i, 0)),
            pl.BlockSpec((tm, TOP_K), lambda i: (i, 0)),
            pl.BlockSpec((1, N_EXPERTS), lambda i: (0, 0)),
        ],
        out_shape=[
            jax.ShapeDtypeStruct((t, TOP_K), jnp.int32),
            jax.ShapeDtypeStruct((t, TOP_K), F32),
            jax.ShapeDtypeStruct((t, TOP_K), jnp.int32),
            jax.ShapeDtypeStruct((1, N_EXPERTS), jnp.int32),
        ],
        scratch_shapes=[pltpu.VMEM((1, N_EXPERTS), F32)],
        compiler_params=pltpu.CompilerParams(
            dimension_semantics=("arbitrary",), vmem_limit_bytes=VMEM_LIMIT),
        name="router",
    )(h_tiles, w_router, b_router)


def _row_copy(src_ref, src_tok, dst_ref, dst_tok, sem):
    return pltpu.make_async_copy(
        src_ref.at[pl.ds(pl.multiple_of(src_tok * N_CHUNKS, N_CHUNKS), N_CHUNKS)],
        dst_ref.at[pl.ds(pl.multiple_of(dst_tok * N_CHUNKS, N_CHUNKS), N_CHUNKS)],
        sem)


def _dispatch_kernel(e_ref, r_ref, start_ref, end_ref, cnt_ref, h_ref, xs_ref,
                     zero_ref, sem, zsem):
    i = pl.program_id(0)
    n_slots = TM_DISPATCH * TOP_K

    @pl.when(i == 0)
    def _():
        zero_ref[...] = jnp.zeros_like(zero_ref)
        for e in range(N_EXPERTS):
            @pl.when(cnt_ref[e] > 0)
            def _():
                first = pl.multiple_of((end_ref[e] - TB_MOE) * N_CHUNKS, N_CHUNKS)
                cp = pltpu.make_async_copy(
                    zero_ref, xs_ref.at[pl.ds(first, TB_MOE * N_CHUNKS)], zsem)
                cp.start()
                cp.wait()

        def zero_block(blk, carry):
            first = pl.multiple_of(blk * (TB_MOE * N_CHUNKS), TB_MOE * N_CHUNKS)
            cp = pltpu.make_async_copy(
                zero_ref, xs_ref.at[pl.ds(first, TB_MOE * N_CHUNKS)], zsem)
            cp.start()
            cp.wait()
            return carry

        n_blocks = xs_ref.shape[0] // (TB_MOE * N_CHUNKS)
        lax.fori_loop(end_ref[N_EXPERTS - 1] // TB_MOE, n_blocks, zero_block, 0)

    def issue(tok, carry):
        for k in range(TOP_K):
            n = tok * TOP_K + k
            dst = start_ref[e_ref[0, n]] + r_ref[0, n]
            _row_copy(h_ref, tok, xs_ref, dst, sem).start()
        return carry

    lax.fori_loop(0, TM_DISPATCH, issue, 0, unroll=4)

    def drain(n, carry):
        _row_copy(h_ref, 0, xs_ref, 0, sem).wait()
        return carry

    lax.fori_loop(0, n_slots, drain, 0, unroll=8)


def _dispatch(eflat, rflat, pad_starts, pad_ends, counts, h_tiles, n_rows):
    t = h_tiles.shape[0] // N_CHUNKS
    nt = t // TM_DISPATCH
    n_slots = TM_DISPATCH * TOP_K
    smem_vec = pl.BlockSpec(memory_space=pltpu.SMEM)
    slot_spec = pl.BlockSpec((None, 1, n_slots), lambda i: (i, 0, 0), memory_space=pltpu.SMEM)
    return pl.pallas_call(
        _dispatch_kernel,
        grid=(nt,),
        in_specs=[slot_spec, slot_spec, smem_vec, smem_vec, smem_vec,
                  pl.BlockSpec((TM_DISPATCH * N_CHUNKS, LANES), lambda i: (i, 0))],
        out_specs=pl.BlockSpec(memory_space=pl.ANY),
        out_shape=jax.ShapeDtypeStruct((n_rows * N_CHUNKS, LANES), F32),
        scratch_shapes=[
            pltpu.VMEM((TB_MOE * N_CHUNKS, LANES), F32),
            pltpu.SemaphoreType.DMA(()),
            pltpu.SemaphoreType.DMA(()),
        ],
        compiler_params=pltpu.CompilerParams(
            dimension_semantics=("arbitrary",), has_side_effects=True,
            vmem_limit_bytes=VMEM_LIMIT),
        name="dispatch",
    )(eflat.reshape(nt, 1, n_slots), rflat.reshape(nt, 1, n_slots),
      pad_starts, pad_ends, counts, h_tiles)


def _experts_kernel(be_ref, nused_ref, xs_ref, wg_ref, wu_ref, wd_ref, ys_ref,
                    wg_bf, wu_bf, wd_bf):
    i = pl.program_id(0)
    nused = nused_ref[0]

    @pl.when(i < nused)
    def _():
        prev = be_ref[jnp.maximum(i - 1, 0)]

        @pl.when((i == 0) | (be_ref[i] != prev))
        def _():
            wg_bf[...] = wg_ref[...].astype(BF16)
            wu_bf[...] = wu_ref[...].astype(BF16)
            wd_bf[...] = wd_ref[...].astype(BF16)

        x = _from_tiles(xs_ref, TB_MOE).astype(BF16)
        hg = _dot(x, wg_bf[...])
        hu = _dot(x, wu_bf[...])
        act = (hg * jax.nn.sigmoid(hg) * hu).astype(BF16)
        _to_tiles(ys_ref, _dot(act, wd_bf[...]), TB_MOE)

    @pl.when(i >= nused)
    def _():
        ys_ref[...] = jnp.zeros_like(ys_ref)


def _experts(block_expert, nused, xs, w_gate, w_up, w_down, layer):
    n_blocks = xs.shape[0] // (TB_MOE * N_CHUNKS)
    d, de = w_gate.shape[2], w_gate.shape[3]

    def x_map(i, be, nu):
        return (jnp.minimum(i, nu[0] - 1), 0)

    def w_map(i, be, nu):
        return (layer, be[jnp.minimum(i, nu[0] - 1)], 0, 0)

    return pl.pallas_call(
        _experts_kernel,
        grid_spec=pltpu.PrefetchScalarGridSpec(
            num_scalar_prefetch=2,
            grid=(n_blocks,),
            in_specs=[
                pl.BlockSpec((TB_MOE * N_CHUNKS, LANES), x_map),
                pl.BlockSpec((None, None, d, de), w_map),
                pl.BlockSpec((None, None, d, de), w_map),
                pl.BlockSpec((None, None, de, d), w_map),
            ],
            out_specs=pl.BlockSpec((TB_MOE * N_CHUNKS, LANES), lambda i, be, nu: (i, 0)),
            scratch_shapes=[
                pltpu.VMEM((d, de), BF16),
                pltpu.VMEM((d, de), BF16),
                pltpu.VMEM((de, d), BF16),
            ],
        ),
        out_shape=jax.ShapeDtypeStruct(xs.shape, F32),
        compiler_params=pltpu.CompilerParams(
            dimension_semantics=("arbitrary",), vmem_limit_bytes=VMEM_LIMIT),
        name="experts",
    )(block_expert, nused, xs, w_gate, w_up, w_down)


def _combine_kernel(e_ref, r_ref, start_ref, h_ref, gate_ref, g2_ref, b2_ref, ys_ref,
                    out_ref, buf, sem, *, chunk_major):
    tm = TM_COMBINE
    n_slots = tm * TOP_K

    def issue(tok, carry):
        for k in range(TOP_K):
            n = tok * TOP_K + k
            src = start_ref[e_ref[0, n]] + r_ref[0, n]
            _row_copy(ys_ref, src, buf.at[k], tok, sem).start()
        return carry

    lax.fori_loop(0, tm, issue, 0, unroll=4)

    def drain(n, carry):
        _row_copy(ys_ref, 0, buf.at[0], 0, sem).wait()
        return carry

    lax.fori_loop(0, n_slots, drain, 0, unroll=8)

    gate = gate_ref[...]
    ffn = (_from_tiles(buf.at[0], tm) * gate[:, 0:1]
           + _from_tiles(buf.at[1], tm) * gate[:, 1:2])
    out = _layer_norm(ALPHA * _from_tiles(h_ref, tm) + ffn, g2_ref[...], b2_ref[...])
    if chunk_major:
        for c in range(N_CHUNKS):
            out_ref[c] = out[:, c * LANES:(c + 1) * LANES]
    else:
        out_ref[...] = out


def _combine(eflat, rflat, pad_starts, h_tiles, gate, g2, b2, ys, batch, seq, chunk_major):
    tm = TM_COMBINE
    t = batch * seq
    nt = t // tm
    nts = seq // tm
    n_slots = tm * TOP_K
    slot_spec = pl.BlockSpec((None, 1, n_slots), lambda i: (i, 0, 0), memory_space=pltpu.SMEM)
    if chunk_major:
        out_spec = pl.BlockSpec((None, N_CHUNKS, tm, LANES), lambda i: (i // nts, 0, i % nts, 0))
        out_shape = jax.ShapeDtypeStruct((batch, N_CHUNKS, seq, LANES), F32)
    else:
        out_spec = pl.BlockSpec((None, tm, D_MODEL), lambda i: (i // nts, i % nts, 0))
        out_shape = jax.ShapeDtypeStruct((batch, seq, D_MODEL), F32)
    return pl.pallas_call(
        functools.partial(_combine_kernel, chunk_major=chunk_major),
        grid=(nt,),
        in_specs=[
            slot_spec, slot_spec,
            pl.BlockSpec(memory_space=pltpu.SMEM),
            pl.BlockSpec((tm * N_CHUNKS, LANES), lambda i: (i, 0)),
            pl.BlockSpec((tm, TOP_K), lambda i: (i, 0)),
            pl.BlockSpec((1, D_MODEL), lambda i: (0, 0)),
            pl.BlockSpec((1, D_MODEL), lambda i: (0, 0)),
            pl.BlockSpec(memory_space=pl.ANY),
        ],
        out_specs=out_spec,
        out_shape=out_shape,
        scratch_shapes=[
            pltpu.VMEM((TOP_K, tm * N_CHUNKS, LANES), F32),
            pltpu.SemaphoreType.DMA(()),
        ],
        compiler_params=pltpu.CompilerParams(
            dimension_semantics=("arbitrary",), vmem_limit_bytes=VMEM_LIMIT),
        name="combine",
    )(eflat.reshape(nt, 1, n_slots), rflat.reshape(nt, 1, n_slots), pad_starts,
      h_tiles, gate, g2, b2, ys)


def _moe(h_tiles, batch, seq, w_router, b_router, w_gate, w_up, w_down, g2, b2, layer,
         chunk_major):
    t = batch * seq
    eidx, gate, rank, counts = _router(h_tiles, w_router, b_router)
    counts = counts[0]
    padded = (counts + TB_MOE - 1) // TB_MOE * TB_MOE
    pad_ends = jnp.cumsum(padded).astype(jnp.int32)
    pad_starts = pad_ends - padded
    n_rows = t * TOP_K + N_EXPERTS * TB_MOE
    n_blocks = n_rows // TB_MOE
    block_start = jnp.arange(n_blocks, dtype=jnp.int32) * TB_MOE
    block_expert = jnp.minimum(
        jnp.sum(block_start[:, None] >= pad_ends[None, :], -1), N_EXPERTS - 1).astype(jnp.int32)
    nused = (pad_ends[-1:] // TB_MOE).astype(jnp.int32)
    eflat = eidx.reshape(-1)
    rflat = rank.reshape(-1)
    xs = _dispatch(eflat, rflat, pad_starts, pad_ends, counts, h_tiles, n_rows)
    ys = _experts(block_expert, nused, xs, w_gate, w_up, w_down, layer)
    return _combine(eflat, rflat, pad_starts, h_tiles, gate, g2, b2, ys, batch, seq, chunk_major)


def _residue_rows(x_ref, dilation):
    n = x_ref.shape[1]
    if dilation == 1:
        return _from_chunks(x_ref)
    per = n // dilation
    return jnp.concatenate(
        [jnp.concatenate([x_ref[c, pl.ds(r, per, stride=dilation), :] for r in range(dilation)], 0)
         for c in range(N_CHUNKS)], -1)


def _proj_b_kernel(x_ref, wqkv_ref, wqm_ref, qkv0_ref, qkv1_ref, qkv2_ref, qm_ref):
    ts = x_ref.shape[1]
    qscale = jnp.where(
        lax.broadcasted_iota(jnp.int32, (1, 3 * DIL_OUT_WIDTH), 1) < DIL_OUT_WIDTH, ATT_SCALE, 1.0)
    for g, (out_ref, (_, dilation)) in enumerate(zip((qkv0_ref, qkv1_ref, qkv2_ref), DIL_PAIRS)):
        xb = _residue_rows(x_ref, dilation).astype(BF16)
        qkv = (_dot(xb, wqkv_ref[g]) * qscale).astype(BF16)
        per = ts // dilation
        for r in range(dilation):
            out_ref[r] = qkv[r * per:(r + 1) * per]
    qm = _dot(_from_chunks(x_ref).astype(BF16), wqm_ref[...])
    qm_ref[...] = (qm * ATT_SCALE).astype(BF16)


def _proj_b(x_cm, wqkv, wqm):
    b, _, s, _ = x_cm.shape
    ts = TS_MIX
    width = 3 * DIL_OUT_WIDTH
    out_specs, out_shapes = [], []
    for _, dilation in DIL_PAIRS:
        out_specs.append(pl.BlockSpec((None, dilation, ts // dilation, width),
                                      lambda i, j: (i, 0, j, 0)))
        out_shapes.append(jax.ShapeDtypeStruct((b, dilation, s // dilation, width), BF16))
    out_specs.append(pl.BlockSpec((None, ts, MEM_WIDTH), lambda i, j: (i, j, 0)))
    out_shapes.append(jax.ShapeDtypeStruct((b, s, MEM_WIDTH), BF16))
    return pl.pallas_call(
        _proj_b_kernel,
        grid=(b, s // ts),
        in_specs=[
            pl.BlockSpec((None, N_CHUNKS, ts, LANES), lambda i, j: (i, 0, j, 0)),
            pl.BlockSpec(wqkv.shape, lambda i, j: (0, 0, 0)),
            pl.BlockSpec(wqm.shape, lambda i, j: (0, 0)),
        ],
        out_specs=out_specs,
        out_shape=out_shapes,
        compiler_params=pltpu.CompilerParams(vmem_limit_bytes=VMEM_LIMIT),
        name="proj_b",
    )(x_cm, wqkv, wqm)


def _dil_attn_kernel(q_ref, kp_ref, kc_ref, vp_ref, vc_ref, o_ref, lse_ref, *, dilation, slopes):
    jb = pl.program_id(2)
    q = q_ref[...]
    kband = jnp.concatenate([kp_ref[...], kc_ref[...]], 0)
    vband = jnp.concatenate([vp_ref[...], vc_ref[...]], 0)
    iq = lax.broadcasted_iota(jnp.int32, (CHUNK, 2 * CHUNK), 0)
    jk = lax.broadcasted_iota(jnp.int32, (CHUNK, 2 * CHUNK), 1)
    rel = CHUNK + iq - jk
    valid = (rel >= 0) & (rel <= CHUNK) & ((jk >= CHUNK) | (jb > 0))
    dist = (dilation * rel).astype(F32)
    head = lax.broadcasted_iota(jnp.int32, (1, DIL_OUT_WIDTH), 1) // HEAD_DIM
    out = jnp.zeros((CHUNK, DIL_OUT_WIDTH), F32)
    lse = jnp.zeros((CHUNK, DIL_OUT_WIDTH), F32)
    for h in range(HEADS_PER_DIL):
        sel = head == h
        s = _dot_nt(jnp.where(sel, q, jnp.zeros_like(q)), kband)
        s = jnp.where(valid, s - slopes[h] * dist, -jnp.inf)
        m = jnp.max(s, -1, keepdims=True)
        p = jnp.exp(s - m)
        l = jnp.sum(p, -1, keepdims=True)
        out = out + _dot((p * (1.0 / l)).astype(BF16), jnp.where(sel, vband, jnp.zeros_like(vband)))
        lse = jnp.where(sel, m + jnp.log(l), lse)
    o_ref[...] = out
    lse_ref[...] = lse


def _dil_attn(qkv, group):
    b, dilation, length, _ = qkv.shape
    slopes = tuple(2.0 ** (-8.0 * (group * HEADS_PER_DIL + h + 1) / N_DIL_HEADS)
                   for h in range(HEADS_PER_DIL))
    blk = (None, None, CHUNK, DIL_OUT_WIDTH)
    cur = lambda col: pl.BlockSpec(blk, lambda i, r, j: (i, r, j, col))
    prev = lambda col: pl.BlockSpec(blk, lambda i, r, j: (i, r, jnp.maximum(j - 1, 0), col))
    out_spec = pl.BlockSpec(blk, lambda i, r, j: (i, r, j, 0))
    out_shape = jax.ShapeDtypeStruct((b, dilation, length, DIL_OUT_WIDTH), F32)
    return pl.pallas_call(
        functools.partial(_dil_attn_kernel, dilation=dilation, slopes=slopes),
        grid=(b, dilation, length // CHUNK),
        in_specs=[cur(0), prev(1), cur(1), prev(2), cur(2)],
        out_specs=[out_spec, out_spec],
        out_shape=[out_shape, out_shape],
        compiler_params=pltpu.CompilerParams(vmem_limit_bytes=VMEM_LIMIT),
        name=f"dil_attn_{group}",
    )(qkv, qkv, qkv, qkv, qkv)


def _token_order(src_ref, scratch, dilation):
    if dilation == 1:
        return src_ref[0]
    per = src_ref.shape[1]
    halves = DIL_OUT_WIDTH // LANES
    for r in range(dilation):
        v = src_ref[r]
        for c in range(halves):
            scratch[c, pl.ds(r, per, stride=dilation), :] = v[:, c * LANES:(c + 1) * LANES]
    return jnp.concatenate([scratch[c] for c in range(halves)], -1)


def _mixer_b_kernel(x_ref, qm_ref, o0_ref, l0_ref, o1_ref, l1_ref, o2_ref, l2_ref,
                    mk_ref, mv_ref, wout_ref, g1_ref, b1_ref, h_ref,
                    so1, sl1, so2, sl2):
    ts = x_ref.shape[1]
    outs = [o0_ref[0], _token_order(o1_ref, so1, DIL_PAIRS[1][1]),
            _token_order(o2_ref, so2, DIL_PAIRS[2][1])]
    lses = [l0_ref[0], _token_order(l1_ref, sl1, DIL_PAIRS[1][1]),
            _token_order(l2_ref, sl2, DIL_PAIRS[2][1])]
    top = functools.reduce(jnp.maximum, lses)
    es = [jnp.exp(l - top) for l in lses]
    inv = 1.0 / functools.reduce(jnp.add, es)
    mix = functools.reduce(jnp.add, [e * inv * o for e, o in zip(es, outs)])
    mo = _mem_attention(qm_ref[...], mk_ref[...], mv_ref[...])
    att = _dot(jnp.concatenate([mix, mo], -1).astype(BF16), wout_ref[...])
    h = _layer_norm(ALPHA * _from_chunks(x_ref) + att, g1_ref[...], b1_ref[...])
    _to_tiles(h_ref, h, ts)


def _mixer_b(x_cm, qm, attn, mk, mv, w_out, g1, b1):
    b, _, s, _ = x_cm.shape
    ts = TS_MIX
    nts = s // ts
    m = mk.shape[0] // b
    in_specs = [
        pl.BlockSpec((None, N_CHUNKS, ts, LANES), lambda i, j: (i, 0, j, 0)),
        pl.BlockSpec((None, ts, MEM_WIDTH), lambda i, j: (i, j, 0)),
    ]
    args = [x_cm, qm]
    for (o, lse), (_, dilation) in zip(attn, DIL_PAIRS):
        spec = pl.BlockSpec((None, dilation, ts // dilation, DIL_OUT_WIDTH),
                            lambda i, j: (i, 0, j, 0))
        in_specs += [spec, spec]
        args += [o, lse]
    in_specs += [
        pl.BlockSpec((m, MEM_WIDTH), lambda i, j: (i, 0)),
        pl.BlockSpec((m, MEM_WIDTH), lambda i, j: (i, 0)),
        pl.BlockSpec(w_out.shape, lambda i, j: (0, 0)),
        pl.BlockSpec((1, D_MODEL), lambda i, j: (0, 0)),
        pl.BlockSpec((1, D_MODEL), lambda i, j: (0, 0)),
    ]
    args += [mk, mv, w_out, g1, b1]
    halves = DIL_OUT_WIDTH // LANES
    return pl.pallas_call(
        _mixer_b_kernel,
        grid=(b, nts),
        in_specs=in_specs,
        out_specs=pl.BlockSpec((ts * N_CHUNKS, LANES), lambda i, j: (i * nts + j, 0)),
        out_shape=jax.ShapeDtypeStruct((b * s * N_CHUNKS, LANES), F32),
        scratch_shapes=[pltpu.VMEM((halves, ts, LANES), F32)] * 4,
        compiler_params=pltpu.CompilerParams(vmem_limit_bytes=VMEM_LIMIT),
        name="mixer_b",
    )(*args)


def kernel(x, mem, w_in_a, w_out_a, sg_ln_g, sg_ln_b, sg_w, sg_b, w_in_b, w_out_b, w_k_shared,
           w_v_shared, w_mem_k, w_mem_v, ln1_g, ln1_b, ln2_g, ln2_b, w_router, b_router,
           w_gate, w_up, w_down):
    batch, seq, d = x.shape
    mk, mv = _memkv(mem.reshape(-1, d), w_mem_k, w_mem_v)
    row = lambda v: v.reshape(1, -1)

    sg_bias = jnp.repeat(sg_b[0].T, HEAD_DIM, axis=1)
    h = _mixer_a(x, w_in_a[0].astype(BF16), w_out_a[0].astype(BF16), row(sg_ln_g[0]),
                 row(sg_ln_b[0]), sg_w[0], sg_bias, mk[0], mv[0], row(ln1_g[0]), row(ln1_b[0]))
    x_cm = _moe(h, batch, seq, w_router, row(b_router), w_gate, w_up, w_down,
                row(ln2_g[0]), row(ln2_b[0]), layer=0, chunk_major=True)

    wqkv = jnp.stack([
        jnp.concatenate([w[:, g * DIL_OUT_WIDTH:(g + 1) * DIL_OUT_WIDTH]
                         for w in (w_in_b[0], w_k_shared, w_v_shared)], -1)
        for g in range(len(DIL_PAIRS))]).astype(BF16)
    wqm = w_in_b[0][:, DIL_Q_WIDTH:].astype(BF16)
    qkv0, qkv1, qkv2, qm = _proj_b(x_cm, wqkv, wqm)
    attn = [_dil_attn(qkv, g) for g, qkv in enumerate((qkv0, qkv1, qkv2))]
    h = _mixer_b(x_cm, qm, attn, mk[1], mv[1], w_out_b[0].astype(BF16),
                 row(ln1_g[1]), row(ln1_b[1]))
    return _moe(h, batch, seq, w_router, row(b_router), w_gate, w_up, w_down,
                row(ln2_g[1]), row(ln2_b[1]), layer=1, chunk_major=False)
```

```python
import functools
import math

import jax
import jax.numpy as jnp
from jax import lax
from jax.experimental import pallas as pl
from jax.experimental.pallas import tpu as pltpu

D_MODEL = 1024
HEAD_DIM = 64
CHUNK = 128
N_SG = 12
SG_WIDTH = N_SG * HEAD_DIM
DIL_PAIRS = ((128, 1), (512, 4), (2048, 16))
HEADS_PER_DIL = 4
N_DIL_HEADS = HEADS_PER_DIL * len(DIL_PAIRS)
DIL_Q_WIDTH = N_DIL_HEADS * HEAD_DIM
DIL_OUT_WIDTH = HEADS_PER_DIL * HEAD_DIM
MEM_HEADS = 4
MEM_WIDTH = MEM_HEADS * HEAD_DIM
N_EXPERTS = 16
N_EXPERT_GROUPS = 4
EXPERTS_PER_GROUP = N_EXPERTS // N_EXPERT_GROUPS
TOP_K = 2
DEPTH = 2
ALPHA = (2 * DEPTH) ** 0.25
LN_EPS = 1e-5
ATT_SCALE = 1.0 / math.sqrt(HEAD_DIM)

LANES = 128
SUBLANES = 8
N_CHUNKS = D_MODEL // LANES

TS_MIX = 512
TM_ROUTE = 512
TM_DISPATCH = 512
TM_COMBINE = 256
TB_MOE = 512
VMEM_LIMIT = 56 * 1024 * 1024

F32 = jnp.float32
BF16 = jnp.bfloat16


def _dot(a, b):
    return jnp.dot(a, b, preferred_element_type=F32)


def _dot_nt(a, b):
    return lax.dot_general(a, b, (((1,), (1,)), ((), ())), preferred_element_type=F32)


def _layer_norm(x, g, b):
    mu = jnp.mean(x, -1, keepdims=True)
    xc = x - mu
    var = jnp.mean(xc * xc, -1, keepdims=True)
    return xc * lax.rsqrt(var + LN_EPS) * g + b


def _gelu(x):
    return 0.5 * x * (1.0 + lax.erf(x * (1.0 / math.sqrt(2.0))))


def _from_tiles(ref, n):
    return jnp.concatenate(
        [ref[pl.ds(c, n, stride=N_CHUNKS), :] for c in range(N_CHUNKS)], -1)


def _to_tiles(ref, val, n):
    for c in range(N_CHUNKS):
        ref[pl.ds(c, n, stride=N_CHUNKS), :] = val[:, c * LANES:(c + 1) * LANES]


def _from_chunks(ref):
    return jnp.concatenate([ref[c] for c in range(N_CHUNKS)], -1)


def _mem_attention(q, mk, mv):
    head = lax.broadcasted_iota(jnp.int32, (1, MEM_WIDTH), 1) // HEAD_DIM
    out = jnp.zeros(q.shape, F32)
    for h in range(MEM_HEADS):
        sel = head == h
        s = _dot_nt(q, jnp.where(sel, mk, jnp.zeros_like(mk)))
        m = jnp.max(s, -1, keepdims=True)
        p = jnp.exp(s - m)
        l = jnp.sum(p, -1, keepdims=True)
        out = out + _dot((p * (1.0 / l)).astype(BF16), jnp.where(sel, mv, jnp.zeros_like(mv)))
    return out


def _memkv_kernel(mem_ref, wk_ref, wv_ref, mk_ref, mv_ref):
    m = mem_ref[...].astype(BF16)
    mk_ref[...] = _dot(m, wk_ref[...].astype(BF16)).astype(BF16)
    mv_ref[...] = _dot(m, wv_ref[...].astype(BF16)).astype(BF16)


def _memkv(mem2d, w_mem_k, w_mem_v):
    n = mem2d.shape[0]
    return pl.pallas_call(
        _memkv_kernel,
        grid=(DEPTH,),
        in_specs=[
            pl.BlockSpec((n, D_MODEL), lambda l: (0, 0)),
            pl.BlockSpec((None, D_MODEL, MEM_WIDTH), lambda l: (l, 0, 0)),
            pl.BlockSpec((None, D_MODEL, MEM_WIDTH), lambda l: (l, 0, 0)),
        ],
        out_specs=[
            pl.BlockSpec((None, n, MEM_WIDTH), lambda l: (l, 0, 0)),
            pl.BlockSpec((None, n, MEM_WIDTH), lambda l: (l, 0, 0)),
        ],
        out_shape=[jax.ShapeDtypeStruct((DEPTH, n, MEM_WIDTH), BF16)] * 2,
        compiler_params=pltpu.CompilerParams(vmem_limit_bytes=VMEM_LIMIT),
        name="memkv",
    )(mem2d, w_mem_k, w_mem_v)


def _mixer_a_kernel(x_ref, win_ref, wout_ref, sgg_ref, sgb_ref, sgw_ref, sgbias_ref,
                    mk_ref, mv_ref, g1_ref, b1_ref, h_ref):
    ts = x_ref.shape[0]
    x = x_ref[...]
    proj = _dot(x.astype(BF16), win_ref[...])
    u = _gelu(proj[:, :SG_WIDTH])
    gv = _gelu(proj[:, SG_WIDTH:2 * SG_WIDTH])
    gv = _layer_norm(gv, sgg_ref[...], sgb_ref[...]).astype(BF16)

    row = lax.broadcasted_iota(jnp.int32, (CHUNK, CHUNK), 0)
    col = lax.broadcasted_iota(jnp.int32, (CHUNK, CHUNK), 1)
    ws = [jnp.where(row >= col, sgw_ref[g], 0.0).astype(BF16) for g in range(N_SG)]
    low_half = col < HEAD_DIM
    rows = []
    for c in range(ts // CHUNK):
        vc = gv[c * CHUNK:(c + 1) * CHUNK]
        parts = []
        for j in range(N_SG // 2):
            vp = vc[:, j * LANES:(j + 1) * LANES]
            parts.append(jnp.where(low_half, _dot(ws[2 * j], vp), _dot(ws[2 * j + 1], vp)))
        rows.append(jnp.concatenate(parts, -1) + sgbias_ref[...])
    mix = u * jnp.concatenate(rows, 0)

    qm = (proj[:, 2 * SG_WIDTH:] * ATT_SCALE).astype(BF16)
    mo = _mem_attention(qm, mk_ref[...], mv_ref[...])
    att = _dot(jnp.concatenate([mix, mo], -1).astype(BF16), wout_ref[...])
    h = _layer_norm(ALPHA * x + att, g1_ref[...], b1_ref[...])
    _to_tiles(h_ref, h, ts)


def _mixer_a(x, w_in, w_out, sg_g, sg_b, sg_w, sg_bias, mk, mv, g1, b1):
    b, s, d = x.shape
    nts = s // TS_MIX
    m = mk.shape[0] // b
    full = lambda *shape: pl.BlockSpec(shape, lambda i, j: (0,) * len(shape))
    return pl.pallas_call(
        _mixer_a_kernel,
        grid=(b, nts),
        in_specs=[
            pl.BlockSpec((None, TS_MIX, d), lambda i, j: (i, j, 0)),
            full(d, 2 * SG_WIDTH + MEM_WIDTH),
            full(SG_WIDTH + MEM_WIDTH, d),
            full(1, SG_WIDTH),
            full(1, SG_WIDTH),
            full(N_SG, CHUNK, CHUNK),
            full(CHUNK, SG_WIDTH),
            pl.BlockSpec((m, MEM_WIDTH), lambda i, j: (i, 0)),
            pl.BlockSpec((m, MEM_WIDTH), lambda i, j: (i, 0)),
            full(1, d),
            full(1, d),
        ],
        out_specs=pl.BlockSpec((TS_MIX * N_CHUNKS, LANES), lambda i, j: (i * nts + j, 0)),
        out_shape=jax.ShapeDtypeStruct((b * s * N_CHUNKS, LANES), F32),
        compiler_params=pltpu.CompilerParams(vmem_limit_bytes=VMEM_LIMIT),
        name="mixer_a",
    )(x, w_in, w_out, sg_g, sg_b, sg_w, sg_bias, mk, mv, g1, b1)


def _router_kernel(h_ref, wr_ref, br_ref, slots_ref, gate_ref, counts_ref, base_ref):
    tm = TM_ROUTE

    @pl.when(pl.program_id(0) == 0)
    def _():
        base_ref[...] = jnp.zeros_like(base_ref)

    h = _from_tiles(h_ref, tm)
    w = wr_ref[...]
    h_hi = h.astype(BF16)
    h_lo = (h - h_hi.astype(F32)).astype(BF16)
    w_hi = w.astype(BF16)
    w_lo = (w - w_hi.astype(F32)).astype(BF16)
    logits = _dot(h_hi, w_hi) + (_dot(h_hi, w_lo) + _dot(h_lo, w_hi))
    lt = logits.T[:N_EXPERTS]

    ex = jnp.exp(lt - jnp.max(lt, 0, keepdims=True))
    probs = ex / jnp.sum(ex, 0, keepdims=True)
    sel = probs + br_ref[...]
    eid = lax.broadcasted_iota(jnp.int32, (N_EXPERTS, tm), 0)
    group = eid // EXPERTS_PER_GROUP

    def top2(mask):
        v = jnp.where(mask, sel, -jnp.inf)
        m1 = jnp.max(v, 0, keepdims=True)
        i1 = jnp.min(jnp.where(v == m1, eid, N_EXPERTS), 0, keepdims=True)
        v2 = jnp.where(eid == i1, -jnp.inf, v)
        m2 = jnp.max(v2, 0, keepdims=True)
        i2 = jnp.min(jnp.where(v2 == m2, eid, N_EXPERTS), 0, keepdims=True)
        return m1, i1, m2, i2

    scores = []
    for g in range(N_EXPERT_GROUPS):
        m1, _, m2, _ = top2(group == g)
        scores.append(m1 + m2)
    best = functools.reduce(jnp.maximum, scores)
    g_idx = jnp.full((1, tm), N_EXPERT_GROUPS - 1, jnp.int32)
    for g in reversed(range(N_EXPERT_GROUPS - 1)):
        g_idx = jnp.where(scores[g] == best, g, g_idx)
    _, e0, _, e1 = top2(group == g_idx)

    hot0 = eid == e0
    hot1 = eid == e1
    p0 = jnp.sum(jnp.where(hot0, probs, 0.0), 0, keepdims=True)
    p1 = jnp.sum(jnp.where(hot1, probs, 0.0), 0, keepdims=True)
    psum = p0 + p1

    hot = jnp.where(hot0 | hot1, 1.0, 0.0)
    earlier = (lax.broadcasted_iota(jnp.int32, (tm, tm), 0)
               < lax.broadcasted_iota(jnp.int32, (tm, tm), 1))
    before = _dot(hot.astype(BF16), jnp.where(earlier, 1.0, 0.0).astype(BF16)) + base_ref[...]
    r0 = jnp.sum(jnp.where(hot0, before, 0.0), 0, keepdims=True).astype(jnp.int32)
    r1 = jnp.sum(jnp.where(hot1, before, 0.0), 0, keepdims=True).astype(jnp.int32)

    row = lax.broadcasted_iota(jnp.int32, (SUBLANES, tm), 0)
    slots_ref[...] = jnp.where(row == 0, e0, jnp.where(row == 1, e1, jnp.where(
        row == 2, r0, jnp.where(row == 3, r1, 0))))
    grow = lax.broadcasted_iota(jnp.int32, (LANES, tm), 0)
    gates = jnp.where(grow == 0, p0 / psum, jnp.where(grow == 1, p1 / psum, 0.0))
    gate_ref[...] = gates.T[:, :TOP_K]
    total = base_ref[...] + jnp.sum(hot, 1, keepdims=True)
    base_ref[...] = total
    counts_ref[...] = total.astype(jnp.int32)


def _router(h_tiles, w_router, b_router):
    t = h_tiles.shape[0] // N_CHUNKS
    tm = TM_ROUTE
    return pl.pallas_call(
        _router_kernel,
        grid=(t // tm,),
        in_specs=[
            pl.BlockSpec((tm * N_CHUNKS, LANES), lambda i: (i, 0)),
            pl.BlockSpec((D_MODEL, LANES), lambda i: (0, 0)),
            pl.BlockSpec((N_EXPERTS, 1), lambda i: (0, 0)),
        ],
        out_specs=[
            pl.BlockSpec((None, SUBLANES, tm), lambda i: (i, 0, 0)),
            pl.BlockSpec((tm, TOP_K), lambda i: (i, 0)),
            pl.BlockSpec((N_EXPERTS, 1), lambda i: (0, 0)),
        ],
        out_shape=[
            jax.ShapeDtypeStruct((t // tm, SUBLANES, tm), jnp.int32),
            jax.ShapeDtypeStruct((t, TOP_K), F32),
            jax.ShapeDtypeStruct((N_EXPERTS, 1), jnp.int32),
        ],
        scratch_shapes=[pltpu.VMEM((N_EXPERTS, 1), F32)],
        compiler_params=pltpu.CompilerParams(
            dimension_semantics=("arbitrary",), vmem_limit_bytes=VMEM_LIMIT),
        name="router",
    )(h_tiles, w_router, b_router)


def _row_copy(src_ref, src_tok, dst_ref, dst_tok, sem):
    return pltpu.make_async_copy(
        src_ref.at[pl.ds(pl.multiple_of(src_tok * N_CHUNKS, N_CHUNKS), N_CHUNKS)],
        dst_ref.at[pl.ds(pl.multiple_of(dst_tok * N_CHUNKS, N_CHUNKS), N_CHUNKS)],
        sem)


def _dispatch_kernel(slot_ref, start_ref, end_ref, cnt_ref, h_ref, xs_ref,
                     zero_ref, sem, zsem):
    i = pl.program_id(0)
    n_slots = TM_DISPATCH * TOP_K

    @pl.when(i == 0)
    def _():
        zero_ref[...] = jnp.zeros_like(zero_ref)
        for e in range(N_EXPERTS):
            @pl.when(cnt_ref[e, 0] > 0)
            def _():
                first = pl.multiple_of((end_ref[e] - TB_MOE) * N_CHUNKS, N_CHUNKS)
                cp = pltpu.make_async_copy(
                    zero_ref, xs_ref.at[pl.ds(first, TB_MOE * N_CHUNKS)], zsem)
                cp.start()
                cp.wait()

        def zero_block(blk, carry):
            first = pl.multiple_of(blk * (TB_MOE * N_CHUNKS), TB_MOE * N_CHUNKS)
            cp = pltpu.make_async_copy(
                zero_ref, xs_ref.at[pl.ds(first, TB_MOE * N_CHUNKS)], zsem)
            cp.start()
            cp.wait()
            return carry

        n_blocks = xs_ref.shape[0] // (TB_MOE * N_CHUNKS)
        lax.fori_loop(end_ref[N_EXPERTS - 1] // TB_MOE, n_blocks, zero_block, 0)

    def issue(tok, carry):
        for k in range(TOP_K):
            dst = start_ref[slot_ref[k, tok]] + slot_ref[TOP_K + k, tok]
            _row_copy(h_ref, tok, xs_ref, dst, sem).start()
        return carry

    lax.fori_loop(0, TM_DISPATCH, issue, 0, unroll=4)

    def drain(n, carry):
        _row_copy(h_ref, 0, xs_ref, 0, sem).wait()
        return carry

    lax.fori_loop(0, n_slots, drain, 0, unroll=8)


def _dispatch(slots, pad_starts, pad_ends, counts, h_tiles, n_rows):
    nt = slots.shape[0]
    smem_vec = pl.BlockSpec(memory_space=pltpu.SMEM)
    slot_spec = pl.BlockSpec((None, SUBLANES, TM_DISPATCH), lambda i: (i, 0, 0),
                             memory_space=pltpu.SMEM)
    return pl.pallas_call(
        _dispatch_kernel,
        grid=(nt,),
        in_specs=[slot_spec, smem_vec, smem_vec, smem_vec,
                  pl.BlockSpec((TM_DISPATCH * N_CHUNKS, LANES), lambda i: (i, 0))],
        out_specs=pl.BlockSpec(memory_space=pl.ANY),
        out_shape=jax.ShapeDtypeStruct((n_rows * N_CHUNKS, LANES), F32),
        scratch_shapes=[
            pltpu.VMEM((TB_MOE * N_CHUNKS, LANES), F32),
            pltpu.SemaphoreType.DMA(()),
            pltpu.SemaphoreType.DMA(()),
        ],
        compiler_params=pltpu.CompilerParams(
            dimension_semantics=("arbitrary",), has_side_effects=True,
            vmem_limit_bytes=VMEM_LIMIT),
        name="dispatch",
    )(slots, pad_starts, pad_ends, counts, h_tiles)


def _experts_kernel(be_ref, nused_ref, xs_ref, wg_ref, wu_ref, wd_ref, ys_ref,
                    wg_bf, wu_bf, wd_bf):
    i = pl.program_id(0)
    nused = nused_ref[0]

    @pl.when(i < nused)
    def _():
        prev = be_ref[jnp.maximum(i - 1, 0)]

        @pl.when((i == 0) | (be_ref[i] != prev))
        def _():
            wg_bf[...] = wg_ref[...].astype(BF16)
            wu_bf[...] = wu_ref[...].astype(BF16)
            wd_bf[...] = wd_ref[...].astype(BF16)

        x = _from_tiles(xs_ref, TB_MOE).astype(BF16)
        hg = _dot(x, wg_bf[...])
        hu = _dot(x, wu_bf[...])
        act = (hg * jax.nn.sigmoid(hg) * hu).astype(BF16)
        _to_tiles(ys_ref, _dot(act, wd_bf[...]), TB_MOE)

    @pl.when(i >= nused)
    def _():
        ys_ref[...] = jnp.zeros_like(ys_ref)


def _experts(block_expert, nused, xs, w_gate, w_up, w_down, layer):
    n_blocks = xs.shape[0] // (TB_MOE * N_CHUNKS)
    d, de = w_gate.shape[2], w_gate.shape[3]

    def x_map(i, be, nu):
        return (jnp.minimum(i, nu[0] - 1), 0)

    def w_map(i, be, nu):
        return (layer, be[jnp.minimum(i, nu[0] - 1)], 0, 0)

    return pl.pallas_call(
        _experts_kernel,
        grid_spec=pltpu.PrefetchScalarGridSpec(
            num_scalar_prefetch=2,
            grid=(n_blocks,),
            in_specs=[
                pl.BlockSpec((TB_MOE * N_CHUNKS, LANES), x_map),
                pl.BlockSpec((None, None, d, de), w_map),
                pl.BlockSpec((None, None, d, de), w_map),
                pl.BlockSpec((None, None, de, d), w_map),
            ],
            out_specs=pl.BlockSpec((TB_MOE * N_CHUNKS, LANES), lambda i, be, nu: (i, 0)),
            scratch_shapes=[
                pltpu.VMEM((d, de), BF16),
                pltpu.VMEM((d, de), BF16),
                pltpu.VMEM((de, d), BF16),
            ],
        ),
        out_shape=jax.ShapeDtypeStruct(xs.shape, F32),
        compiler_params=pltpu.CompilerParams(
            dimension_semantics=("arbitrary",), vmem_limit_bytes=VMEM_LIMIT),
        name="experts",
    )(block_expert, nused, xs, w_gate, w_up, w_down)


def _combine_kernel(slot_ref, start_ref, h_ref, gate_ref, g2_ref, b2_ref, ys_ref,
                    out_ref, buf, sem, *, chunk_major):
    tm = TM_COMBINE
    n_slots = tm * TOP_K

    def issue(tok, carry):
        for k in range(TOP_K):
            src = start_ref[slot_ref[k, tok]] + slot_ref[TOP_K + k, tok]
            _row_copy(ys_ref, src, buf.at[k], tok, sem).start()
        return carry

    lax.fori_loop(0, tm, issue, 0, unroll=4)

    def drain(n, carry):
        _row_copy(ys_ref, 0, buf.at[0], 0, sem).wait()
        return carry

    lax.fori_loop(0, n_slots, drain, 0, unroll=8)

    gate = gate_ref[...]
    ffn = (_from_tiles(buf.at[0], tm) * gate[:, 0:1]
           + _from_tiles(buf.at[1], tm) * gate[:, 1:2])
    out = _layer_norm(ALPHA * _from_tiles(h_ref, tm) + ffn, g2_ref[...], b2_ref[...])
    if chunk_major:
        for c in range(N_CHUNKS):
            out_ref[c] = out[:, c * LANES:(c + 1) * LANES]
    else:
        out_ref[...] = out


def _combine(slots, pad_starts, h_tiles, gate, g2, b2, ys, batch, seq, chunk_major):
    tm = TM_COMBINE
    t = batch * seq
    nt = t // tm
    nts = seq // tm
    per_route = TM_ROUTE // tm
    slot_spec = pl.BlockSpec((None, SUBLANES, tm), lambda i: (i // per_route, 0, i % per_route),
                             memory_space=pltpu.SMEM)
    if chunk_major:
        out_spec = pl.BlockSpec((None, N_CHUNKS, tm, LANES), lambda i: (i // nts, 0, i % nts, 0))
        out_shape = jax.ShapeDtypeStruct((batch, N_CHUNKS, seq, LANES), F32)
    else:
        out_spec = pl.BlockSpec((None, tm, D_MODEL), lambda i: (i // nts, i % nts, 0))
        out_shape = jax.ShapeDtypeStruct((batch, seq, D_MODEL), F32)
    return pl.pallas_call(
        functools.partial(_combine_kernel, chunk_major=chunk_major),
        grid=(nt,),
        in_specs=[
            slot_spec,
            pl.BlockSpec(memory_space=pltpu.SMEM),
            pl.BlockSpec((tm * N_CHUNKS, LANES), lambda i: (i, 0)),
            pl.BlockSpec((tm, TOP_K), lambda i: (i, 0)),
            pl.BlockSpec((1, D_MODEL), lambda i: (0, 0)),
            pl.BlockSpec((1, D_MODEL), lambda i: (0, 0)),
            pl.BlockSpec(memory_space=pl.ANY),
        ],
        out_specs=out_spec,
        out_shape=out_shape,
        scratch_shapes=[
            pltpu.VMEM((TOP_K, tm * N_CHUNKS, LANES), F32),
            pltpu.SemaphoreType.DMA(()),
        ],
        compiler_params=pltpu.CompilerParams(
            dimension_semantics=("arbitrary",), vmem_limit_bytes=VMEM_LIMIT),
        name="combine",
    )(slots, pad_starts, h_tiles, gate, g2, b2, ys)


def _moe(h_tiles, batch, seq, w_router, b_router, w_gate, w_up, w_down, g2, b2, layer,
         chunk_major):
    t = batch * seq
    slots, gate, counts = _router(h_tiles, w_router, b_router)
    cnt = counts[:, 0]
    padded = (cnt + TB_MOE - 1) // TB_MOE * TB_MOE
    pad_ends = jnp.cumsum(padded).astype(jnp.int32)
    pad_starts = pad_ends - padded
    n_rows = t * TOP_K + N_EXPERTS * TB_MOE
    n_blocks = n_rows // TB_MOE
    block_start = jnp.arange(n_blocks, dtype=jnp.int32) * TB_MOE
    block_expert = jnp.minimum(
        jnp.sum(block_start[:, None] >= pad_ends[None, :], -1), N_EXPERTS - 1).astype(jnp.int32)
    nused = (pad_ends[-1:] // TB_MOE).astype(jnp.int32)
    xs = _dispatch(slots, pad_starts, pad_ends, counts, h_tiles, n_rows)
    ys = _experts(block_expert, nused, xs, w_gate, w_up, w_down, layer)
    return _combine(slots, pad_starts, h_tiles, gate, g2, b2, ys, batch, seq, chunk_major)


def _residue_rows(x_ref, dilation):
    n = x_ref.shape[1]
    if dilation == 1:
        return _from_chunks(x_ref)
    per = n // dilation
    return jnp.concatenate(
        [jnp.concatenate([x_ref[c, pl.ds(r, per, stride=dilation), :] for r in range(dilation)], 0)
         for c in range(N_CHUNKS)], -1)


def _proj_b_kernel(x_ref, wqkv_ref, wqm_ref, qkv0_ref, qkv1_ref, qkv2_ref, qm_ref):
    ts = x_ref.shape[1]
    qscale = jnp.where(
        lax.broadcasted_iota(jnp.int32, (1, 3 * DIL_OUT_WIDTH), 1) < DIL_OUT_WIDTH, ATT_SCALE, 1.0)
    for g, (out_ref, (_, dilation)) in enumerate(zip((qkv0_ref, qkv1_ref, qkv2_ref), DIL_PAIRS)):
        xb = _residue_rows(x_ref, dilation).astype(BF16)
        qkv = (_dot(xb, wqkv_ref[g]) * qscale).astype(BF16)
        per = ts // dilation
        for r in range(dilation):
            out_ref[r] = qkv[r * per:(r + 1) * per]
    qm = _dot(_from_chunks(x_ref).astype(BF16), wqm_ref[...])
    qm_ref[...] = (qm * ATT_SCALE).astype(BF16)


def _proj_b(x_cm, wqkv, wqm):
    b, _, s, _ = x_cm.shape
    ts = TS_MIX
    width = 3 * DIL_OUT_WIDTH
    out_specs, out_shapes = [], []
    for _, dilation in DIL_PAIRS:
        out_specs.append(pl.BlockSpec((None, dilation, ts // dilation, width),
                                      lambda i, j: (i, 0, j, 0)))
        out_shapes.append(jax.ShapeDtypeStruct((b, dilation, s // dilation, width), BF16))
    out_specs.append(pl.BlockSpec((None, ts, MEM_WIDTH), lambda i, j: (i, j, 0)))
    out_shapes.append(jax.ShapeDtypeStruct((b, s, MEM_WIDTH), BF16))
    return pl.pallas_call(
        _proj_b_kernel,
        grid=(b, s // ts),
        in_specs=[
            pl.BlockSpec((None, N_CHUNKS, ts, LANES), lambda i, j: (i, 0, j, 0)),
            pl.BlockSpec(wqkv.shape, lambda i, j: (0, 0, 0)),
            pl.BlockSpec(wqm.shape, lambda i, j: (0, 0)),
        ],
        out_specs=out_specs,
        out_shape=out_shapes,
        compiler_params=pltpu.CompilerParams(vmem_limit_bytes=VMEM_LIMIT),
        name="proj_b",
    )(x_cm, wqkv, wqm)


def _dil_attn_kernel(qkv_ref, o_ref, lse_ref, *, dilation, slopes):
    nb = qkv_ref.shape[1] // CHUNK
    width = DIL_OUT_WIDTH
    iq = lax.broadcasted_iota(jnp.int32, (CHUNK, 2 * CHUNK), 0)
    jk = lax.broadcasted_iota(jnp.int32, (CHUNK, 2 * CHUNK), 1)
    rel = CHUNK + iq - jk
    in_window = (rel >= 0) & (rel <= CHUNK)
    dist = (dilation * rel).astype(F32)
    head = lax.broadcasted_iota(jnp.int32, (1, width), 1) // HEAD_DIM

    def one_block(idx):
        r = idx // nb
        jb = idx % nb
        cur = pl.ds(pl.multiple_of(jb * CHUNK, CHUNK), CHUNK)
        prev = pl.ds(pl.multiple_of(jnp.maximum(jb - 1, 0) * CHUNK, CHUNK), CHUNK)
        q = qkv_ref[r, cur, 0:width]
        kband = jnp.concatenate([qkv_ref[r, prev, width:2 * width],
                                 qkv_ref[r, cur, width:2 * width]], 0)
        vband = jnp.concatenate([qkv_ref[r, prev, 2 * width:3 * width],
                                 qkv_ref[r, cur, 2 * width:3 * width]], 0)
        valid = in_window & ((jk >= CHUNK) | (jb > 0))
        out = jnp.zeros((CHUNK, width), F32)
        lse = jnp.zeros((CHUNK, width), F32)
        for h in range(HEADS_PER_DIL):
            sel = head == h
            s = _dot_nt(jnp.where(sel, q, jnp.zeros_like(q)), kband)
            s = jnp.where(valid, s - slopes[h] * dist, -jnp.inf)
            m = jnp.max(s, -1, keepdims=True)
            p = jnp.exp(s - m)
            l = jnp.sum(p, -1, keepdims=True)
            out = out + _dot((p * (1.0 / l)).astype(BF16),
                             jnp.where(sel, vband, jnp.zeros_like(vband)))
            lse = jnp.where(sel, m + jnp.log(l), lse)
        o_ref[r, cur, :] = out
        lse_ref[r, cur, :] = lse

    def pair(i, carry):
        one_block(2 * i)
        one_block(2 * i + 1)
        return carry

    lax.fori_loop(0, dilation * nb // 2, pair, 0)


def _dil_attn(qkv, group):
    b, dilation, length, _ = qkv.shape
    slopes = tuple(2.0 ** (-8.0 * (group * HEADS_PER_DIL + h + 1) / N_DIL_HEADS)
                   for h in range(HEADS_PER_DIL))
    out_spec = pl.BlockSpec((None, dilation, length, DIL_OUT_WIDTH), lambda i: (i, 0, 0, 0))
    out_shape = jax.ShapeDtypeStruct((b, dilation, length, DIL_OUT_WIDTH), F32)
    return pl.pallas_call(
        functools.partial(_dil_attn_kernel, dilation=dilation, slopes=slopes),
        grid=(b,),
        in_specs=[pl.BlockSpec((None, dilation, length, 3 * DIL_OUT_WIDTH),
                               lambda i: (i, 0, 0, 0))],
        out_specs=[out_spec, out_spec],
        out_shape=[out_shape, out_shape],
        compiler_params=pltpu.CompilerParams(vmem_limit_bytes=VMEM_LIMIT),
        name=f"dil_attn_{group}",
    )(qkv)


def _token_order(src_ref, scratch, dilation):
    if dilation == 1:
        return src_ref[0]
    per = src_ref.shape[1]
    halves = DIL_OUT_WIDTH // LANES
    for r in range(dilation):
        v = src_ref[r]
        for c in range(halves):
            scratch[c, pl.ds(r, per, stride=dilation), :] = v[:, c * LANES:(c + 1) * LANES]
    return jnp.concatenate([scratch[c] for c in range(halves)], -1)


def _mixer_b_kernel(x_ref, qm_ref, o0_ref, l0_ref, o1_ref, l1_ref, o2_ref, l2_ref,
                    mk_ref, mv_ref, wout_ref, g1_ref, b1_ref, h_ref,
                    so1, sl1, so2, sl2):
    ts = x_ref.shape[1]
    outs = [o0_ref[0], _token_order(o1_ref, so1, DIL_PAIRS[1][1]),
            _token_order(o2_ref, so2, DIL_PAIRS[2][1])]
    lses = [l0_ref[0], _token_order(l1_ref, sl1, DIL_PAIRS[1][1]),
            _token_order(l2_ref, sl2, DIL_PAIRS[2][1])]
    top = functools.reduce(jnp.maximum, lses)
    es = [jnp.exp(l - top) for l in lses]
    inv = 1.0 / functools.reduce(jnp.add, es)
    mix = functools.reduce(jnp.add, [e * inv * o for e, o in zip(es, outs)])
    mo = _mem_attention(qm_ref[...], mk_ref[...], mv_ref[...])
    att = _dot(jnp.concatenate([mix, mo], -1).astype(BF16), wout_ref[...])
    h = _layer_norm(ALPHA * _from_chunks(x_ref) + att, g1_ref[...], b1_ref[...])
    _to_tiles(h_ref, h, ts)


def _mixer_b(x_cm, qm, attn, mk, mv, w_out, g1, b1):
    b, _, s, _ = x_cm.shape
    ts = TS_MIX
    nts = s // ts
    m = mk.shape[0] // b
    in_specs = [
        pl.BlockSpec((None, N_CHUNKS, ts, LANES), lambda i, j: (i, 0, j, 0)),
        pl.BlockSpec((None, ts, MEM_WIDTH), lambda i, j: (i, j, 0)),
    ]
    args = [x_cm, qm]
    for (o, lse), (_, dilation) in zip(attn, DIL_PAIRS):
        spec = pl.BlockSpec((None, dilation, ts // dilation, DIL_OUT_WIDTH),
                            lambda i, j: (i, 0, j, 0))
        in_specs += [spec, spec]
        args += [o, lse]
    in_specs += [
        pl.BlockSpec((m, MEM_WIDTH), lambda i, j: (i, 0)),
        pl.BlockSpec((m, MEM_WIDTH), lambda i, j: (i, 0)),
        pl.BlockSpec(w_out.shape, lambda i, j: (0, 0)),
        pl.BlockSpec((1, D_MODEL), lambda i, j: (0, 0)),
        pl.BlockSpec((1, D_MODEL), lambda i, j: (0, 0)),
    ]
    args += [mk, mv, w_out, g1, b1]
    halves = DIL_OUT_WIDTH // LANES
    return pl.pallas_call(
        _mixer_b_kernel,
        grid=(b, nts),
        in_specs=in_specs,
        out_specs=pl.BlockSpec((ts * N_CHUNKS, LANES), lambda i, j: (i * nts + j, 0)),
        out_shape=jax.ShapeDtypeStruct((b * s * N_CHUNKS, LANES), F32),
        scratch_shapes=[pltpu.VMEM((halves, ts, LANES), F32)] * 4,
        compiler_params=pltpu.CompilerParams(vmem_limit_bytes=VMEM_LIMIT),
        name="mixer_b",
    )(*args)


def kernel(x, mem, w_in_a, w_out_a, sg_ln_g, sg_ln_b, sg_w, sg_b, w_in_b, w_out_b, w_k_shared,
           w_v_shared, w_mem_k, w_mem_v, ln1_g, ln1_b, ln2_g, ln2_b, w_router, b_router,
           w_gate, w_up, w_down):
    batch, seq, d = x.shape
    mk, mv = _memkv(mem.reshape(-1, d), w_mem_k, w_mem_v)
    row = lambda v: v.reshape(1, -1)

    sg_bias = jnp.repeat(sg_b[0].T, HEAD_DIM, axis=1)
    h = _mixer_a(x, w_in_a[0].astype(BF16), w_out_a[0].astype(BF16), row(sg_ln_g[0]),
                 row(sg_ln_b[0]), sg_w[0], sg_bias, mk[0], mv[0], row(ln1_g[0]), row(ln1_b[0]))
    w_router_pad = jnp.pad(w_router, ((0, 0), (0, LANES - N_EXPERTS)))
    b_router_col = b_router.reshape(N_EXPERTS, 1)
    x_cm = _moe(h, batch, seq, w_router_pad, b_router_col, w_gate, w_up, w_down,
                row(ln2_g[0]), row(ln2_b[0]), layer=0, chunk_major=True)

    wqkv = jnp.stack([
        jnp.concatenate([w[:, g * DIL_OUT_WIDTH:(g + 1) * DIL_OUT_WIDTH]
                         for w in (w_in_b[0], w_k_shared, w_v_shared)], -1)
        for g in range(len(DIL_PAIRS))]).astype(BF16)
    wqm = w_in_b[0][:, DIL_Q_WIDTH:].astype(BF16)
    qkv0, qkv1, qkv2, qm = _proj_b(x_cm, wqkv, wqm)
    attn = [_dil_attn(qkv, g) for g, qkv in enumerate((qkv0, qkv1, qkv2))]
    h = _mixer_b(x_cm, qm, attn, mk[1], mv[1], w_out_b[0].astype(BF16),
                 row(ln1_g[1]), row(ln1_b[1]))
    return _moe(h, batch, seq, w_router_pad, b_router_col, w_gate, w_up, w_down,
                row(ln2_g[1]), row(ln2_b[1]), layer=1, chunk_major=False)
```

```python
import functools
import math

import jax
import jax.numpy as jnp
from jax import lax
from jax.experimental import pallas as pl
from jax.experimental.pallas import tpu as pltpu

D_MODEL = 1024
HEAD_DIM = 64
CHUNK = 128
N_SG = 12
SG_WIDTH = N_SG * HEAD_DIM
DIL_PAIRS = ((128, 1), (512, 4), (2048, 16))
HEADS_PER_DIL = 4
N_DIL_HEADS = HEADS_PER_DIL * len(DIL_PAIRS)
DIL_Q_WIDTH = N_DIL_HEADS * HEAD_DIM
DIL_OUT_WIDTH = HEADS_PER_DIL * HEAD_DIM
MEM_HEADS = 4
MEM_WIDTH = MEM_HEADS * HEAD_DIM
N_EXPERTS = 16
N_EXPERT_GROUPS = 4
EXPERTS_PER_GROUP = N_EXPERTS // N_EXPERT_GROUPS
TOP_K = 2
DEPTH = 2
ALPHA = (2 * DEPTH) ** 0.25
LN_EPS = 1e-5
ATT_SCALE = 1.0 / math.sqrt(HEAD_DIM)

LANES = 128
SUBLANES = 8
N_CHUNKS = D_MODEL // LANES

TS_MIX = 512
MXU_DEPTH = 256
TM_ROUTE = MXU_DEPTH
RUN_BITS = TM_ROUTE.bit_length()
TB_MOE = 512
VMEM_LIMIT = 56 * 1024 * 1024

F32 = jnp.float32
BF16 = jnp.bfloat16


def _dot(a, b):
    return jnp.dot(a, b, preferred_element_type=F32)


def _dot_nt(a, b):
    return lax.dot_general(a, b, (((1,), (1,)), ((), ())), preferred_element_type=F32)


def _layer_norm(x, g, b):
    mu = jnp.mean(x, -1, keepdims=True)
    xc = x - mu
    var = jnp.mean(xc * xc, -1, keepdims=True)
    return xc * lax.rsqrt(var + LN_EPS) * g + b


def _gelu(x):
    return 0.5 * x * (1.0 + lax.erf(x * (1.0 / math.sqrt(2.0))))


def _from_tiles(ref, n):
    return jnp.concatenate(
        [ref[pl.ds(c, n, stride=N_CHUNKS), :] for c in range(N_CHUNKS)], -1)


def _to_tiles(ref, val, n):
    for c in range(N_CHUNKS):
        ref[pl.ds(c, n, stride=N_CHUNKS), :] = val[:, c * LANES:(c + 1) * LANES]


def _from_chunks(ref):
    return jnp.concatenate([ref[c] for c in range(N_CHUNKS)], -1)


def _mem_attention(q, mk, mv):
    head = lax.broadcasted_iota(jnp.int32, (1, MEM_WIDTH), 1) // HEAD_DIM
    out = jnp.zeros(q.shape, F32)
    for h in range(MEM_HEADS):
        sel = head == h
        s = _dot_nt(q, jnp.where(sel, mk, jnp.zeros_like(mk)))
        m = jnp.max(s, -1, keepdims=True)
        p = jnp.exp(s - m)
        l = jnp.sum(p, -1, keepdims=True)
        out = out + _dot((p * (1.0 / l)).astype(BF16), jnp.where(sel, mv, jnp.zeros_like(mv)))
    return out


def _memkv_kernel(mem_ref, wk_ref, wv_ref, mk_ref, mv_ref):
    m = mem_ref[...].astype(BF16)
    mk_ref[...] = _dot(m, wk_ref[...].astype(BF16)).astype(BF16)
    mv_ref[...] = _dot(m, wv_ref[...].astype(BF16)).astype(BF16)


def _memkv(mem2d, w_mem_k, w_mem_v):
    n = mem2d.shape[0]
    return pl.pallas_call(
        _memkv_kernel,
        grid=(DEPTH,),
        in_specs=[
            pl.BlockSpec((n, D_MODEL), lambda l: (0, 0)),
            pl.BlockSpec((None, D_MODEL, MEM_WIDTH), lambda l: (l, 0, 0)),
            pl.BlockSpec((None, D_MODEL, MEM_WIDTH), lambda l: (l, 0, 0)),
        ],
        out_specs=[
            pl.BlockSpec((None, n, MEM_WIDTH), lambda l: (l, 0, 0)),
            pl.BlockSpec((None, n, MEM_WIDTH), lambda l: (l, 0, 0)),
        ],
        out_shape=[jax.ShapeDtypeStruct((DEPTH, n, MEM_WIDTH), BF16)] * 2,
        compiler_params=pltpu.CompilerParams(vmem_limit_bytes=VMEM_LIMIT),
        name="memkv",
    )(mem2d, w_mem_k, w_mem_v)


def _mixer_a_kernel(x_ref, win_ref, wout_ref, sgg_ref, sgb_ref, sgw_ref, sgbias_ref,
                    mk_ref, mv_ref, g1_ref, b1_ref, h_ref):
    ts = x_ref.shape[0]
    x = x_ref[...]
    proj = _dot(x.astype(BF16), win_ref[...])
    u = _gelu(proj[:, :SG_WIDTH])
    gv = _gelu(proj[:, SG_WIDTH:2 * SG_WIDTH])
    gv = _layer_norm(gv, sgg_ref[...], sgb_ref[...]).astype(BF16)

    row = lax.broadcasted_iota(jnp.int32, (CHUNK, CHUNK), 0)
    col = lax.broadcasted_iota(jnp.int32, (CHUNK, CHUNK), 1)
    ws = [jnp.where(row >= col, sgw_ref[g], 0.0).astype(BF16) for g in range(N_SG)]
    low_half = col < HEAD_DIM
    rows = []
    for c in range(ts // CHUNK):
        vc = gv[c * CHUNK:(c + 1) * CHUNK]
        parts = []
        for j in range(N_SG // 2):
            vp = vc[:, j * LANES:(j + 1) * LANES]
            parts.append(jnp.where(low_half, _dot(ws[2 * j], vp), _dot(ws[2 * j + 1], vp)))
        rows.append(jnp.concatenate(parts, -1) + sgbias_ref[...])
    mix = u * jnp.concatenate(rows, 0)

    qm = (proj[:, 2 * SG_WIDTH:] * ATT_SCALE).astype(BF16)
    mo = _mem_attention(qm, mk_ref[...], mv_ref[...])
    att = _dot(jnp.concatenate([mix, mo], -1).astype(BF16), wout_ref[...])
    h = _layer_norm(ALPHA * x + att, g1_ref[...], b1_ref[...])
    _to_tiles(h_ref, h, ts)


def _mixer_a(x, w_in, w_out, sg_g, sg_b, sg_w, sg_bias, mk, mv, g1, b1):
    b, s, d = x.shape
    nts = s // TS_MIX
    m = mk.shape[0] // b
    full = lambda *shape: pl.BlockSpec(shape, lambda i, j: (0,) * len(shape))
    return pl.pallas_call(
        _mixer_a_kernel,
        grid=(b, nts),
        in_specs=[
            pl.BlockSpec((None, TS_MIX, d), lambda i, j: (i, j, 0)),
            full(d, 2 * SG_WIDTH + MEM_WIDTH),
            full(SG_WIDTH + MEM_WIDTH, d),
            full(1, SG_WIDTH),
            full(1, SG_WIDTH),
            full(N_SG, CHUNK, CHUNK),
            full(CHUNK, SG_WIDTH),
            pl.BlockSpec((m, MEM_WIDTH), lambda i, j: (i, 0)),
            pl.BlockSpec((m, MEM_WIDTH), lambda i, j: (i, 0)),
            full(1, d),
            full(1, d),
        ],
        out_specs=pl.BlockSpec((TS_MIX * N_CHUNKS, LANES), lambda i, j: (i * nts + j, 0)),
        out_shape=jax.ShapeDtypeStruct((b * s * N_CHUNKS, LANES), F32),
        compiler_params=pltpu.CompilerParams(vmem_limit_bytes=VMEM_LIMIT),
        name="mixer_a",
    )(x, w_in, w_out, sg_g, sg_b, sg_w, sg_bias, mk, mv, g1, b1)


def _router_kernel(h_ref, wr_ref, br_ref, pos_ref, gate_ref, runs_ref, counts_ref, base_ref):
    tm = TM_ROUTE

    @pl.when(pl.program_id(0) == 0)
    def _():
        base_ref[...] = jnp.zeros_like(base_ref)

    h = _from_tiles(h_ref, tm)
    w = wr_ref[...]
    h_hi = h.astype(BF16)
    h_lo = (h - h_hi.astype(F32)).astype(BF16)
    w_hi = w.astype(BF16)
    w_lo = (w - w_hi.astype(F32)).astype(BF16)
    logits = _dot(h_hi, w_hi) + (_dot(h_hi, w_lo) + _dot(h_lo, w_hi))
    lt = logits.T[:N_EXPERTS]

    ex = jnp.exp(lt - jnp.max(lt, 0, keepdims=True))
    probs = ex / jnp.sum(ex, 0, keepdims=True)
    sel = probs + br_ref[...]
    eid = lax.broadcasted_iota(jnp.int32, (N_EXPERTS, tm), 0)
    group = eid // EXPERTS_PER_GROUP

    def top2(mask):
        v = jnp.where(mask, sel, -jnp.inf)
        m1 = jnp.max(v, 0, keepdims=True)
        i1 = jnp.min(jnp.where(v == m1, eid, N_EXPERTS), 0, keepdims=True)
        v2 = jnp.where(eid == i1, -jnp.inf, v)
        m2 = jnp.max(v2, 0, keepdims=True)
        i2 = jnp.min(jnp.where(v2 == m2, eid, N_EXPERTS), 0, keepdims=True)
        return m1, i1, m2, i2

    scores = []
    for g in range(N_EXPERT_GROUPS):
        m1, _, m2, _ = top2(group == g)
        scores.append(m1 + m2)
    best = functools.reduce(jnp.maximum, scores)
    g_idx = jnp.full((1, tm), N_EXPERT_GROUPS - 1, jnp.int32)
    for g in reversed(range(N_EXPERT_GROUPS - 1)):
        g_idx = jnp.where(scores[g] == best, g, g_idx)
    _, e0, _, e1 = top2(group == g_idx)

    hot0 = eid == e0
    hot1 = eid == e1
    p0 = jnp.sum(jnp.where(hot0, probs, 0.0), 0, keepdims=True)
    p1 = jnp.sum(jnp.where(hot1, probs, 0.0), 0, keepdims=True)
    psum = p0 + p1

    hot = jnp.where(hot0 | hot1, 1.0, 0.0)
    earlier = (lax.broadcasted_iota(jnp.int32, (tm, tm), 0)
               < lax.broadcasted_iota(jnp.int32, (tm, tm), 1))
    hot_bf = hot.astype(BF16)
    before = _dot(hot_bf, jnp.where(earlier, 1.0, 0.0).astype(BF16))
    below = (lax.broadcasted_iota(jnp.int32, (N_EXPERTS, N_EXPERTS), 1)
             < lax.broadcasted_iota(jnp.int32, (N_EXPERTS, N_EXPERTS), 0))
    run_start = jnp.sum(_dot(jnp.where(below, 1.0, 0.0).astype(BF16), hot_bf), 1, keepdims=True)
    local = run_start + before
    q0 = jnp.sum(jnp.where(hot0, local, 0.0), 0, keepdims=True)
    q1 = jnp.sum(jnp.where(hot1, local, 0.0), 0, keepdims=True)

    row = lax.broadcasted_iota(jnp.int32, (SUBLANES, tm), 0)
    pos_ref[...] = jnp.where(row == 0, q0, jnp.where(row == 1, q1, 0.0)).astype(jnp.int32)
    grow = lax.broadcasted_iota(jnp.int32, (LANES, tm), 0)
    cols = jnp.where(grow == 0, p0 / psum, jnp.where(grow == 1, p1 / psum, jnp.where(
        grow == 2, q0, jnp.where(grow == 3, q1, 0.0))))
    gate_ref[...] = cols.T[:, :2 * TOP_K]
    count = jnp.sum(hot, 1, keepdims=True)
    lane = lax.broadcasted_iota(jnp.int32, (N_EXPERTS, LANES), 1)
    runs_ref[...] = jnp.where(lane == 0, count, jnp.where(
        lane == 1, base_ref[...], 0.0)).astype(jnp.int32)
    total = base_ref[...] + count
    base_ref[...] = total
    counts_ref[...] = total.astype(jnp.int32)


def _router(h_tiles, w_router, b_router):
    t = h_tiles.shape[0] // N_CHUNKS
    tm = TM_ROUTE
    return pl.pallas_call(
        _router_kernel,
        grid=(t // tm,),
        in_specs=[
            pl.BlockSpec((tm * N_CHUNKS, LANES), lambda i: (i, 0)),
            pl.BlockSpec((D_MODEL, LANES), lambda i: (0, 0)),
            pl.BlockSpec((N_EXPERTS, 1), lambda i: (0, 0)),
        ],
        out_specs=[
            pl.BlockSpec((None, SUBLANES, tm), lambda i: (i, 0, 0)),
            pl.BlockSpec((tm, 2 * TOP_K), lambda i: (i, 0)),
            pl.BlockSpec((None, N_EXPERTS, LANES), lambda i: (i, 0, 0)),
            pl.BlockSpec((N_EXPERTS, 1), lambda i: (0, 0)),
        ],
        out_shape=[
            jax.ShapeDtypeStruct((t // tm, SUBLANES, tm), jnp.int32),
            jax.ShapeDtypeStruct((t, 2 * TOP_K), F32),
            jax.ShapeDtypeStruct((t // tm, N_EXPERTS, LANES), jnp.int32),
            jax.ShapeDtypeStruct((N_EXPERTS, 1), jnp.int32),
        ],
        scratch_shapes=[pltpu.VMEM((N_EXPERTS, 1), F32)],
        compiler_params=pltpu.CompilerParams(
            dimension_semantics=("arbitrary",), vmem_limit_bytes=VMEM_LIMIT),
        name="router",
    )(h_tiles, w_router, b_router)


def _start_run_copies(runs_ref, start_ref, tile_ref, rows_ref, sem, to_rows):
    offset = 0
    for e in range(N_EXPERTS):
        count = runs_ref[e, 0]
        first_row = start_ref[e] + runs_ref[e, 1]
        for bit in reversed(range(RUN_BITS)):
            size = (1 << bit) * N_CHUNKS
            done = (count >> (bit + 1)) << (bit + 1)

            @pl.when(((count >> bit) & 1) == 1)
            def _():
                in_tile = tile_ref.at[
                    pl.ds(pl.multiple_of((offset + done) * N_CHUNKS, N_CHUNKS), size)]
                in_rows = rows_ref.at[
                    pl.ds(pl.multiple_of((first_row + done) * N_CHUNKS, N_CHUNKS), size)]
                if to_rows:
                    pltpu.make_async_copy(in_tile, in_rows, sem).start()
                else:
                    pltpu.make_async_copy(in_rows, in_tile, sem).start()
        offset = offset + count


def _wait_run_copies(tile_ref, rows_ref, sem):
    pltpu.make_async_copy(tile_ref, rows_ref.at[pl.ds(0, tile_ref.shape[0])], sem).wait()


def _dispatch_kernel(pos_ref, runs_ref, start_ref, end_ref, cnt_ref, h_ref, xs_ref,
                     zero_ref, sorted_ref, sem, zsem):
    i = pl.program_id(0)
    tm = TM_ROUTE

    @pl.when(i == 0)
    def _():
        zero_ref[...] = jnp.zeros_like(zero_ref)
        for e in range(N_EXPERTS):
            @pl.when(cnt_ref[e, 0] > 0)
            def _():
                first = pl.multiple_of((end_ref[e] - TB_MOE) * N_CHUNKS, N_CHUNKS)
                cp = pltpu.make_async_copy(
                    zero_ref, xs_ref.at[pl.ds(first, TB_MOE * N_CHUNKS)], zsem)
                cp.start()
                cp.wait()

        def zero_block(blk, carry):
            first = pl.multiple_of(blk * (TB_MOE * N_CHUNKS), TB_MOE * N_CHUNKS)
            cp = pltpu.make_async_copy(
                zero_ref, xs_ref.at[pl.ds(first, TB_MOE * N_CHUNKS)], zsem)
            cp.start()
            cp.wait()
            return carry

        n_blocks = xs_ref.shape[0] // (TB_MOE * N_CHUNKS)
        lax.fori_loop(end_ref[N_EXPERTS - 1] // TB_MOE, n_blocks, zero_block, 0)

    p = lax.broadcasted_iota(jnp.int32, (TOP_K * tm, tm), 0)
    onehot = (p == pos_ref[0:1, :]) | (p == pos_ref[1:2, :])
    sorted_rows = _dot(jnp.where(onehot, 1.0, 0.0).astype(BF16),
                       _from_tiles(h_ref, tm).astype(BF16))
    _to_tiles(sorted_ref, sorted_rows, TOP_K * tm)
    _start_run_copies(runs_ref, start_ref, sorted_ref, xs_ref, sem, to_rows=True)
    _wait_run_copies(sorted_ref, xs_ref, sem)


def _dispatch(pos, runs, pad_starts, pad_ends, counts, h_tiles, n_rows):
    nt = pos.shape[0]
    tm = TM_ROUTE
    smem_vec = pl.BlockSpec(memory_space=pltpu.SMEM)
    return pl.pallas_call(
        _dispatch_kernel,
        grid=(nt,),
        in_specs=[
            pl.BlockSpec((None, SUBLANES, tm), lambda i: (i, 0, 0)),
            pl.BlockSpec((None, N_EXPERTS, LANES), lambda i: (i, 0, 0), memory_space=pltpu.SMEM),
            smem_vec, smem_vec, smem_vec,
            pl.BlockSpec((tm * N_CHUNKS, LANES), lambda i: (i, 0))],
        out_specs=pl.BlockSpec(memory_space=pl.ANY),
        out_shape=jax.ShapeDtypeStruct((n_rows * N_CHUNKS, LANES), F32),
        scratch_shapes=[
            pltpu.VMEM((TB_MOE * N_CHUNKS, LANES), F32),
            pltpu.VMEM((TOP_K * tm * N_CHUNKS, LANES), F32),
            pltpu.SemaphoreType.DMA(()),
            pltpu.SemaphoreType.DMA(()),
        ],
        compiler_params=pltpu.CompilerParams(
            dimension_semantics=("arbitrary",), has_side_effects=True,
            vmem_limit_bytes=VMEM_LIMIT),
        name="dispatch",
    )(pos, runs, pad_starts, pad_ends, counts, h_tiles)


def _experts_kernel(be_ref, nused_ref, xs_ref, wg_ref, wu_ref, wd_ref, ys_ref,
                    wg_bf, wu_bf, wd_bf):
    i = pl.program_id(0)
    nused = nused_ref[0]

    @pl.when(i < nused)
    def _():
        prev = be_ref[jnp.maximum(i - 1, 0)]

        @pl.when((i == 0) | (be_ref[i] != prev))
        def _():
            wg_bf[...] = wg_ref[...].astype(BF16)
            wu_bf[...] = wu_ref[...].astype(BF16)
            wd_bf[...] = wd_ref[...].astype(BF16)

        x = _from_tiles(xs_ref, TB_MOE).astype(BF16)
        hg = _dot(x, wg_bf[...])
        hu = _dot(x, wu_bf[...])
        act = (hg * jax.nn.sigmoid(hg) * hu).astype(BF16)
        _to_tiles(ys_ref, _dot(act, wd_bf[...]), TB_MOE)

    @pl.when(i >= nused)
    def _():
        ys_ref[...] = jnp.zeros_like(ys_ref)


def _experts(block_expert, nused, xs, w_gate, w_up, w_down, layer):
    n_blocks = xs.shape[0] // (TB_MOE * N_CHUNKS)
    d, de = w_gate.shape[2], w_gate.shape[3]

    def x_map(i, be, nu):
        return (jnp.minimum(i, nu[0] - 1), 0)

    def w_map(i, be, nu):
        return (layer, be[jnp.minimum(i, nu[0] - 1)], 0, 0)

    return pl.pallas_call(
        _experts_kernel,
        grid_spec=pltpu.PrefetchScalarGridSpec(
            num_scalar_prefetch=2,
            grid=(n_blocks,),
            in_specs=[
                pl.BlockSpec((TB_MOE * N_CHUNKS, LANES), x_map),
                pl.BlockSpec((None, None, d, de), w_map),
                pl.BlockSpec((None, None, d, de), w_map),
                pl.BlockSpec((None, None, de, d), w_map),
            ],
            out_specs=pl.BlockSpec((TB_MOE * N_CHUNKS, LANES), lambda i, be, nu: (i, 0)),
            scratch_shapes=[
                pltpu.VMEM((d, de), BF16),
                pltpu.VMEM((d, de), BF16),
                pltpu.VMEM((de, d), BF16),
            ],
        ),
        out_shape=jax.ShapeDtypeStruct(xs.shape, F32),
        compiler_params=pltpu.CompilerParams(
            dimension_semantics=("arbitrary",), vmem_limit_bytes=VMEM_LIMIT),
        name="experts",
    )(block_expert, nused, xs, w_gate, w_up, w_down)


def _combine_kernel(runs_ref, start_ref, h_ref, gate_ref, g2_ref, b2_ref, ys_ref,
                    out_ref, sorted_ref, sem, *, chunk_major):
    tm = TM_ROUTE
    _start_run_copies(runs_ref, start_ref, sorted_ref, ys_ref, sem, to_rows=False)
    _wait_run_copies(sorted_ref, ys_ref, sem)

    gate = gate_ref[...]
    p = lax.broadcasted_iota(jnp.int32, (tm, TOP_K * tm), 1)
    onehot = jnp.concatenate([p == gate[:, 2:3].astype(jnp.int32),
                              p == gate[:, 3:4].astype(jnp.int32)], 0)
    picked = _dot(jnp.where(onehot, 1.0, 0.0).astype(BF16),
                  _from_tiles(sorted_ref, TOP_K * tm).astype(BF16))
    ffn = picked[:tm] * gate[:, 0:1] + picked[tm:] * gate[:, 1:2]
    out = _layer_norm(ALPHA * _from_tiles(h_ref, tm) + ffn, g2_ref[...], b2_ref[...])
    if chunk_major:
        for c in range(N_CHUNKS):
            out_ref[c] = out[:, c * LANES:(c + 1) * LANES]
    else:
        out_ref[...] = out


def _combine(runs, pad_starts, h_tiles, gate, g2, b2, ys, batch, seq, chunk_major):
    tm = TM_ROUTE
    t = batch * seq
    nt = t // tm
    nts = seq // tm
    if chunk_major:
        out_spec = pl.BlockSpec((None, N_CHUNKS, tm, LANES), lambda i: (i // nts, 0, i % nts, 0))
        out_shape = jax.ShapeDtypeStruct((batch, N_CHUNKS, seq, LANES), F32)
    else:
        out_spec = pl.BlockSpec((None, tm, D_MODEL), lambda i: (i // nts, i % nts, 0))
        out_shape = jax.ShapeDtypeStruct((batch, seq, D_MODEL), F32)
    return pl.pallas_call(
        functools.partial(_combine_kernel, chunk_major=chunk_major),
        grid=(nt,),
        in_specs=[
            pl.BlockSpec((None, N_EXPERTS, LANES), lambda i: (i, 0, 0), memory_space=pltpu.SMEM),
            pl.BlockSpec(memory_space=pltpu.SMEM),
            pl.BlockSpec((tm * N_CHUNKS, LANES), lambda i: (i, 0)),
            pl.BlockSpec((tm, 2 * TOP_K), lambda i: (i, 0)),
            pl.BlockSpec((1, D_MODEL), lambda i: (0, 0)),
            pl.BlockSpec((1, D_MODEL), lambda i: (0, 0)),
            pl.BlockSpec(memory_space=pl.ANY),
        ],
        out_specs=out_spec,
        out_shape=out_shape,
        scratch_shapes=[
            pltpu.VMEM((TOP_K * tm * N_CHUNKS, LANES), F32),
            pltpu.SemaphoreType.DMA(()),
        ],
        compiler_params=pltpu.CompilerParams(
            dimension_semantics=("arbitrary",), vmem_limit_bytes=VMEM_LIMIT),
        name="combine",
    )(runs, pad_starts, h_tiles, gate, g2, b2, ys)


def _moe(h_tiles, batch, seq, w_router, b_router, w_gate, w_up, w_down, g2, b2, layer,
         chunk_major):
    t = batch * seq
    pos, gate, runs, counts = _router(h_tiles, w_router, b_router)
    cnt = counts[:, 0]
    padded = (cnt + TB_MOE - 1) // TB_MOE * TB_MOE
    pad_ends = jnp.cumsum(padded).astype(jnp.int32)
    pad_starts = pad_ends - padded
    n_rows = t * TOP_K + N_EXPERTS * TB_MOE
    n_blocks = n_rows // TB_MOE
    block_start = jnp.arange(n_blocks, dtype=jnp.int32) * TB_MOE
    block_expert = jnp.minimum(
        jnp.sum(block_start[:, None] >= pad_ends[None, :], -1), N_EXPERTS - 1).astype(jnp.int32)
    nused = (pad_ends[-1:] // TB_MOE).astype(jnp.int32)
    xs = _dispatch(pos, runs, pad_starts, pad_ends, counts, h_tiles, n_rows)
    ys = _experts(block_expert, nused, xs, w_gate, w_up, w_down, layer)
    return _combine(runs, pad_starts, h_tiles, gate, g2, b2, ys, batch, seq, chunk_major)


def _residue_rows(x_ref, dilation):
    n = x_ref.shape[1]
    if dilation == 1:
        return _from_chunks(x_ref)
    per = n // dilation
    return jnp.concatenate(
        [jnp.concatenate([x_ref[c, pl.ds(r, per, stride=dilation), :] for r in range(dilation)], 0)
         for c in range(N_CHUNKS)], -1)


def _proj_b_kernel(x_ref, wqkv_ref, wqm_ref, qkv0_ref, qkv1_ref, qkv2_ref, qm_ref):
    ts = x_ref.shape[1]
    qscale = jnp.where(
        lax.broadcasted_iota(jnp.int32, (1, 3 * DIL_OUT_WIDTH), 1) < DIL_OUT_WIDTH, ATT_SCALE, 1.0)
    for g, (out_ref, (_, dilation)) in enumerate(zip((qkv0_ref, qkv1_ref, qkv2_ref), DIL_PAIRS)):
        xb = _residue_rows(x_ref, dilation).astype(BF16)
        qkv = (_dot(xb, wqkv_ref[g]) * qscale).astype(BF16)
        per = ts // dilation
        for r in range(dilation):
            out_ref[r] = qkv[r * per:(r + 1) * per]
    qm = _dot(_from_chunks(x_ref).astype(BF16), wqm_ref[...])
    qm_ref[...] = (qm * ATT_SCALE).astype(BF16)


def _proj_b(x_cm, wqkv, wqm):
    b, _, s, _ = x_cm.shape
    ts = TS_MIX
    width = 3 * DIL_OUT_WIDTH
    out_specs, out_shapes = [], []
    for _, dilation in DIL_PAIRS:
        out_specs.append(pl.BlockSpec((None, dilation, ts // dilation, width),
                                      lambda i, j: (i, 0, j, 0)))
        out_shapes.append(jax.ShapeDtypeStruct((b, dilation, s // dilation, width), BF16))
    out_specs.append(pl.BlockSpec((None, ts, MEM_WIDTH), lambda i, j: (i, j, 0)))
    out_shapes.append(jax.ShapeDtypeStruct((b, s, MEM_WIDTH), BF16))
    return pl.pallas_call(
        _proj_b_kernel,
        grid=(b, s // ts),
        in_specs=[
            pl.BlockSpec((None, N_CHUNKS, ts, LANES), lambda i, j: (i, 0, j, 0)),
            pl.BlockSpec(wqkv.shape, lambda i, j: (0, 0, 0)),
            pl.BlockSpec(wqm.shape, lambda i, j: (0, 0)),
        ],
        out_specs=out_specs,
        out_shape=out_shapes,
        compiler_params=pltpu.CompilerParams(vmem_limit_bytes=VMEM_LIMIT),
        name="proj_b",
    )(x_cm, wqkv, wqm)


def _dil_attn_kernel(qkv_ref, o_ref, lse_ref, *, dilation, slopes):
    nb = qkv_ref.shape[1] // CHUNK
    width = DIL_OUT_WIDTH
    iq = lax.broadcasted_iota(jnp.int32, (CHUNK, 2 * CHUNK), 0)
    jk = lax.broadcasted_iota(jnp.int32, (CHUNK, 2 * CHUNK), 1)
    rel = CHUNK + iq - jk
    in_window = (rel >= 0) & (rel <= CHUNK)
    dist = (dilation * rel).astype(F32)
    head = lax.broadcasted_iota(jnp.int32, (1, width), 1) // HEAD_DIM

    def one_block(idx):
        r = idx // nb
        jb = idx % nb
        cur = pl.ds(pl.multiple_of(jb * CHUNK, CHUNK), CHUNK)
        prev = pl.ds(pl.multiple_of(jnp.maximum(jb - 1, 0) * CHUNK, CHUNK), CHUNK)
        q = qkv_ref[r, cur, 0:width]
        kband = jnp.concatenate([qkv_ref[r, prev, width:2 * width],
                                 qkv_ref[r, cur, width:2 * width]], 0)
        vband = jnp.concatenate([qkv_ref[r, prev, 2 * width:3 * width],
                                 qkv_ref[r, cur, 2 * width:3 * width]], 0)
        valid = in_window & ((jk >= CHUNK) | (jb > 0))
        out = jnp.zeros((CHUNK, width), F32)
        lse = jnp.zeros((CHUNK, width), F32)
        for h in range(HEADS_PER_DIL):
            sel = head == h
            s = _dot_nt(jnp.where(sel, q, jnp.zeros_like(q)), kband)
            s = jnp.where(valid, s - slopes[h] * dist, -jnp.inf)
            m = jnp.max(s, -1, keepdims=True)
            p = jnp.exp(s - m)
            l = jnp.sum(p, -1, keepdims=True)
            out = out + _dot((p * (1.0 / l)).astype(BF16),
                             jnp.where(sel, vband, jnp.zeros_like(vband)))
            lse = jnp.where(sel, m + jnp.log(l), lse)
        o_ref[r, cur, :] = out
        lse_ref[r, cur, :] = lse

    def pair(i, carry):
        one_block(2 * i)
        one_block(2 * i + 1)
        return carry

    lax.fori_loop(0, dilation * nb // 2, pair, 0)


def _dil_attn(qkv, group):
    b, dilation, length, _ = qkv.shape
    slopes = tuple(2.0 ** (-8.0 * (group * HEADS_PER_DIL + h + 1) / N_DIL_HEADS)
                   for h in range(HEADS_PER_DIL))
    out_spec = pl.BlockSpec((None, dilation, length, DIL_OUT_WIDTH), lambda i: (i, 0, 0, 0))
    out_shape = jax.ShapeDtypeStruct((b, dilation, length, DIL_OUT_WIDTH), F32)
    return pl.pallas_call(
        functools.partial(_dil_attn_kernel, dilation=dilation, slopes=slopes),
        grid=(b,),
        in_specs=[pl.BlockSpec((None, dilation, length, 3 * DIL_OUT_WIDTH),
                               lambda i: (i, 0, 0, 0))],
        out_specs=[out_spec, out_spec],
        out_shape=[out_shape, out_shape],
        compiler_params=pltpu.CompilerParams(vmem_limit_bytes=VMEM_LIMIT),
        name=f"dil_attn_{group}",
    )(qkv)


def _token_order(src_ref, scratch, dilation):
    if dilation == 1:
        return src_ref[0]
    per = src_ref.shape[1]
    halves = DIL_OUT_WIDTH // LANES
    for r in range(dilation):
        v = src_ref[r]
        for c in range(halves):
            scratch[c, pl.ds(r, per, stride=dilation), :] = v[:, c * LANES:(c + 1) * LANES]
    return jnp.concatenate([scratch[c] for c in range(halves)], -1)


def _mixer_b_kernel(x_ref, qm_ref, o0_ref, l0_ref, o1_ref, l1_ref, o2_ref, l2_ref,
                    mk_ref, mv_ref, wout_ref, g1_ref, b1_ref, h_ref,
                    so1, sl1, so2, sl2):
    ts = x_ref.shape[1]
    outs = [o0_ref[0], _token_order(o1_ref, so1, DIL_PAIRS[1][1]),
            _token_order(o2_ref, so2, DIL_PAIRS[2][1])]
    lses = [l0_ref[0], _token_order(l1_ref, sl1, DIL_PAIRS[1][1]),
            _token_order(l2_ref, sl2, DIL_PAIRS[2][1])]
    top = functools.reduce(jnp.maximum, lses)
    es = [jnp.exp(l - top) for l in lses]
    inv = 1.0 / functools.reduce(jnp.add, es)
    mix = functools.reduce(jnp.add, [e * inv * o for e, o in zip(es, outs)])
    mo = _mem_attention(qm_ref[...], mk_ref[...], mv_ref[...])
    att = _dot(jnp.concatenate([mix, mo], -1).astype(BF16), wout_ref[...])
    h = _layer_norm(ALPHA * _from_chunks(x_ref) + att, g1_ref[...], b1_ref[...])
    _to_tiles(h_ref, h, ts)


def _mixer_b(x_cm, qm, attn, mk, mv, w_out, g1, b1):
    b, _, s, _ = x_cm.shape
    ts = TS_MIX
    nts = s // ts
    m = mk.shape[0] // b
    in_specs = [
        pl.BlockSpec((None, N_CHUNKS, ts, LANES), lambda i, j: (i, 0, j, 0)),
        pl.BlockSpec((None, ts, MEM_WIDTH), lambda i, j: (i, j, 0)),
    ]
    args = [x_cm, qm]
    for (o, lse), (_, dilation) in zip(attn, DIL_PAIRS):
        spec = pl.BlockSpec((None, dilation, ts // dilation, DIL_OUT_WIDTH),
                            lambda i, j: (i, 0, j, 0))
        in_specs += [spec, spec]
        args += [o, lse]
    in_specs += [
        pl.BlockSpec((m, MEM_WIDTH), lambda i, j: (i, 0)),
        pl.BlockSpec((m, MEM_WIDTH), lambda i, j: (i, 0)),
        pl.BlockSpec(w_out.shape, lambda i, j: (0, 0)),
        pl.BlockSpec((1, D_MODEL), lambda i, j: (0, 0)),
        pl.BlockSpec((1, D_MODEL), lambda i, j: (0, 0)),
    ]
    args += [mk, mv, w_out, g1, b1]
    halves = DIL_OUT_WIDTH // LANES
    return pl.pallas_call(
        _mixer_b_kernel,
        grid=(b, nts),
        in_specs=in_specs,
        out_specs=pl.BlockSpec((ts * N_CHUNKS, LANES), lambda i, j: (i * nts + j, 0)),
        out_shape=jax.ShapeDtypeStruct((b * s * N_CHUNKS, LANES), F32),
        scratch_shapes=[pltpu.VMEM((halves, ts, LANES), F32)] * 4,
        compiler_params=pltpu.CompilerParams(vmem_limit_bytes=VMEM_LIMIT),
        name="mixer_b",
    )(*args)


def kernel(x, mem, w_in_a, w_out_a, sg_ln_g, sg_ln_b, sg_w, sg_b, w_in_b, w_out_b, w_k_shared,
           w_v_shared, w_mem_k, w_mem_v, ln1_g, ln1_b, ln2_g, ln2_b, w_router, b_router,
           w_gate, w_up, w_down):
    batch, seq, d = x.shape
    mk, mv = _memkv(mem.reshape(-1, d), w_mem_k, w_mem_v)
    row = lambda v: v.reshape(1, -1)

    sg_bias = jnp.repeat(sg_b[0].T, HEAD_DIM, axis=1)
    h = _mixer_a(x, w_in_a[0].astype(BF16), w_out_a[0].astype(BF16), row(sg_ln_g[0]),
                 row(sg_ln_b[0]), sg_w[0], sg_bias, mk[0], mv[0], row(ln1_g[0]), row(ln1_b[0]))
    w_router_pad = jnp.pad(w_router, ((0, 0), (0, LANES - N_EXPERTS)))
    b_router_col = b_router.reshape(N_EXPERTS, 1)
    x_cm = _moe(h, batch, seq, w_router_pad, b_router_col, w_gate, w_up, w_down,
                row(ln2_g[0]), row(ln2_b[0]), layer=0, chunk_major=True)

    wqkv = jnp.stack([
        jnp.concatenate([w[:, g * DIL_OUT_WIDTH:(g + 1) * DIL_OUT_WIDTH]
                         for w in (w_in_b[0], w_k_shared, w_v_shared)], -1)
        for g in range(len(DIL_PAIRS))]).astype(BF16)
    wqm = w_in_b[0][:, DIL_Q_WIDTH:].astype(BF16)
    qkv0, qkv1, qkv2, qm = _proj_b(x_cm, wqkv, wqm)
    attn = [_dil_attn(qkv, g) for g, qkv in enumerate((qkv0, qkv1, qkv2))]
    h = _mixer_b(x_cm, qm, attn, mk[1], mv[1], w_out_b[0].astype(BF16),
                 row(ln1_g[1]), row(ln1_b[1]))
    return _moe(h, batch, seq, w_router_pad, b_router_col, w_gate, w_up, w_down,
                row(ln2_g[1]), row(ln2_b[1]), layer=1, chunk_major=False)
```

```python
import functools
import math

import jax
import jax.numpy as jnp
from jax import lax
from jax.experimental import pallas as pl
from jax.experimental.pallas import tpu as pltpu

D_MODEL = 1024
HEAD_DIM = 64
CHUNK = 128
N_SG = 12
SG_WIDTH = N_SG * HEAD_DIM
DIL_PAIRS = ((128, 1), (512, 4), (2048, 16))
HEADS_PER_DIL = 4
N_DIL_HEADS = HEADS_PER_DIL * len(DIL_PAIRS)
DIL_Q_WIDTH = N_DIL_HEADS * HEAD_DIM
DIL_OUT_WIDTH = HEADS_PER_DIL * HEAD_DIM
MEM_HEADS = 4
MEM_WIDTH = MEM_HEADS * HEAD_DIM
N_EXPERTS = 16
N_EXPERT_GROUPS = 4
EXPERTS_PER_GROUP = N_EXPERTS // N_EXPERT_GROUPS
TOP_K = 2
DEPTH = 2
ALPHA = (2 * DEPTH) ** 0.25
LN_EPS = 1e-5
ATT_SCALE = 1.0 / math.sqrt(HEAD_DIM)

LANES = 128
SUBLANES = 8
N_CHUNKS = D_MODEL // LANES

TS_MIX = 512
MXU_DEPTH = 256
TM_ROUTE = MXU_DEPTH
RUN_BITS = TM_ROUTE.bit_length()
TB_MOE = 512
VMEM_LIMIT = 56 * 1024 * 1024

F32 = jnp.float32
BF16 = jnp.bfloat16


def _dot(a, b):
    return jnp.dot(a, b, preferred_element_type=F32)


def _dot_nt(a, b):
    return lax.dot_general(a, b, (((1,), (1,)), ((), ())), preferred_element_type=F32)


def _layer_norm(x, g, b):
    mu = jnp.mean(x, -1, keepdims=True)
    xc = x - mu
    var = jnp.mean(xc * xc, -1, keepdims=True)
    return xc * lax.rsqrt(var + LN_EPS) * g + b


def _gelu(x):
    return 0.5 * x * (1.0 + lax.erf(x * (1.0 / math.sqrt(2.0))))


def _from_tiles(ref, n):
    return jnp.concatenate(
        [ref[pl.ds(c, n, stride=N_CHUNKS), :] for c in range(N_CHUNKS)], -1)


def _to_tiles(ref, val, n):
    for c in range(N_CHUNKS):
        ref[pl.ds(c, n, stride=N_CHUNKS), :] = val[:, c * LANES:(c + 1) * LANES]


def _from_chunks(ref):
    return jnp.concatenate([ref[c] for c in range(N_CHUNKS)], -1)


def _mem_attention(q, mk, mv):
    n = q.shape[0]
    head = lax.broadcasted_iota(jnp.int32, (1, MEM_WIDTH), 1) // HEAD_DIM
    stacked = jnp.concatenate(
        [jnp.where(head == h, q, jnp.zeros_like(q)) for h in range(MEM_HEADS)], 0)
    s = _dot_nt(stacked, mk)
    m = jnp.max(s, -1, keepdims=True)
    p = jnp.exp(s - m)
    l = jnp.sum(p, -1, keepdims=True)
    o = _dot((p * (1.0 / l)).astype(BF16), mv)
    out = jnp.zeros((n, MEM_WIDTH), F32)
    for h in range(MEM_HEADS):
        out = jnp.where(head == h, o[h * n:(h + 1) * n], out)
    return out


def _memkv_kernel(mem_ref, wk_ref, wv_ref, mk_ref, mv_ref):
    m = mem_ref[...].astype(BF16)
    mk_ref[...] = _dot(m, wk_ref[...].astype(BF16)).astype(BF16)
    mv_ref[...] = _dot(m, wv_ref[...].astype(BF16)).astype(BF16)


def _memkv(mem2d, w_mem_k, w_mem_v):
    n = mem2d.shape[0]
    return pl.pallas_call(
        _memkv_kernel,
        grid=(DEPTH,),
        in_specs=[
            pl.BlockSpec((n, D_MODEL), lambda l: (0, 0)),
            pl.BlockSpec((None, D_MODEL, MEM_WIDTH), lambda l: (l, 0, 0)),
            pl.BlockSpec((None, D_MODEL, MEM_WIDTH), lambda l: (l, 0, 0)),
        ],
        out_specs=[
            pl.BlockSpec((None, n, MEM_WIDTH), lambda l: (l, 0, 0)),
            pl.BlockSpec((None, n, MEM_WIDTH), lambda l: (l, 0, 0)),
        ],
        out_shape=[jax.ShapeDtypeStruct((DEPTH, n, MEM_WIDTH), BF16)] * 2,
        compiler_params=pltpu.CompilerParams(vmem_limit_bytes=VMEM_LIMIT),
        name="memkv",
    )(mem2d, w_mem_k, w_mem_v)


def _mixer_a_kernel(x_ref, win_ref, wout_ref, sgg_ref, sgb_ref, sgw_ref, sgbias_ref,
                    mk_ref, mv_ref, g1_ref, b1_ref, h_ref):
    ts = x_ref.shape[0]
    x = x_ref[...]
    proj = _dot(x.astype(BF16), win_ref[...])
    u = _gelu(proj[:, :SG_WIDTH])
    gv = _gelu(proj[:, SG_WIDTH:2 * SG_WIDTH])
    gv = _layer_norm(gv, sgg_ref[...], sgb_ref[...]).astype(BF16)

    row = lax.broadcasted_iota(jnp.int32, (CHUNK, CHUNK), 0)
    col = lax.broadcasted_iota(jnp.int32, (CHUNK, CHUNK), 1)
    ws = [jnp.where(row >= col, sgw_ref[g], 0.0).astype(BF16) for g in range(N_SG)]
    low_half = col < HEAD_DIM
    rows = []
    for c in range(ts // CHUNK):
        vc = gv[c * CHUNK:(c + 1) * CHUNK]
        parts = []
        for j in range(N_SG // 2):
            vp = vc[:, j * LANES:(j + 1) * LANES]
            parts.append(jnp.where(low_half, _dot(ws[2 * j], vp), _dot(ws[2 * j + 1], vp)))
        rows.append(jnp.concatenate(parts, -1) + sgbias_ref[...])
    mix = u * jnp.concatenate(rows, 0)

    qm = (proj[:, 2 * SG_WIDTH:] * ATT_SCALE).astype(BF16)
    mo = _mem_attention(qm, mk_ref[...], mv_ref[...])
    att = _dot(jnp.concatenate([mix, mo], -1).astype(BF16), wout_ref[...])
    h = _layer_norm(ALPHA * x + att, g1_ref[...], b1_ref[...])
    _to_tiles(h_ref, h, ts)


def _mixer_a(x, w_in, w_out, sg_g, sg_b, sg_w, sg_bias, mk, mv, g1, b1):
    b, s, d = x.shape
    nts = s // TS_MIX
    m = mk.shape[0] // b
    full = lambda *shape: pl.BlockSpec(shape, lambda i, j: (0,) * len(shape))
    return pl.pallas_call(
        _mixer_a_kernel,
        grid=(b, nts),
        in_specs=[
            pl.BlockSpec((None, TS_MIX, d), lambda i, j: (i, j, 0)),
            full(d, 2 * SG_WIDTH + MEM_WIDTH),
            full(SG_WIDTH + MEM_WIDTH, d),
            full(1, SG_WIDTH),
            full(1, SG_WIDTH),
            full(N_SG, CHUNK, CHUNK),
            full(CHUNK, SG_WIDTH),
            pl.BlockSpec((m, MEM_WIDTH), lambda i, j: (i, 0)),
            pl.BlockSpec((m, MEM_WIDTH), lambda i, j: (i, 0)),
            full(1, d),
            full(1, d),
        ],
        out_specs=pl.BlockSpec((TS_MIX * N_CHUNKS, LANES), lambda i, j: (i * nts + j, 0)),
        out_shape=jax.ShapeDtypeStruct((b * s * N_CHUNKS, LANES), F32),
        compiler_params=pltpu.CompilerParams(vmem_limit_bytes=VMEM_LIMIT),
        name="mixer_a",
    )(x, w_in, w_out, sg_g, sg_b, sg_w, sg_bias, mk, mv, g1, b1)


def _router_kernel(h_ref, wr_ref, br_ref, pos_ref, gate_ref, runs_ref, counts_ref, base_ref):
    tm = TM_ROUTE

    @pl.when(pl.program_id(0) == 0)
    def _():
        base_ref[...] = jnp.zeros_like(base_ref)

    h = _from_tiles(h_ref, tm)
    w = wr_ref[...]
    h_hi = h.astype(BF16)
    h_lo = (h - h_hi.astype(F32)).astype(BF16)
    w_hi = w.astype(BF16)
    w_lo = (w - w_hi.astype(F32)).astype(BF16)
    logits = _dot(h_hi, w_hi) + (_dot(h_hi, w_lo) + _dot(h_lo, w_hi))
    lt = logits.T[:N_EXPERTS]

    ex = jnp.exp(lt - jnp.max(lt, 0, keepdims=True))
    probs = ex / jnp.sum(ex, 0, keepdims=True)
    sel = probs + br_ref[...]
    eid = lax.broadcasted_iota(jnp.int32, (N_EXPERTS, tm), 0)
    group = eid // EXPERTS_PER_GROUP

    def top2(mask):
        v = jnp.where(mask, sel, -jnp.inf)
        m1 = jnp.max(v, 0, keepdims=True)
        i1 = jnp.min(jnp.where(v == m1, eid, N_EXPERTS), 0, keepdims=True)
        v2 = jnp.where(eid == i1, -jnp.inf, v)
        m2 = jnp.max(v2, 0, keepdims=True)
        i2 = jnp.min(jnp.where(v2 == m2, eid, N_EXPERTS), 0, keepdims=True)
        return m1, i1, m2, i2

    scores = []
    for g in range(N_EXPERT_GROUPS):
        m1, _, m2, _ = top2(group == g)
        scores.append(m1 + m2)
    best = functools.reduce(jnp.maximum, scores)
    g_idx = jnp.full((1, tm), N_EXPERT_GROUPS - 1, jnp.int32)
    for g in reversed(range(N_EXPERT_GROUPS - 1)):
        g_idx = jnp.where(scores[g] == best, g, g_idx)
    _, e0, _, e1 = top2(group == g_idx)

    hot0 = eid == e0
    hot1 = eid == e1
    p0 = jnp.sum(jnp.where(hot0, probs, 0.0), 0, keepdims=True)
    p1 = jnp.sum(jnp.where(hot1, probs, 0.0), 0, keepdims=True)
    psum = p0 + p1

    hot = jnp.where(hot0 | hot1, 1.0, 0.0)
    earlier = (lax.broadcasted_iota(jnp.int32, (tm, tm), 0)
               < lax.broadcasted_iota(jnp.int32, (tm, tm), 1))
    hot_bf = hot.astype(BF16)
    before = _dot(hot_bf, jnp.where(earlier, 1.0, 0.0).astype(BF16))
    below = (lax.broadcasted_iota(jnp.int32, (N_EXPERTS, N_EXPERTS), 1)
             < lax.broadcasted_iota(jnp.int32, (N_EXPERTS, N_EXPERTS), 0))
    run_start = jnp.sum(_dot(jnp.where(below, 1.0, 0.0).astype(BF16), hot_bf), 1, keepdims=True)
    local = run_start + before
    q0 = jnp.sum(jnp.where(hot0, local, 0.0), 0, keepdims=True)
    q1 = jnp.sum(jnp.where(hot1, local, 0.0), 0, keepdims=True)

    row = lax.broadcasted_iota(jnp.int32, (SUBLANES, tm), 0)
    pos_ref[...] = jnp.where(row == 0, q0, jnp.where(row == 1, q1, 0.0)).astype(jnp.int32)
    grow = lax.broadcasted_iota(jnp.int32, (LANES, tm), 0)
    cols = jnp.where(grow == 0, p0 / psum, jnp.where(grow == 1, p1 / psum, jnp.where(
        grow == 2, q0, jnp.where(grow == 3, q1, 0.0))))
    gate_ref[...] = cols.T[:, :2 * TOP_K]
    count = jnp.sum(hot, 1, keepdims=True)
    lane = lax.broadcasted_iota(jnp.int32, (N_EXPERTS, LANES), 1)
    runs_ref[...] = jnp.where(lane == 0, count, jnp.where(
        lane == 1, base_ref[...], 0.0)).astype(jnp.int32)
    total = base_ref[...] + count
    base_ref[...] = total
    counts_ref[...] = total.astype(jnp.int32)


def _router(h_tiles, w_router, b_router):
    t = h_tiles.shape[0] // N_CHUNKS
    tm = TM_ROUTE
    return pl.pallas_call(
        _router_kernel,
        grid=(t // tm,),
        in_specs=[
            pl.BlockSpec((tm * N_CHUNKS, LANES), lambda i: (i, 0)),
            pl.BlockSpec((D_MODEL, LANES), lambda i: (0, 0)),
            pl.BlockSpec((N_EXPERTS, 1), lambda i: (0, 0)),
        ],
        out_specs=[
            pl.BlockSpec((None, SUBLANES, tm), lambda i: (i, 0, 0)),
            pl.BlockSpec((tm, 2 * TOP_K), lambda i: (i, 0)),
            pl.BlockSpec((None, N_EXPERTS, LANES), lambda i: (i, 0, 0)),
            pl.BlockSpec((N_EXPERTS, 1), lambda i: (0, 0)),
        ],
        out_shape=[
            jax.ShapeDtypeStruct((t // tm, SUBLANES, tm), jnp.int32),
            jax.ShapeDtypeStruct((t, 2 * TOP_K), F32),
            jax.ShapeDtypeStruct((t // tm, N_EXPERTS, LANES), jnp.int32),
            jax.ShapeDtypeStruct((N_EXPERTS, 1), jnp.int32),
        ],
        scratch_shapes=[pltpu.VMEM((N_EXPERTS, 1), F32)],
        compiler_params=pltpu.CompilerParams(
            dimension_semantics=("arbitrary",), vmem_limit_bytes=VMEM_LIMIT),
        name="router",
    )(h_tiles, w_router, b_router)


def _start_run_copies(runs_ref, start_ref, tile_ref, rows_ref, sem, to_rows):
    offset = 0
    for e in range(N_EXPERTS):
        count = runs_ref[e, 0]
        first_row = start_ref[e] + runs_ref[e, 1]
        for bit in reversed(range(RUN_BITS)):
            size = (1 << bit) * N_CHUNKS
            done = (count >> (bit + 1)) << (bit + 1)

            @pl.when(((count >> bit) & 1) == 1)
            def _():
                in_tile = tile_ref.at[
                    pl.ds(pl.multiple_of((offset + done) * N_CHUNKS, N_CHUNKS), size)]
                in_rows = rows_ref.at[
                    pl.ds(pl.multiple_of((first_row + done) * N_CHUNKS, N_CHUNKS), size)]
                if to_rows:
                    pltpu.make_async_copy(in_tile, in_rows, sem).start()
                else:
                    pltpu.make_async_copy(in_rows, in_tile, sem).start()
        offset = offset + count


def _wait_run_copies(tile_ref, rows_ref, sem):
    pltpu.make_async_copy(tile_ref, rows_ref.at[pl.ds(0, tile_ref.shape[0])], sem).wait()


def _dispatch_kernel(pos_ref, runs_ref, start_ref, end_ref, cnt_ref, h_ref, xs_ref,
                     zero_ref, sorted_ref, sem, zsem):
    i = pl.program_id(0)
    tm = TM_ROUTE

    @pl.when(i == 0)
    def _():
        zero_ref[...] = jnp.zeros_like(zero_ref)
        for e in range(N_EXPERTS):
            @pl.when(cnt_ref[e, 0] > 0)
            def _():
                first = pl.multiple_of((end_ref[e] - TB_MOE) * N_CHUNKS, N_CHUNKS)
                cp = pltpu.make_async_copy(
                    zero_ref, xs_ref.at[pl.ds(first, TB_MOE * N_CHUNKS)], zsem)
                cp.start()
                cp.wait()

        def zero_block(blk, carry):
            first = pl.multiple_of(blk * (TB_MOE * N_CHUNKS), TB_MOE * N_CHUNKS)
            cp = pltpu.make_async_copy(
                zero_ref, xs_ref.at[pl.ds(first, TB_MOE * N_CHUNKS)], zsem)
            cp.start()
            cp.wait()
            return carry

        n_blocks = xs_ref.shape[0] // (TB_MOE * N_CHUNKS)
        lax.fori_loop(end_ref[N_EXPERTS - 1] // TB_MOE, n_blocks, zero_block, 0)

    p = lax.broadcasted_iota(jnp.int32, (TOP_K * tm, tm), 0)
    onehot = (p == pos_ref[0:1, :]) | (p == pos_ref[1:2, :])
    sorted_rows = _dot(jnp.where(onehot, 1.0, 0.0).astype(BF16),
                       _from_tiles(h_ref, tm).astype(BF16))

    slot = i % 2
    buf = sorted_ref.at[slot]

    @pl.when(i >= 2)
    def _():
        _wait_run_copies(buf, xs_ref, sem.at[slot])

    _to_tiles(buf, sorted_rows, TOP_K * tm)
    _start_run_copies(runs_ref, start_ref, buf, xs_ref, sem.at[slot], to_rows=True)

    @pl.when(i == pl.num_programs(0) - 1)
    def _():
        _wait_run_copies(buf, xs_ref, sem.at[slot])
        _wait_run_copies(sorted_ref.at[1 - slot], xs_ref, sem.at[1 - slot])


def _dispatch(pos, runs, pad_starts, pad_ends, counts, h_tiles, n_rows):
    nt = pos.shape[0]
    tm = TM_ROUTE
    smem_vec = pl.BlockSpec(memory_space=pltpu.SMEM)
    return pl.pallas_call(
        _dispatch_kernel,
        grid=(nt,),
        in_specs=[
            pl.BlockSpec((None, SUBLANES, tm), lambda i: (i, 0, 0)),
            pl.BlockSpec((None, N_EXPERTS, LANES), lambda i: (i, 0, 0), memory_space=pltpu.SMEM),
            smem_vec, smem_vec, smem_vec,
            pl.BlockSpec((tm * N_CHUNKS, LANES), lambda i: (i, 0))],
        out_specs=pl.BlockSpec(memory_space=pl.ANY),
        out_shape=jax.ShapeDtypeStruct((n_rows * N_CHUNKS, LANES), F32),
        scratch_shapes=[
            pltpu.VMEM((TB_MOE * N_CHUNKS, LANES), F32),
            pltpu.VMEM((2, TOP_K * tm * N_CHUNKS, LANES), F32),
            pltpu.SemaphoreType.DMA((2,)),
            pltpu.SemaphoreType.DMA(()),
        ],
        compiler_params=pltpu.CompilerParams(
            dimension_semantics=("arbitrary",), has_side_effects=True,
            vmem_limit_bytes=VMEM_LIMIT),
        name="dispatch",
    )(pos, runs, pad_starts, pad_ends, counts, h_tiles)


def _experts_kernel(be_ref, nused_ref, xs_ref, wg_ref, wu_ref, wd_ref, ys_ref,
                    wg_bf, wu_bf, wd_bf):
    i = pl.program_id(0)
    nused = nused_ref[0]

    @pl.when(i < nused)
    def _():
        prev = be_ref[jnp.maximum(i - 1, 0)]

        @pl.when((i == 0) | (be_ref[i] != prev))
        def _():
            wg_bf[...] = wg_ref[...].astype(BF16)
            wu_bf[...] = wu_ref[...].astype(BF16)
            wd_bf[...] = wd_ref[...].astype(BF16)

        x = _from_tiles(xs_ref, TB_MOE).astype(BF16)
        hg = _dot(x, wg_bf[...])
        hu = _dot(x, wu_bf[...])
        act = (hg * jax.nn.sigmoid(hg) * hu).astype(BF16)
        _to_tiles(ys_ref, _dot(act, wd_bf[...]), TB_MOE)

    @pl.when(i >= nused)
    def _():
        ys_ref[...] = jnp.zeros_like(ys_ref)


def _experts(block_expert, nused, xs, w_gate, w_up, w_down, layer):
    n_blocks = xs.shape[0] // (TB_MOE * N_CHUNKS)
    d, de = w_gate.shape[2], w_gate.shape[3]

    def x_map(i, be, nu):
        return (jnp.minimum(i, nu[0] - 1), 0)

    def w_map(i, be, nu):
        return (layer, be[jnp.minimum(i, nu[0] - 1)], 0, 0)

    return pl.pallas_call(
        _experts_kernel,
        grid_spec=pltpu.PrefetchScalarGridSpec(
            num_scalar_prefetch=2,
            grid=(n_blocks,),
            in_specs=[
                pl.BlockSpec((TB_MOE * N_CHUNKS, LANES), x_map),
                pl.BlockSpec((None, None, d, de), w_map),
                pl.BlockSpec((None, None, d, de), w_map),
                pl.BlockSpec((None, None, de, d), w_map),
            ],
            out_specs=pl.BlockSpec((TB_MOE * N_CHUNKS, LANES), lambda i, be, nu: (i, 0)),
            scratch_shapes=[
                pltpu.VMEM((d, de), BF16),
                pltpu.VMEM((d, de), BF16),
                pltpu.VMEM((de, d), BF16),
            ],
        ),
        out_shape=jax.ShapeDtypeStruct(xs.shape, F32),
        compiler_params=pltpu.CompilerParams(
            dimension_semantics=("arbitrary",), vmem_limit_bytes=VMEM_LIMIT),
        name="experts",
    )(block_expert, nused, xs, w_gate, w_up, w_down)


def _combine_kernel(runs_ref, next_runs_ref, start_ref, h_ref, gate_ref, g2_ref, b2_ref, ys_ref,
                    out_ref, sorted_ref, sem, *, chunk_major):
    tm = TM_ROUTE
    i = pl.program_id(0)
    slot = i % 2

    @pl.when(i == 0)
    def _():
        _start_run_copies(runs_ref, start_ref, sorted_ref.at[0], ys_ref, sem.at[0], to_rows=False)

    @pl.when(i + 1 < pl.num_programs(0))
    def _():
        _start_run_copies(next_runs_ref, start_ref, sorted_ref.at[1 - slot], ys_ref,
                          sem.at[1 - slot], to_rows=False)

    _wait_run_copies(sorted_ref.at[slot], ys_ref, sem.at[slot])

    gate = gate_ref[...]
    p = lax.broadcasted_iota(jnp.int32, (tm, TOP_K * tm), 1)
    onehot = jnp.concatenate([p == gate[:, 2:3].astype(jnp.int32),
                              p == gate[:, 3:4].astype(jnp.int32)], 0)
    picked = _dot(jnp.where(onehot, 1.0, 0.0).astype(BF16),
                  _from_tiles(sorted_ref.at[slot], TOP_K * tm).astype(BF16))
    ffn = picked[:tm] * gate[:, 0:1] + picked[tm:] * gate[:, 1:2]
    out = _layer_norm(ALPHA * _from_tiles(h_ref, tm) + ffn, g2_ref[...], b2_ref[...])
    if chunk_major:
        for c in range(N_CHUNKS):
            out_ref[c] = out[:, c * LANES:(c + 1) * LANES]
    else:
        out_ref[...] = out


def _combine(runs, pad_starts, h_tiles, gate, g2, b2, ys, batch, seq, chunk_major):
    tm = TM_ROUTE
    t = batch * seq
    nt = t // tm
    nts = seq // tm
    if chunk_major:
        out_spec = pl.BlockSpec((None, N_CHUNKS, tm, LANES), lambda i: (i // nts, 0, i % nts, 0))
        out_shape = jax.ShapeDtypeStruct((batch, N_CHUNKS, seq, LANES), F32)
    else:
        out_spec = pl.BlockSpec((None, tm, D_MODEL), lambda i: (i // nts, i % nts, 0))
        out_shape = jax.ShapeDtypeStruct((batch, seq, D_MODEL), F32)
    return pl.pallas_call(
        functools.partial(_combine_kernel, chunk_major=chunk_major),
        grid=(nt,),
        in_specs=[
            pl.BlockSpec((None, N_EXPERTS, LANES), lambda i: (i, 0, 0), memory_space=pltpu.SMEM),
            pl.BlockSpec((None, N_EXPERTS, LANES), lambda i: (jnp.minimum(i + 1, nt - 1), 0, 0),
                         memory_space=pltpu.SMEM),
            pl.BlockSpec(memory_space=pltpu.SMEM),
            pl.BlockSpec((tm * N_CHUNKS, LANES), lambda i: (i, 0)),
            pl.BlockSpec((tm, 2 * TOP_K), lambda i: (i, 0)),
            pl.BlockSpec((1, D_MODEL), lambda i: (0, 0)),
            pl.BlockSpec((1, D_MODEL), lambda i: (0, 0)),
            pl.BlockSpec(memory_space=pl.ANY),
        ],
        out_specs=out_spec,
        out_shape=out_shape,
        scratch_shapes=[
            pltpu.VMEM((2, TOP_K * tm * N_CHUNKS, LANES), F32),
            pltpu.SemaphoreType.DMA((2,)),
        ],
        compiler_params=pltpu.CompilerParams(
            dimension_semantics=("arbitrary",), vmem_limit_bytes=VMEM_LIMIT),
        name="combine",
    )(runs, runs, pad_starts, h_tiles, gate, g2, b2, ys)


def _moe(h_tiles, batch, seq, w_router, b_router, w_gate, w_up, w_down, g2, b2, layer,
         chunk_major):
    t = batch * seq
    pos, gate, runs, counts = _router(h_tiles, w_router, b_router)
    cnt = counts[:, 0]
    padded = (cnt + TB_MOE - 1) // TB_MOE * TB_MOE
    pad_ends = jnp.cumsum(padded).astype(jnp.int32)
    pad_starts = pad_ends - padded
    n_rows = t * TOP_K + N_EXPERTS * TB_MOE
    n_blocks = n_rows // TB_MOE
    block_start = jnp.arange(n_blocks, dtype=jnp.int32) * TB_MOE
    block_expert = jnp.minimum(
        jnp.sum(block_start[:, None] >= pad_ends[None, :], -1), N_EXPERTS - 1).astype(jnp.int32)
    nused = (pad_ends[-1:] // TB_MOE).astype(jnp.int32)
    xs = _dispatch(pos, runs, pad_starts, pad_ends, counts, h_tiles, n_rows)
    ys = _experts(block_expert, nused, xs, w_gate, w_up, w_down, layer)
    return _combine(runs, pad_starts, h_tiles, gate, g2, b2, ys, batch, seq, chunk_major)


def _residue_rows(x_ref, dilation):
    n = x_ref.shape[1]
    if dilation == 1:
        return _from_chunks(x_ref)
    per = n // dilation
    return jnp.concatenate(
        [jnp.concatenate([x_ref[c, pl.ds(r, per, stride=dilation), :] for r in range(dilation)], 0)
         for c in range(N_CHUNKS)], -1)


def _proj_b_kernel(x_ref, wqkv_ref, wqm_ref, qkv0_ref, qkv1_ref, qkv2_ref, qm_ref):
    ts = x_ref.shape[1]
    qscale = jnp.where(
        lax.broadcasted_iota(jnp.int32, (1, 3 * DIL_OUT_WIDTH), 1) < DIL_OUT_WIDTH, ATT_SCALE, 1.0)
    for g, (out_ref, (_, dilation)) in enumerate(zip((qkv0_ref, qkv1_ref, qkv2_ref), DIL_PAIRS)):
        xb = _residue_rows(x_ref, dilation).astype(BF16)
        qkv = (_dot(xb, wqkv_ref[g]) * qscale).astype(BF16)
        per = ts // dilation
        for r in range(dilation):
            out_ref[r] = qkv[r * per:(r + 1) * per]
    qm = _dot(_from_chunks(x_ref).astype(BF16), wqm_ref[...])
    qm_ref[...] = (qm * ATT_SCALE).astype(BF16)


def _proj_b(x_cm, wqkv, wqm):
    b, _, s, _ = x_cm.shape
    ts = TS_MIX
    width = 3 * DIL_OUT_WIDTH
    out_specs, out_shapes = [], []
    for _, dilation in DIL_PAIRS:
        out_specs.append(pl.BlockSpec((None, dilation, ts // dilation, width),
                                      lambda i, j: (i, 0, j, 0)))
        out_shapes.append(jax.ShapeDtypeStruct((b, dilation, s // dilation, width), BF16))
    out_specs.append(pl.BlockSpec((None, ts, MEM_WIDTH), lambda i, j: (i, j, 0)))
    out_shapes.append(jax.ShapeDtypeStruct((b, s, MEM_WIDTH), BF16))
    return pl.pallas_call(
        _proj_b_kernel,
        grid=(b, s // ts),
        in_specs=[
            pl.BlockSpec((None, N_CHUNKS, ts, LANES), lambda i, j: (i, 0, j, 0)),
            pl.BlockSpec(wqkv.shape, lambda i, j: (0, 0, 0)),
            pl.BlockSpec(wqm.shape, lambda i, j: (0, 0)),
        ],
        out_specs=out_specs,
        out_shape=out_shapes,
        compiler_params=pltpu.CompilerParams(vmem_limit_bytes=VMEM_LIMIT),
        name="proj_b",
    )(x_cm, wqkv, wqm)


def _dil_attn_kernel(qkv_ref, o_ref, lse_ref, *, dilation, slopes):
    nb = qkv_ref.shape[1] // CHUNK
    width = DIL_OUT_WIDTH
    heads = HEADS_PER_DIL
    row = lax.broadcasted_iota(jnp.int32, (heads * CHUNK, 2 * CHUNK), 0)
    jk = lax.broadcasted_iota(jnp.int32, (heads * CHUNK, 2 * CHUNK), 1)
    rel = CHUNK + row % CHUNK - jk
    slope = functools.reduce(
        lambda acc, h: jnp.where(row // CHUNK == h, slopes[h], acc), range(1, heads), slopes[0])
    bias = jnp.where((rel >= 0) & (rel <= CHUNK), -slope * (dilation * rel).astype(F32), -jnp.inf)
    before_start = jk < CHUNK
    head = lax.broadcasted_iota(jnp.int32, (1, width), 1) // HEAD_DIM

    def one_block(idx):
        r = idx // nb
        jb = idx % nb
        cur = pl.ds(pl.multiple_of(jb * CHUNK, CHUNK), CHUNK)
        prev = pl.ds(pl.multiple_of(jnp.maximum(jb - 1, 0) * CHUNK, CHUNK), CHUNK)
        q = qkv_ref[r, cur, 0:width]
        kband = jnp.concatenate([qkv_ref[r, prev, width:2 * width],
                                 qkv_ref[r, cur, width:2 * width]], 0)
        vband = jnp.concatenate([qkv_ref[r, prev, 2 * width:3 * width],
                                 qkv_ref[r, cur, 2 * width:3 * width]], 0)
        stacked = jnp.concatenate(
            [jnp.where(head == h, q, jnp.zeros_like(q)) for h in range(heads)], 0)
        s = _dot_nt(stacked, kband) + bias
        s = jnp.where(before_start & (jb == 0), -jnp.inf, s)
        m = jnp.max(s, -1, keepdims=True)
        p = jnp.exp(s - m)
        l = jnp.sum(p, -1, keepdims=True)
        o = _dot((p * (1.0 / l)).astype(BF16), vband)
        row_lse = m + jnp.log(l)
        out = jnp.zeros((CHUNK, width), F32)
        lse = jnp.zeros((CHUNK, width), F32)
        for h in range(heads):
            rows = slice(h * CHUNK, (h + 1) * CHUNK)
            out = jnp.where(head == h, o[rows], out)
            lse = jnp.where(head == h, row_lse[rows], lse)
        o_ref[r, cur, :] = out
        lse_ref[r, cur, :] = lse

    def pair(i, carry):
        one_block(2 * i)
        one_block(2 * i + 1)
        return carry

    lax.fori_loop(0, dilation * nb // 2, pair, 0)


def _dil_attn(qkv, group):
    b, dilation, length, _ = qkv.shape
    slopes = tuple(2.0 ** (-8.0 * (group * HEADS_PER_DIL + h + 1) / N_DIL_HEADS)
                   for h in range(HEADS_PER_DIL))
    out_spec = pl.BlockSpec((None, dilation, length, DIL_OUT_WIDTH), lambda i: (i, 0, 0, 0))
    out_shape = jax.ShapeDtypeStruct((b, dilation, length, DIL_OUT_WIDTH), F32)
    return pl.pallas_call(
        functools.partial(_dil_attn_kernel, dilation=dilation, slopes=slopes),
        grid=(b,),
        in_specs=[pl.BlockSpec((None, dilation, length, 3 * DIL_OUT_WIDTH),
                               lambda i: (i, 0, 0, 0))],
        out_specs=[out_spec, out_spec],
        out_shape=[out_shape, out_shape],
        compiler_params=pltpu.CompilerParams(vmem_limit_bytes=VMEM_LIMIT),
        name=f"dil_attn_{group}",
    )(qkv)


def _token_order(src_ref, scratch, dilation):
    if dilation == 1:
        return src_ref[0]
    per = src_ref.shape[1]
    halves = DIL_OUT_WIDTH // LANES
    for r in range(dilation):
        v = src_ref[r]
        for c in range(halves):
            scratch[c, pl.ds(r, per, stride=dilation), :] = v[:, c * LANES:(c + 1) * LANES]
    return jnp.concatenate([scratch[c] for c in range(halves)], -1)


def _mixer_b_kernel(x_ref, qm_ref, o0_ref, l0_ref, o1_ref, l1_ref, o2_ref, l2_ref,
                    mk_ref, mv_ref, wout_ref, g1_ref, b1_ref, h_ref,
                    so1, sl1, so2, sl2):
    ts = x_ref.shape[1]
    outs = [o0_ref[0], _token_order(o1_ref, so1, DIL_PAIRS[1][1]),
            _token_order(o2_ref, so2, DIL_PAIRS[2][1])]
    lses = [l0_ref[0], _token_order(l1_ref, sl1, DIL_PAIRS[1][1]),
            _token_order(l2_ref, sl2, DIL_PAIRS[2][1])]
    top = functools.reduce(jnp.maximum, lses)
    es = [jnp.exp(l - top) for l in lses]
    inv = 1.0 / functools.reduce(jnp.add, es)
    mix = functools.reduce(jnp.add, [e * inv * o for e, o in zip(es, outs)])
    mo = _mem_attention(qm_ref[...], mk_ref[...], mv_ref[...])
    att = _dot(jnp.concatenate([mix, mo], -1).astype(BF16), wout_ref[...])
    h = _layer_norm(ALPHA * _from_chunks(x_ref) + att, g1_ref[...], b1_ref[...])
    _to_tiles(h_ref, h, ts)


def _mixer_b(x_cm, qm, attn, mk, mv, w_out, g1, b1):
    b, _, s, _ = x_cm.shape
    ts = TS_MIX
    nts = s // ts
    m = mk.shape[0] // b
    in_specs = [
        pl.BlockSpec((None, N_CHUNKS, ts, LANES), lambda i, j: (i, 0, j, 0)),
        pl.BlockSpec((None, ts, MEM_WIDTH), lambda i, j: (i, j, 0)),
    ]
    args = [x_cm, qm]
    for (o, lse), (_, dilation) in zip(attn, DIL_PAIRS):
        spec = pl.BlockSpec((None, dilation, ts // dilation, DIL_OUT_WIDTH),
                            lambda i, j: (i, 0, j, 0))
        in_specs += [spec, spec]
        args += [o, lse]
    in_specs += [
        pl.BlockSpec((m, MEM_WIDTH), lambda i, j: (i, 0)),
        pl.BlockSpec((m, MEM_WIDTH), lambda i, j: (i, 0)),
        pl.BlockSpec(w_out.shape, lambda i, j: (0, 0)),
        pl.BlockSpec((1, D_MODEL), lambda i, j: (0, 0)),
        pl.BlockSpec((1, D_MODEL), lambda i, j: (0, 0)),
    ]
    args += [mk, mv, w_out, g1, b1]
    halves = DIL_OUT_WIDTH // LANES
    return pl.pallas_call(
        _mixer_b_kernel,
        grid=(b, nts),
        in_specs=in_specs,
        out_specs=pl.BlockSpec((ts * N_CHUNKS, LANES), lambda i, j: (i * nts + j, 0)),
        out_shape=jax.ShapeDtypeStruct((b * s * N_CHUNKS, LANES), F32),
        scratch_shapes=[pltpu.VMEM((halves, ts, LANES), F32)] * 4,
        compiler_params=pltpu.CompilerParams(vmem_limit_bytes=VMEM_LIMIT),
        name="mixer_b",
    )(*args)


def kernel(x, mem, w_in_a, w_out_a, sg_ln_g, sg_ln_b, sg_w, sg_b, w_in_b, w_out_b, w_k_shared,
           w_v_shared, w_mem_k, w_mem_v, ln1_g, ln1_b, ln2_g, ln2_b, w_router, b_router,
           w_gate, w_up, w_down):
    batch, seq, d = x.shape
    mk, mv = _memkv(mem.reshape(-1, d), w_mem_k, w_mem_v)
    row = lambda v: v.reshape(1, -1)

    sg_bias = jnp.repeat(sg_b[0].T, HEAD_DIM, axis=1)
    h = _mixer_a(x, w_in_a[0].astype(BF16), w_out_a[0].astype(BF16), row(sg_ln_g[0]),
                 row(sg_ln_b[0]), sg_w[0], sg_bias, mk[0], mv[0], row(ln1_g[0]), row(ln1_b[0]))
    w_router_pad = jnp.pad(w_router, ((0, 0), (0, LANES - N_EXPERTS)))
    b_router_col = b_router.reshape(N_EXPERTS, 1)
    x_cm = _moe(h, batch, seq, w_router_pad, b_router_col, w_gate, w_up, w_down,
                row(ln2_g[0]), row(ln2_b[0]), layer=0, chunk_major=True)

    wqkv = jnp.stack([
        jnp.concatenate([w[:, g * DIL_OUT_WIDTH:(g + 1) * DIL_OUT_WIDTH]
                         for w in (w_in_b[0], w_k_shared, w_v_shared)], -1)
        for g in range(len(DIL_PAIRS))]).astype(BF16)
    wqm = w_in_b[0][:, DIL_Q_WIDTH:].astype(BF16)
    qkv0, qkv1, qkv2, qm = _proj_b(x_cm, wqkv, wqm)
    attn = [_dil_attn(qkv, g) for g, qkv in enumerate((qkv0, qkv1, qkv2))]
    h = _mixer_b(x_cm, qm, attn, mk[1], mv[1], w_out_b[0].astype(BF16),
                 row(ln1_g[1]), row(ln1_b[1]))
    return _moe(h, batch, seq, w_router_pad, b_router_col, w_gate, w_up, w_down,
                row(ln2_g[1]), row(ln2_b[1]), layer=1, chunk_major=False)
```

```python
import functools
import math

import jax
import jax.numpy as jnp
from jax import lax
from jax.experimental import pallas as pl
from jax.experimental.pallas import tpu as pltpu

D_MODEL = 1024
HEAD_DIM = 64
CHUNK = 128
N_SG = 12
SG_WIDTH = N_SG * HEAD_DIM
DIL_PAIRS = ((128, 1), (512, 4), (2048, 16))
HEADS_PER_DIL = 4
N_DIL_HEADS = HEADS_PER_DIL * len(DIL_PAIRS)
DIL_Q_WIDTH = N_DIL_HEADS * HEAD_DIM
DIL_OUT_WIDTH = HEADS_PER_DIL * HEAD_DIM
MEM_HEADS = 4
MEM_WIDTH = MEM_HEADS * HEAD_DIM
N_EXPERTS = 16
N_EXPERT_GROUPS = 4
EXPERTS_PER_GROUP = N_EXPERTS // N_EXPERT_GROUPS
TOP_K = 2
DEPTH = 2
ALPHA = (2 * DEPTH) ** 0.25
LN_EPS = 1e-5
ATT_SCALE = 1.0 / math.sqrt(HEAD_DIM)

LANES = 128
SUBLANES = 8
N_CHUNKS = D_MODEL // LANES

TS_MIX = 512
MXU_DEPTH = 256
TM_ROUTE = MXU_DEPTH
RUN_BITS = TM_ROUTE.bit_length()
TB_MOE = 512
DIL_INTERLEAVE = 4
VMEM_LIMIT = 56 * 1024 * 1024

F32 = jnp.float32
BF16 = jnp.bfloat16


def _dot(a, b):
    return jnp.dot(a, b, preferred_element_type=F32)


def _dot_nt(a, b):
    return lax.dot_general(a, b, (((1,), (1,)), ((), ())), preferred_element_type=F32)


def _layer_norm(x, g, b):
    mu = jnp.mean(x, -1, keepdims=True)
    xc = x - mu
    var = jnp.mean(xc * xc, -1, keepdims=True)
    return xc * lax.rsqrt(var + LN_EPS) * g + b


def _gelu(x):
    return 0.5 * x * (1.0 + lax.erf(x * (1.0 / math.sqrt(2.0))))


def _from_tiles(ref, n):
    return jnp.concatenate(
        [ref[pl.ds(c, n, stride=N_CHUNKS), :] for c in range(N_CHUNKS)], -1)


def _to_tiles(ref, val, n):
    for c in range(N_CHUNKS):
        ref[pl.ds(c, n, stride=N_CHUNKS), :] = val[:, c * LANES:(c + 1) * LANES]


def _from_chunks(ref):
    return jnp.concatenate([ref[c] for c in range(N_CHUNKS)], -1)


def _mem_attention(q, mk, mv):
    n = q.shape[0]
    head = lax.broadcasted_iota(jnp.int32, (1, MEM_WIDTH), 1) // HEAD_DIM
    stacked = jnp.concatenate(
        [jnp.where(head == h, q, jnp.zeros_like(q)) for h in range(MEM_HEADS)], 0)
    s = _dot_nt(stacked, mk)
    m = jnp.max(s, -1, keepdims=True)
    p = jnp.exp(s - m)
    l = jnp.sum(p, -1, keepdims=True)
    o = _dot((p * (1.0 / l)).astype(BF16), mv)
    out = jnp.zeros((n, MEM_WIDTH), F32)
    for h in range(MEM_HEADS):
        out = jnp.where(head == h, o[h * n:(h + 1) * n], out)
    return out


def _memkv_kernel(mem_ref, wk_ref, wv_ref, mk_ref, mv_ref):
    m = mem_ref[...].astype(BF16)
    mk_ref[...] = _dot(m, wk_ref[...].astype(BF16)).astype(BF16)
    mv_ref[...] = _dot(m, wv_ref[...].astype(BF16)).astype(BF16)


def _memkv(mem2d, w_mem_k, w_mem_v):
    n = mem2d.shape[0]
    return pl.pallas_call(
        _memkv_kernel,
        grid=(DEPTH,),
        in_specs=[
            pl.BlockSpec((n, D_MODEL), lambda l: (0, 0)),
            pl.BlockSpec((None, D_MODEL, MEM_WIDTH), lambda l: (l, 0, 0)),
            pl.BlockSpec((None, D_MODEL, MEM_WIDTH), lambda l: (l, 0, 0)),
        ],
        out_specs=[
            pl.BlockSpec((None, n, MEM_WIDTH), lambda l: (l, 0, 0)),
            pl.BlockSpec((None, n, MEM_WIDTH), lambda l: (l, 0, 0)),
        ],
        out_shape=[jax.ShapeDtypeStruct((DEPTH, n, MEM_WIDTH), BF16)] * 2,
        compiler_params=pltpu.CompilerParams(vmem_limit_bytes=VMEM_LIMIT),
        name="memkv",
    )(mem2d, w_mem_k, w_mem_v)


def _mixer_a_kernel(x_ref, win_ref, wout_ref, sgg_ref, sgb_ref, sgw_ref, sgbias_ref,
                    mk_ref, mv_ref, g1_ref, b1_ref, h_ref):
    ts = x_ref.shape[0]
    x = x_ref[...]
    proj = _dot(x.astype(BF16), win_ref[...])
    u = _gelu(proj[:, :SG_WIDTH])
    gv = _gelu(proj[:, SG_WIDTH:2 * SG_WIDTH])
    gv = _layer_norm(gv, sgg_ref[...], sgb_ref[...]).astype(BF16)

    row = lax.broadcasted_iota(jnp.int32, (CHUNK, CHUNK), 0)
    col = lax.broadcasted_iota(jnp.int32, (CHUNK, CHUNK), 1)
    ws = [jnp.where(row >= col, sgw_ref[g], 0.0).astype(BF16) for g in range(N_SG)]
    n_chunks = ts // CHUNK
    low_half = (lax.broadcasted_iota(jnp.int32, (CHUNK, n_chunks * LANES), 1) % LANES) < HEAD_DIM
    gated = []
    for j in range(N_SG // 2):
        slabs = jnp.concatenate(
            [gv[c * CHUNK:(c + 1) * CHUNK, j * LANES:(j + 1) * LANES] for c in range(n_chunks)], -1)
        both = _dot(jnp.concatenate([ws[2 * j], ws[2 * j + 1]], 0), slabs)
        gated.append(jnp.where(low_half, both[:CHUNK], both[CHUNK:]))
    rows = [jnp.concatenate([g[:, c * LANES:(c + 1) * LANES] for g in gated], -1)
            + sgbias_ref[...] for c in range(n_chunks)]
    mix = u * jnp.concatenate(rows, 0)

    qm = (proj[:, 2 * SG_WIDTH:] * ATT_SCALE).astype(BF16)
    mo = _mem_attention(qm, mk_ref[...], mv_ref[...])
    att = _dot(jnp.concatenate([mix, mo], -1).astype(BF16), wout_ref[...])
    h = _layer_norm(ALPHA * x + att, g1_ref[...], b1_ref[...])
    _to_tiles(h_ref, h, ts)


def _mixer_a(x, w_in, w_out, sg_g, sg_b, sg_w, sg_bias, mk, mv, g1, b1):
    b, s, d = x.shape
    nts = s // TS_MIX
    m = mk.shape[0] // b
    full = lambda *shape: pl.BlockSpec(shape, lambda i, j: (0,) * len(shape))
    return pl.pallas_call(
        _mixer_a_kernel,
        grid=(b, nts),
        in_specs=[
            pl.BlockSpec((None, TS_MIX, d), lambda i, j: (i, j, 0)),
            full(d, 2 * SG_WIDTH + MEM_WIDTH),
            full(SG_WIDTH + MEM_WIDTH, d),
            full(1, SG_WIDTH),
            full(1, SG_WIDTH),
            full(N_SG, CHUNK, CHUNK),
            full(CHUNK, SG_WIDTH),
            pl.BlockSpec((m, MEM_WIDTH), lambda i, j: (i, 0)),
            pl.BlockSpec((m, MEM_WIDTH), lambda i, j: (i, 0)),
            full(1, d),
            full(1, d),
        ],
        out_specs=pl.BlockSpec((TS_MIX * N_CHUNKS, LANES), lambda i, j: (i * nts + j, 0)),
        out_shape=jax.ShapeDtypeStruct((b * s * N_CHUNKS, LANES), F32),
        compiler_params=pltpu.CompilerParams(vmem_limit_bytes=VMEM_LIMIT),
        name="mixer_a",
    )(x, w_in, w_out, sg_g, sg_b, sg_w, sg_bias, mk, mv, g1, b1)


def _router_kernel(h_ref, wr_ref, br_ref, pos_ref, gate_ref, runs_ref, counts_ref, base_ref):
    tm = TM_ROUTE

    @pl.when(pl.program_id(0) == 0)
    def _():
        base_ref[...] = jnp.zeros_like(base_ref)

    h = _from_tiles(h_ref, tm)
    w = wr_ref[...]
    h_hi = h.astype(BF16)
    h_lo = (h - h_hi.astype(F32)).astype(BF16)
    w_hi = w.astype(BF16)
    w_lo = (w - w_hi.astype(F32)).astype(BF16)
    logits = _dot(h_hi, w_hi) + (_dot(h_hi, w_lo) + _dot(h_lo, w_hi))
    lt = logits.T[:N_EXPERTS]

    ex = jnp.exp(lt - jnp.max(lt, 0, keepdims=True))
    probs = ex / jnp.sum(ex, 0, keepdims=True)
    sel = probs + br_ref[...]
    eid = lax.broadcasted_iota(jnp.int32, (N_EXPERTS, tm), 0)
    group = eid // EXPERTS_PER_GROUP

    def top2(mask):
        v = jnp.where(mask, sel, -jnp.inf)
        m1 = jnp.max(v, 0, keepdims=True)
        i1 = jnp.min(jnp.where(v == m1, eid, N_EXPERTS), 0, keepdims=True)
        v2 = jnp.where(eid == i1, -jnp.inf, v)
        m2 = jnp.max(v2, 0, keepdims=True)
        i2 = jnp.min(jnp.where(v2 == m2, eid, N_EXPERTS), 0, keepdims=True)
        return m1, i1, m2, i2

    scores = []
    for g in range(N_EXPERT_GROUPS):
        m1, _, m2, _ = top2(group == g)
        scores.append(m1 + m2)
    best = functools.reduce(jnp.maximum, scores)
    g_idx = jnp.full((1, tm), N_EXPERT_GROUPS - 1, jnp.int32)
    for g in reversed(range(N_EXPERT_GROUPS - 1)):
        g_idx = jnp.where(scores[g] == best, g, g_idx)
    _, e0, _, e1 = top2(group == g_idx)

    hot0 = eid == e0
    hot1 = eid == e1
    p0 = jnp.sum(jnp.where(hot0, probs, 0.0), 0, keepdims=True)
    p1 = jnp.sum(jnp.where(hot1, probs, 0.0), 0, keepdims=True)
    psum = p0 + p1

    hot = jnp.where(hot0 | hot1, 1.0, 0.0)
    earlier = (lax.broadcasted_iota(jnp.int32, (tm, tm), 0)
               < lax.broadcasted_iota(jnp.int32, (tm, tm), 1))
    hot_bf = hot.astype(BF16)
    before = _dot(hot_bf, jnp.where(earlier, 1.0, 0.0).astype(BF16))
    below = (lax.broadcasted_iota(jnp.int32, (N_EXPERTS, N_EXPERTS), 1)
             < lax.broadcasted_iota(jnp.int32, (N_EXPERTS, N_EXPERTS), 0))
    run_start = jnp.sum(_dot(jnp.where(below, 1.0, 0.0).astype(BF16), hot_bf), 1, keepdims=True)
    local = run_start + before
    q0 = jnp.sum(jnp.where(hot0, local, 0.0), 0, keepdims=True)
    q1 = jnp.sum(jnp.where(hot1, local, 0.0), 0, keepdims=True)

    row = lax.broadcasted_iota(jnp.int32, (SUBLANES, tm), 0)
    pos_ref[...] = jnp.where(row == 0, q0, jnp.where(row == 1, q1, 0.0)).astype(jnp.int32)
    grow = lax.broadcasted_iota(jnp.int32, (LANES, tm), 0)
    cols = jnp.where(grow == 0, p0 / psum, jnp.where(grow == 1, p1 / psum, jnp.where(
        grow == 2, q0, jnp.where(grow == 3, q1, 0.0))))
    gate_ref[...] = cols.T[:, :2 * TOP_K]
    count = jnp.sum(hot, 1, keepdims=True)
    lane = lax.broadcasted_iota(jnp.int32, (N_EXPERTS, LANES), 1)
    runs_ref[...] = jnp.where(lane == 0, count, jnp.where(
        lane == 1, base_ref[...], 0.0)).astype(jnp.int32)
    total = base_ref[...] + count
    base_ref[...] = total
    counts_ref[...] = total.astype(jnp.int32)


def _router(h_tiles, w_router, b_router):
    t = h_tiles.shape[0] // N_CHUNKS
    tm = TM_ROUTE
    return pl.pallas_call(
        _router_kernel,
        grid=(t // tm,),
        in_specs=[
            pl.BlockSpec((tm * N_CHUNKS, LANES), lambda i: (i, 0)),
            pl.BlockSpec((D_MODEL, LANES), lambda i: (0, 0)),
            pl.BlockSpec((N_EXPERTS, 1), lambda i: (0, 0)),
        ],
        out_specs=[
            pl.BlockSpec((None, SUBLANES, tm), lambda i: (i, 0, 0)),
            pl.BlockSpec((tm, 2 * TOP_K), lambda i: (i, 0)),
            pl.BlockSpec((None, N_EXPERTS, LANES), lambda i: (i, 0, 0)),
            pl.BlockSpec((N_EXPERTS, 1), lambda i: (0, 0)),
        ],
        out_shape=[
            jax.ShapeDtypeStruct((t // tm, SUBLANES, tm), jnp.int32),
            jax.ShapeDtypeStruct((t, 2 * TOP_K), F32),
            jax.ShapeDtypeStruct((t // tm, N_EXPERTS, LANES), jnp.int32),
            jax.ShapeDtypeStruct((N_EXPERTS, 1), jnp.int32),
        ],
        scratch_shapes=[pltpu.VMEM((N_EXPERTS, 1), F32)],
        compiler_params=pltpu.CompilerParams(
            dimension_semantics=("arbitrary",), vmem_limit_bytes=VMEM_LIMIT),
        name="router",
    )(h_tiles, w_router, b_router)


def _start_run_copies(runs_ref, start_ref, tile_ref, rows_ref, sem, to_rows):
    offset = 0
    for e in range(N_EXPERTS):
        count = runs_ref[e, 0]
        first_row = start_ref[e] + runs_ref[e, 1]
        for bit in reversed(range(RUN_BITS)):
            size = (1 << bit) * N_CHUNKS
            done = (count >> (bit + 1)) << (bit + 1)

            @pl.when(((count >> bit) & 1) == 1)
            def _():
                in_tile = tile_ref.at[
                    pl.ds(pl.multiple_of((offset + done) * N_CHUNKS, N_CHUNKS), size)]
                in_rows = rows_ref.at[
                    pl.ds(pl.multiple_of((first_row + done) * N_CHUNKS, N_CHUNKS), size)]
                if to_rows:
                    pltpu.make_async_copy(in_tile, in_rows, sem).start()
                else:
                    pltpu.make_async_copy(in_rows, in_tile, sem).start()
        offset = offset + count


def _wait_run_copies(tile_ref, rows_ref, sem):
    pltpu.make_async_copy(tile_ref, rows_ref.at[pl.ds(0, tile_ref.shape[0])], sem).wait()


def _dispatch_kernel(pos_ref, runs_ref, start_ref, end_ref, cnt_ref, h_ref, xs_ref,
                     zero_ref, sorted_ref, sem, zsem):
    i = pl.program_id(0)
    tm = TM_ROUTE

    @pl.when(i == 0)
    def _():
        zero_ref[...] = jnp.zeros_like(zero_ref)
        for e in range(N_EXPERTS):
            @pl.when(cnt_ref[e, 0] > 0)
            def _():
                first = pl.multiple_of((end_ref[e] - TB_MOE) * N_CHUNKS, N_CHUNKS)
                cp = pltpu.make_async_copy(
                    zero_ref, xs_ref.at[pl.ds(first, TB_MOE * N_CHUNKS)], zsem)
                cp.start()
                cp.wait()

        def zero_block(blk, carry):
            first = pl.multiple_of(blk * (TB_MOE * N_CHUNKS), TB_MOE * N_CHUNKS)
            cp = pltpu.make_async_copy(
                zero_ref, xs_ref.at[pl.ds(first, TB_MOE * N_CHUNKS)], zsem)
            cp.start()
            cp.wait()
            return carry

        n_blocks = xs_ref.shape[0] // (TB_MOE * N_CHUNKS)
        lax.fori_loop(end_ref[N_EXPERTS - 1] // TB_MOE, n_blocks, zero_block, 0)

    p = lax.broadcasted_iota(jnp.int32, (TOP_K * tm, tm), 0)
    onehot = (p == pos_ref[0:1, :]) | (p == pos_ref[1:2, :])
    sorted_rows = _dot(jnp.where(onehot, 1.0, 0.0).astype(BF16),
                       _from_tiles(h_ref, tm).astype(BF16))

    slot = i % 2
    buf = sorted_ref.at[slot]

    @pl.when(i >= 2)
    def _():
        _wait_run_copies(buf, xs_ref, sem.at[slot])

    _to_tiles(buf, sorted_rows, TOP_K * tm)
    _start_run_copies(runs_ref, start_ref, buf, xs_ref, sem.at[slot], to_rows=True)

    @pl.when(i == pl.num_programs(0) - 1)
    def _():
        _wait_run_copies(buf, xs_ref, sem.at[slot])
        _wait_run_copies(sorted_ref.at[1 - slot], xs_ref, sem.at[1 - slot])


def _dispatch(pos, runs, pad_starts, pad_ends, counts, h_tiles, n_rows):
    nt = pos.shape[0]
    tm = TM_ROUTE
    smem_vec = pl.BlockSpec(memory_space=pltpu.SMEM)
    return pl.pallas_call(
        _dispatch_kernel,
        grid=(nt,),
        in_specs=[
            pl.BlockSpec((None, SUBLANES, tm), lambda i: (i, 0, 0)),
            pl.BlockSpec((None, N_EXPERTS, LANES), lambda i: (i, 0, 0), memory_space=pltpu.SMEM),
            smem_vec, smem_vec, smem_vec,
            pl.BlockSpec((tm * N_CHUNKS, LANES), lambda i: (i, 0))],
        out_specs=pl.BlockSpec(memory_space=pl.ANY),
        out_shape=jax.ShapeDtypeStruct((n_rows * N_CHUNKS, LANES), F32),
        scratch_shapes=[
            pltpu.VMEM((TB_MOE * N_CHUNKS, LANES), F32),
            pltpu.VMEM((2, TOP_K * tm * N_CHUNKS, LANES), F32),
            pltpu.SemaphoreType.DMA((2,)),
            pltpu.SemaphoreType.DMA(()),
        ],
        compiler_params=pltpu.CompilerParams(
            dimension_semantics=("arbitrary",), has_side_effects=True,
            vmem_limit_bytes=VMEM_LIMIT),
        name="dispatch",
    )(pos, runs, pad_starts, pad_ends, counts, h_tiles)


def _experts_kernel(be_ref, nused_ref, xs_ref, wg_ref, wu_ref, wd_ref, ys_ref,
                    wg_bf, wu_bf, wd_bf):
    i = pl.program_id(0)
    nused = nused_ref[0]

    @pl.when(i < nused)
    def _():
        prev = be_ref[jnp.maximum(i - 1, 0)]

        @pl.when((i == 0) | (be_ref[i] != prev))
        def _():
            wg_bf[...] = wg_ref[...].astype(BF16)
            wu_bf[...] = wu_ref[...].astype(BF16)
            wd_bf[...] = wd_ref[...].astype(BF16)

        x = _from_tiles(xs_ref, TB_MOE).astype(BF16)
        hg = _dot(x, wg_bf[...])
        hu = _dot(x, wu_bf[...])
        act = (hg * jax.nn.sigmoid(hg) * hu).astype(BF16)
        _to_tiles(ys_ref, _dot(act, wd_bf[...]), TB_MOE)

    @pl.when(i >= nused)
    def _():
        ys_ref[...] = jnp.zeros_like(ys_ref)


def _experts(block_expert, nused, xs, w_gate, w_up, w_down, layer):
    n_blocks = xs.shape[0] // (TB_MOE * N_CHUNKS)
    d, de = w_gate.shape[2], w_gate.shape[3]

    def x_map(i, be, nu):
        return (jnp.minimum(i, nu[0] - 1), 0)

    def w_map(i, be, nu):
        return (layer, be[jnp.minimum(i, nu[0] - 1)], 0, 0)

    return pl.pallas_call(
        _experts_kernel,
        grid_spec=pltpu.PrefetchScalarGridSpec(
            num_scalar_prefetch=2,
            grid=(n_blocks,),
            in_specs=[
                pl.BlockSpec((TB_MOE * N_CHUNKS, LANES), x_map),
                pl.BlockSpec((None, None, d, de), w_map),
                pl.BlockSpec((None, None, d, de), w_map),
                pl.BlockSpec((None, None, de, d), w_map),
            ],
            out_specs=pl.BlockSpec((TB_MOE * N_CHUNKS, LANES), lambda i, be, nu: (i, 0)),
            scratch_shapes=[
                pltpu.VMEM((d, de), BF16),
                pltpu.VMEM((d, de), BF16),
                pltpu.VMEM((de, d), BF16),
            ],
        ),
        out_shape=jax.ShapeDtypeStruct(xs.shape, F32),
        compiler_params=pltpu.CompilerParams(
            dimension_semantics=("arbitrary",), vmem_limit_bytes=VMEM_LIMIT),
        name="experts",
    )(block_expert, nused, xs, w_gate, w_up, w_down)


def _combine_kernel(runs_ref, next_runs_ref, start_ref, h_ref, gate_ref, g2_ref, b2_ref, ys_ref,
                    out_ref, sorted_ref, sem, *, chunk_major):
    tm = TM_ROUTE
    i = pl.program_id(0)
    slot = i % 2

    @pl.when(i == 0)
    def _():
        _start_run_copies(runs_ref, start_ref, sorted_ref.at[0], ys_ref, sem.at[0], to_rows=False)

    @pl.when(i + 1 < pl.num_programs(0))
    def _():
        _start_run_copies(next_runs_ref, start_ref, sorted_ref.at[1 - slot], ys_ref,
                          sem.at[1 - slot], to_rows=False)

    _wait_run_copies(sorted_ref.at[slot], ys_ref, sem.at[slot])

    gate = gate_ref[...]
    p = lax.broadcasted_iota(jnp.int32, (tm, TOP_K * tm), 1)
    onehot = jnp.concatenate([p == gate[:, 2:3].astype(jnp.int32),
                              p == gate[:, 3:4].astype(jnp.int32)], 0)
    picked = _dot(jnp.where(onehot, 1.0, 0.0).astype(BF16),
                  _from_tiles(sorted_ref.at[slot], TOP_K * tm).astype(BF16))
    ffn = picked[:tm] * gate[:, 0:1] + picked[tm:] * gate[:, 1:2]
    out = _layer_norm(ALPHA * _from_tiles(h_ref, tm) + ffn, g2_ref[...], b2_ref[...])
    if chunk_major:
        for c in range(N_CHUNKS):
            out_ref[c] = out[:, c * LANES:(c + 1) * LANES]
    else:
        out_ref[...] = out


def _combine(runs, pad_starts, h_tiles, gate, g2, b2, ys, batch, seq, chunk_major):
    tm = TM_ROUTE
    t = batch * seq
    nt = t // tm
    nts = seq // tm
    if chunk_major:
        out_spec = pl.BlockSpec((None, N_CHUNKS, tm, LANES), lambda i: (i // nts, 0, i % nts, 0))
        out_shape = jax.ShapeDtypeStruct((batch, N_CHUNKS, seq, LANES), F32)
    else:
        out_spec = pl.BlockSpec((None, tm, D_MODEL), lambda i: (i // nts, i % nts, 0))
        out_shape = jax.ShapeDtypeStruct((batch, seq, D_MODEL), F32)
    return pl.pallas_call(
        functools.partial(_combine_kernel, chunk_major=chunk_major),
        grid=(nt,),
        in_specs=[
            pl.BlockSpec((None, N_EXPERTS, LANES), lambda i: (i, 0, 0), memory_space=pltpu.SMEM),
            pl.BlockSpec((None, N_EXPERTS, LANES), lambda i: (jnp.minimum(i + 1, nt - 1), 0, 0),
                         memory_space=pltpu.SMEM),
            pl.BlockSpec(memory_space=pltpu.SMEM),
            pl.BlockSpec((tm * N_CHUNKS, LANES), lambda i: (i, 0)),
            pl.BlockSpec((tm, 2 * TOP_K), lambda i: (i, 0)),
            pl.BlockSpec((1, D_MODEL), lambda i: (0, 0)),
            pl.BlockSpec((1, D_MODEL), lambda i: (0, 0)),
            pl.BlockSpec(memory_space=pl.ANY),
        ],
        out_specs=out_spec,
        out_shape=out_shape,
        scratch_shapes=[
            pltpu.VMEM((2, TOP_K * tm * N_CHUNKS, LANES), F32),
            pltpu.SemaphoreType.DMA((2,)),
        ],
        compiler_params=pltpu.CompilerParams(
            dimension_semantics=("arbitrary",), vmem_limit_bytes=VMEM_LIMIT),
        name="combine",
    )(runs, runs, pad_starts, h_tiles, gate, g2, b2, ys)


def _moe(h_tiles, batch, seq, w_router, b_router, w_gate, w_up, w_down, g2, b2, layer,
         chunk_major):
    t = batch * seq
    pos, gate, runs, counts = _router(h_tiles, w_router, b_router)
    cnt = counts[:, 0]
    padded = (cnt + TB_MOE - 1) // TB_MOE * TB_MOE
    pad_ends = jnp.cumsum(padded).astype(jnp.int32)
    pad_starts = pad_ends - padded
    n_rows = t * TOP_K + N_EXPERTS * TB_MOE
    n_blocks = n_rows // TB_MOE
    block_start = jnp.arange(n_blocks, dtype=jnp.int32) * TB_MOE
    block_expert = jnp.minimum(
        jnp.sum(block_start[:, None] >= pad_ends[None, :], -1), N_EXPERTS - 1).astype(jnp.int32)
    nused = (pad_ends[-1:] // TB_MOE).astype(jnp.int32)
    xs = _dispatch(pos, runs, pad_starts, pad_ends, counts, h_tiles, n_rows)
    ys = _experts(block_expert, nused, xs, w_gate, w_up, w_down, layer)
    return _combine(runs, pad_starts, h_tiles, gate, g2, b2, ys, batch, seq, chunk_major)


def _residue_rows(x_ref, dilation):
    n = x_ref.shape[1]
    if dilation == 1:
        return _from_chunks(x_ref)
    per = n // dilation
    return jnp.concatenate(
        [jnp.concatenate([x_ref[c, pl.ds(r, per, stride=dilation), :] for r in range(dilation)], 0)
         for c in range(N_CHUNKS)], -1)


def _proj_b_kernel(x_ref, wqkv_ref, wqm_ref, qkv0_ref, qkv1_ref, qkv2_ref, qm_ref):
    ts = x_ref.shape[1]
    qscale = jnp.where(
        lax.broadcasted_iota(jnp.int32, (1, 3 * DIL_OUT_WIDTH), 1) < DIL_OUT_WIDTH, ATT_SCALE, 1.0)
    for g, (out_ref, (_, dilation)) in enumerate(zip((qkv0_ref, qkv1_ref, qkv2_ref), DIL_PAIRS)):
        xb = _residue_rows(x_ref, dilation).astype(BF16)
        qkv = (_dot(xb, wqkv_ref[g]) * qscale).astype(BF16)
        per = ts // dilation
        for r in range(dilation):
            out_ref[r] = qkv[r * per:(r + 1) * per]
    qm = _dot(_from_chunks(x_ref).astype(BF16), wqm_ref[...])
    qm_ref[...] = (qm * ATT_SCALE).astype(BF16)


def _proj_b(x_cm, wqkv, wqm):
    b, _, s, _ = x_cm.shape
    ts = TS_MIX
    width = 3 * DIL_OUT_WIDTH
    out_specs, out_shapes = [], []
    for _, dilation in DIL_PAIRS:
        out_specs.append(pl.BlockSpec((None, dilation, ts // dilation, width),
                                      lambda i, j: (i, 0, j, 0)))
        out_shapes.append(jax.ShapeDtypeStruct((b, dilation, s // dilation, width), BF16))
    out_specs.append(pl.BlockSpec((None, ts, MEM_WIDTH), lambda i, j: (i, j, 0)))
    out_shapes.append(jax.ShapeDtypeStruct((b, s, MEM_WIDTH), BF16))
    return pl.pallas_call(
        _proj_b_kernel,
        grid=(b, s // ts),
        in_specs=[
            pl.BlockSpec((None, N_CHUNKS, ts, LANES), lambda i, j: (i, 0, j, 0)),
            pl.BlockSpec(wqkv.shape, lambda i, j: (0, 0, 0)),
            pl.BlockSpec(wqm.shape, lambda i, j: (0, 0)),
        ],
        out_specs=out_specs,
        out_shape=out_shapes,
        compiler_params=pltpu.CompilerParams(vmem_limit_bytes=VMEM_LIMIT),
        name="proj_b",
    )(x_cm, wqkv, wqm)


def _dil_attn_kernel(qkv_ref, o_ref, lse_ref, *, dilation, slopes):
    nb = qkv_ref.shape[1] // CHUNK
    width = DIL_OUT_WIDTH
    heads = HEADS_PER_DIL
    row = lax.broadcasted_iota(jnp.int32, (heads * CHUNK, 2 * CHUNK), 0)
    jk = lax.broadcasted_iota(jnp.int32, (heads * CHUNK, 2 * CHUNK), 1)
    rel = CHUNK + row % CHUNK - jk
    slope = functools.reduce(
        lambda acc, h: jnp.where(row // CHUNK == h, slopes[h], acc), range(1, heads), slopes[0])
    bias = jnp.where((rel >= 0) & (rel <= CHUNK), -slope * (dilation * rel).astype(F32), -jnp.inf)
    before_start = jk < CHUNK
    head = lax.broadcasted_iota(jnp.int32, (1, width), 1) // HEAD_DIM

    def scores(idx):
        r = idx // nb
        jb = idx % nb
        cur = pl.ds(pl.multiple_of(jb * CHUNK, CHUNK), CHUNK)
        prev = pl.ds(pl.multiple_of(jnp.maximum(jb - 1, 0) * CHUNK, CHUNK), CHUNK)
        q = qkv_ref[r, cur, 0:width]
        kband = jnp.concatenate([qkv_ref[r, prev, width:2 * width],
                                 qkv_ref[r, cur, width:2 * width]], 0)
        stacked = jnp.concatenate(
            [jnp.where(head == h, q, jnp.zeros_like(q)) for h in range(heads)], 0)
        s = _dot_nt(stacked, kband) + bias
        return r, cur, prev, jnp.where(before_start & (jb == 0), -jnp.inf, s)

    def softmax(s):
        m = jnp.max(s, -1, keepdims=True)
        p = jnp.exp(s - m)
        l = jnp.sum(p, -1, keepdims=True)
        return (p * (1.0 / l)).astype(BF16), m + jnp.log(l)

    def finish(r, cur, prev, p, row_lse):
        vband = jnp.concatenate([qkv_ref[r, prev, 2 * width:3 * width],
                                 qkv_ref[r, cur, 2 * width:3 * width]], 0)
        o = _dot(p, vband)
        out = jnp.zeros((CHUNK, width), F32)
        lse = jnp.zeros((CHUNK, width), F32)
        for h in range(heads):
            rows = slice(h * CHUNK, (h + 1) * CHUNK)
            out = jnp.where(head == h, o[rows], out)
            lse = jnp.where(head == h, row_lse[rows], lse)
        o_ref[r, cur, :] = out
        lse_ref[r, cur, :] = lse

    def group(i, carry):
        blocks = [scores(DIL_INTERLEAVE * i + j) for j in range(DIL_INTERLEAVE)]
        probs = [softmax(s) for _, _, _, s in blocks]
        for (r, cur, prev, _), (p, row_lse) in zip(blocks, probs):
            finish(r, cur, prev, p, row_lse)
        return carry

    lax.fori_loop(0, dilation * nb // DIL_INTERLEAVE, group, 0)


def _dil_attn(qkv, group):
    b, dilation, length, _ = qkv.shape
    slopes = tuple(2.0 ** (-8.0 * (group * HEADS_PER_DIL + h + 1) / N_DIL_HEADS)
                   for h in range(HEADS_PER_DIL))
    out_spec = pl.BlockSpec((None, dilation, length, DIL_OUT_WIDTH), lambda i: (i, 0, 0, 0))
    out_shape = jax.ShapeDtypeStruct((b, dilation, length, DIL_OUT_WIDTH), F32)
    return pl.pallas_call(
        functools.partial(_dil_attn_kernel, dilation=dilation, slopes=slopes),
        grid=(b,),
        in_specs=[pl.BlockSpec((None, dilation, length, 3 * DIL_OUT_WIDTH),
                               lambda i: (i, 0, 0, 0))],
        out_specs=[out_spec, out_spec],
        out_shape=[out_shape, out_shape],
        compiler_params=pltpu.CompilerParams(vmem_limit_bytes=VMEM_LIMIT),
        name=f"dil_attn_{group}",
    )(qkv)


def _token_order(src_ref, scratch, dilation):
    if dilation == 1:
        return src_ref[0]
    per = src_ref.shape[1]
    halves = DIL_OUT_WIDTH // LANES
    for r in range(dilation):
        v = src_ref[r]
        for c in range(halves):
            scratch[c, pl.ds(r, per, stride=dilation), :] = v[:, c * LANES:(c + 1) * LANES]
    return jnp.concatenate([scratch[c] for c in range(halves)], -1)


def _mixer_b_kernel(x_ref, qm_ref, o0_ref, l0_ref, o1_ref, l1_ref, o2_ref, l2_ref,
                    mk_ref, mv_ref, wout_ref, g1_ref, b1_ref, h_ref,
                    so1, sl1, so2, sl2):
    ts = x_ref.shape[1]
    outs = [o0_ref[0], _token_order(o1_ref, so1, DIL_PAIRS[1][1]),
            _token_order(o2_ref, so2, DIL_PAIRS[2][1])]
    lses = [l0_ref[0], _token_order(l1_ref, sl1, DIL_PAIRS[1][1]),
            _token_order(l2_ref, sl2, DIL_PAIRS[2][1])]
    top = functools.reduce(jnp.maximum, lses)
    es = [jnp.exp(l - top) for l in lses]
    inv = 1.0 / functools.reduce(jnp.add, es)
    mix = functools.reduce(jnp.add, [e * inv * o for e, o in zip(es, outs)])
    mo = _mem_attention(qm_ref[...], mk_ref[...], mv_ref[...])
    att = _dot(jnp.concatenate([mix, mo], -1).astype(BF16), wout_ref[...])
    h = _layer_norm(ALPHA * _from_chunks(x_ref) + att, g1_ref[...], b1_ref[...])
    _to_tiles(h_ref, h, ts)


def _mixer_b(x_cm, qm, attn, mk, mv, w_out, g1, b1):
    b, _, s, _ = x_cm.shape
    ts = TS_MIX
    nts = s // ts
    m = mk.shape[0] // b
    in_specs = [
        pl.BlockSpec((None, N_CHUNKS, ts, LANES), lambda i, j: (i, 0, j, 0)),
        pl.BlockSpec((None, ts, MEM_WIDTH), lambda i, j: (i, j, 0)),
    ]
    args = [x_cm, qm]
    for (o, lse), (_, dilation) in zip(attn, DIL_PAIRS):
        spec = pl.BlockSpec((None, dilation, ts // dilation, DIL_OUT_WIDTH),
                            lambda i, j: (i, 0, j, 0))
        in_specs += [spec, spec]
        args += [o, lse]
    in_specs += [
        pl.BlockSpec((m, MEM_WIDTH), lambda i, j: (i, 0)),
        pl.BlockSpec((m, MEM_WIDTH), lambda i, j: (i, 0)),
        pl.BlockSpec(w_out.shape, lambda i, j: (0, 0)),
        pl.BlockSpec((1, D_MODEL), lambda i, j: (0, 0)),
        pl.BlockSpec((1, D_MODEL), lambda i, j: (0, 0)),
    ]
    args += [mk, mv, w_out, g1, b1]
    halves = DIL_OUT_WIDTH // LANES
    return pl.pallas_call(
        _mixer_b_kernel,
        grid=(b, nts),
        in_specs=in_specs,
        out_specs=pl.BlockSpec((ts * N_CHUNKS, LANES), lambda i, j: (i * nts + j, 0)),
        out_shape=jax.ShapeDtypeStruct((b * s * N_CHUNKS, LANES), F32),
        scratch_shapes=[pltpu.VMEM((halves, ts, LANES), F32)] * 4,
        compiler_params=pltpu.CompilerParams(vmem_limit_bytes=VMEM_LIMIT),
        name="mixer_b",
    )(*args)


def kernel(x, mem, w_in_a, w_out_a, sg_ln_g, sg_ln_b, sg_w, sg_b, w_in_b, w_out_b, w_k_shared,
           w_v_shared, w_mem_k, w_mem_v, ln1_g, ln1_b, ln2_g, ln2_b, w_router, b_router,
           w_gate, w_up, w_down):
    batch, seq, d = x.shape
    mk, mv = _memkv(mem.reshape(-1, d), w_mem_k, w_mem_v)
    row = lambda v: v.reshape(1, -1)

    sg_bias = jnp.repeat(sg_b[0].T, HEAD_DIM, axis=1)
    h = _mixer_a(x, w_in_a[0].astype(BF16), w_out_a[0].astype(BF16), row(sg_ln_g[0]),
                 row(sg_ln_b[0]), sg_w[0], sg_bias, mk[0], mv[0], row(ln1_g[0]), row(ln1_b[0]))
    w_router_pad = jnp.pad(w_router, ((0, 0), (0, LANES - N_EXPERTS)))
    b_router_col = b_router.reshape(N_EXPERTS, 1)
    x_cm = _moe(h, batch, seq, w_router_pad, b_router_col, w_gate, w_up, w_down,
                row(ln2_g[0]), row(ln2_b[0]), layer=0, chunk_major=True)

    wqkv = jnp.stack([
        jnp.concatenate([w[:, g * DIL_OUT_WIDTH:(g + 1) * DIL_OUT_WIDTH]
                         for w in (w_in_b[0], w_k_shared, w_v_shared)], -1)
        for g in range(len(DIL_PAIRS))]).astype(BF16)
    wqm = w_in_b[0][:, DIL_Q_WIDTH:].astype(BF16)
    qkv0, qkv1, qkv2, qm = _proj_b(x_cm, wqkv, wqm)
    attn = [_dil_attn(qkv, g) for g, qkv in enumerate((qkv0, qkv1, qkv2))]
    h = _mixer_b(x_cm, qm, attn, mk[1], mv[1], w_out_b[0].astype(BF16),
                 row(ln1_g[1]), row(ln1_b[1]))
    return _moe(h, batch, seq, w_router_pad, b_router_col, w_gate, w_up, w_down,
                row(ln2_g[1]), row(ln2_b[1]), layer=1, chunk_major=False)
```

```python
import functools
import math

import jax
import jax.numpy as jnp
from jax import lax
from jax.experimental import pallas as pl
from jax.experimental.pallas import tpu as pltpu

D_MODEL = 1024
HEAD_DIM = 64
CHUNK = 128
N_SG = 12
SG_WIDTH = N_SG * HEAD_DIM
DIL_PAIRS = ((128, 1), (512, 4), (2048, 16))
HEADS_PER_DIL = 4
N_DIL_HEADS = HEADS_PER_DIL * len(DIL_PAIRS)
DIL_Q_WIDTH = N_DIL_HEADS * HEAD_DIM
DIL_OUT_WIDTH = HEADS_PER_DIL * HEAD_DIM
MEM_HEADS = 4
MEM_WIDTH = MEM_HEADS * HEAD_DIM
N_EXPERTS = 16
N_EXPERT_GROUPS = 4
EXPERTS_PER_GROUP = N_EXPERTS // N_EXPERT_GROUPS
TOP_K = 2
DEPTH = 2
ALPHA = (2 * DEPTH) ** 0.25
LN_EPS = 1e-5
ATT_SCALE = 1.0 / math.sqrt(HEAD_DIM)

LANES = 128
SUBLANES = 8
N_CHUNKS = D_MODEL // LANES

TS_MIX = 512
MXU_DEPTH = 256
TM_ROUTE = MXU_DEPTH
RUN_BITS = TM_ROUTE.bit_length()
TB_MOE = 512
HIDDEN_SLICES = 2
DIL_INTERLEAVE = 4
VMEM_LIMIT = 56 * 1024 * 1024

F32 = jnp.float32
BF16 = jnp.bfloat16


def _dot(a, b):
    return jnp.dot(a, b, preferred_element_type=F32)


def _dot_nt(a, b):
    return lax.dot_general(a, b, (((1,), (1,)), ((), ())), preferred_element_type=F32)


def _layer_norm(x, g, b):
    mu = jnp.mean(x, -1, keepdims=True)
    xc = x - mu
    var = jnp.mean(xc * xc, -1, keepdims=True)
    return xc * lax.rsqrt(var + LN_EPS) * g + b


def _gelu(x):
    return 0.5 * x * (1.0 + lax.erf(x * (1.0 / math.sqrt(2.0))))


def _from_tiles(ref, n):
    return jnp.concatenate(
        [ref[pl.ds(c, n, stride=N_CHUNKS), :] for c in range(N_CHUNKS)], -1)


def _to_tiles(ref, val, n):
    for c in range(N_CHUNKS):
        ref[pl.ds(c, n, stride=N_CHUNKS), :] = val[:, c * LANES:(c + 1) * LANES]


def _from_chunks(ref):
    return jnp.concatenate([ref[c] for c in range(N_CHUNKS)], -1)


def _mem_attention(q, mk, mv):
    n = q.shape[0]
    head = lax.broadcasted_iota(jnp.int32, (1, MEM_WIDTH), 1) // HEAD_DIM
    stacked = jnp.concatenate(
        [jnp.where(head == h, q, jnp.zeros_like(q)) for h in range(MEM_HEADS)], 0)
    s = _dot_nt(stacked, mk)
    m = jnp.max(s, -1, keepdims=True)
    p = jnp.exp(s - m)
    l = jnp.sum(p, -1, keepdims=True)
    o = _dot((p * (1.0 / l)).astype(BF16), mv)
    out = jnp.zeros((n, MEM_WIDTH), F32)
    for h in range(MEM_HEADS):
        out = jnp.where(head == h, o[h * n:(h + 1) * n], out)
    return out


def _memkv_kernel(mem_ref, wk_ref, wv_ref, mk_ref, mv_ref):
    m = mem_ref[...].astype(BF16)
    mk_ref[...] = _dot(m, wk_ref[...].astype(BF16)).astype(BF16)
    mv_ref[...] = _dot(m, wv_ref[...].astype(BF16)).astype(BF16)


def _memkv(mem2d, w_mem_k, w_mem_v):
    n = mem2d.shape[0]
    return pl.pallas_call(
        _memkv_kernel,
        grid=(DEPTH,),
        in_specs=[
            pl.BlockSpec((n, D_MODEL), lambda l: (0, 0)),
            pl.BlockSpec((None, D_MODEL, MEM_WIDTH), lambda l: (l, 0, 0)),
            pl.BlockSpec((None, D_MODEL, MEM_WIDTH), lambda l: (l, 0, 0)),
        ],
        out_specs=[
            pl.BlockSpec((None, n, MEM_WIDTH), lambda l: (l, 0, 0)),
            pl.BlockSpec((None, n, MEM_WIDTH), lambda l: (l, 0, 0)),
        ],
        out_shape=[jax.ShapeDtypeStruct((DEPTH, n, MEM_WIDTH), BF16)] * 2,
        compiler_params=pltpu.CompilerParams(vmem_limit_bytes=VMEM_LIMIT),
        name="memkv",
    )(mem2d, w_mem_k, w_mem_v)


def _mixer_a_kernel(x_ref, win_ref, wout_ref, sgg_ref, sgb_ref, sgw_ref, sgbias_ref,
                    mk_ref, mv_ref, g1_ref, b1_ref, h_ref):
    ts = x_ref.shape[0]
    x = x_ref[...]
    proj = _dot(x.astype(BF16), win_ref[...])
    u = _gelu(proj[:, :SG_WIDTH])
    gv = _gelu(proj[:, SG_WIDTH:2 * SG_WIDTH])
    gv = _layer_norm(gv, sgg_ref[...], sgb_ref[...]).astype(BF16)

    row = lax.broadcasted_iota(jnp.int32, (CHUNK, CHUNK), 0)
    col = lax.broadcasted_iota(jnp.int32, (CHUNK, CHUNK), 1)
    ws = [jnp.where(row >= col, sgw_ref[g], 0.0).astype(BF16) for g in range(N_SG)]
    n_chunks = ts // CHUNK
    low_half = (lax.broadcasted_iota(jnp.int32, (CHUNK, n_chunks * LANES), 1) % LANES) < HEAD_DIM
    gated = []
    for j in range(N_SG // 2):
        slabs = jnp.concatenate(
            [gv[c * CHUNK:(c + 1) * CHUNK, j * LANES:(j + 1) * LANES] for c in range(n_chunks)], -1)
        both = _dot(jnp.concatenate([ws[2 * j], ws[2 * j + 1]], 0), slabs)
        gated.append(jnp.where(low_half, both[:CHUNK], both[CHUNK:]))
    rows = [jnp.concatenate([g[:, c * LANES:(c + 1) * LANES] for g in gated], -1)
            + sgbias_ref[...] for c in range(n_chunks)]
    mix = u * jnp.concatenate(rows, 0)

    qm = (proj[:, 2 * SG_WIDTH:] * ATT_SCALE).astype(BF16)
    mo = _mem_attention(qm, mk_ref[...], mv_ref[...])
    att = _dot(jnp.concatenate([mix, mo], -1).astype(BF16), wout_ref[...])
    h = _layer_norm(ALPHA * x + att, g1_ref[...], b1_ref[...])
    _to_tiles(h_ref, h, ts)


def _mixer_a(x, w_in, w_out, sg_g, sg_b, sg_w, sg_bias, mk, mv, g1, b1):
    b, s, d = x.shape
    nts = s // TS_MIX
    m = mk.shape[0] // b
    full = lambda *shape: pl.BlockSpec(shape, lambda i, j: (0,) * len(shape))
    return pl.pallas_call(
        _mixer_a_kernel,
        grid=(b, nts),
        in_specs=[
            pl.BlockSpec((None, TS_MIX, d), lambda i, j: (i, j, 0)),
            full(d, 2 * SG_WIDTH + MEM_WIDTH),
            full(SG_WIDTH + MEM_WIDTH, d),
            full(1, SG_WIDTH),
            full(1, SG_WIDTH),
            full(N_SG, CHUNK, CHUNK),
            full(CHUNK, SG_WIDTH),
            pl.BlockSpec((m, MEM_WIDTH), lambda i, j: (i, 0)),
            pl.BlockSpec((m, MEM_WIDTH), lambda i, j: (i, 0)),
            full(1, d),
            full(1, d),
        ],
        out_specs=pl.BlockSpec((TS_MIX * N_CHUNKS, LANES), lambda i, j: (i * nts + j, 0)),
        out_shape=jax.ShapeDtypeStruct((b * s * N_CHUNKS, LANES), F32),
        compiler_params=pltpu.CompilerParams(vmem_limit_bytes=VMEM_LIMIT),
        name="mixer_a",
    )(x, w_in, w_out, sg_g, sg_b, sg_w, sg_bias, mk, mv, g1, b1)


def _router_kernel(h_ref, wr_ref, br_ref, pos_ref, gate_ref, runs_ref, counts_ref, base_ref):
    tm = TM_ROUTE

    @pl.when(pl.program_id(0) == 0)
    def _():
        base_ref[...] = jnp.zeros_like(base_ref)

    h = _from_tiles(h_ref, tm)
    w = wr_ref[...]
    h_hi = h.astype(BF16)
    h_lo = (h - h_hi.astype(F32)).astype(BF16)
    w_hi = w.astype(BF16)
    w_lo = (w - w_hi.astype(F32)).astype(BF16)
    logits = _dot(h_hi, w_hi) + (_dot(h_hi, w_lo) + _dot(h_lo, w_hi))
    lt = logits.T[:N_EXPERTS]

    ex = jnp.exp(lt - jnp.max(lt, 0, keepdims=True))
    probs = ex / jnp.sum(ex, 0, keepdims=True)
    sel = probs + br_ref[...]
    eid = lax.broadcasted_iota(jnp.int32, (N_EXPERTS, tm), 0)
    group = eid // EXPERTS_PER_GROUP

    def top2(mask):
        v = jnp.where(mask, sel, -jnp.inf)
        m1 = jnp.max(v, 0, keepdims=True)
        i1 = jnp.min(jnp.where(v == m1, eid, N_EXPERTS), 0, keepdims=True)
        v2 = jnp.where(eid == i1, -jnp.inf, v)
        m2 = jnp.max(v2, 0, keepdims=True)
        i2 = jnp.min(jnp.where(v2 == m2, eid, N_EXPERTS), 0, keepdims=True)
        return m1, i1, m2, i2

    scores = []
    for g in range(N_EXPERT_GROUPS):
        m1, _, m2, _ = top2(group == g)
        scores.append(m1 + m2)
    best = functools.reduce(jnp.maximum, scores)
    g_idx = jnp.full((1, tm), N_EXPERT_GROUPS - 1, jnp.int32)
    for g in reversed(range(N_EXPERT_GROUPS - 1)):
        g_idx = jnp.where(scores[g] == best, g, g_idx)
    _, e0, _, e1 = top2(group == g_idx)

    hot0 = eid == e0
    hot1 = eid == e1
    p0 = jnp.sum(jnp.where(hot0, probs, 0.0), 0, keepdims=True)
    p1 = jnp.sum(jnp.where(hot1, probs, 0.0), 0, keepdims=True)
    psum = p0 + p1

    hot = jnp.where(hot0 | hot1, 1.0, 0.0)
    earlier = (lax.broadcasted_iota(jnp.int32, (tm, tm), 0)
               < lax.broadcasted_iota(jnp.int32, (tm, tm), 1))
    hot_bf = hot.astype(BF16)
    before = _dot(hot_bf, jnp.where(earlier, 1.0, 0.0).astype(BF16))
    below = (lax.broadcasted_iota(jnp.int32, (N_EXPERTS, N_EXPERTS), 1)
             < lax.broadcasted_iota(jnp.int32, (N_EXPERTS, N_EXPERTS), 0))
    run_start = jnp.sum(_dot(jnp.where(below, 1.0, 0.0).astype(BF16), hot_bf), 1, keepdims=True)
    local = run_start + before
    q0 = jnp.sum(jnp.where(hot0, local, 0.0), 0, keepdims=True)
    q1 = jnp.sum(jnp.where(hot1, local, 0.0), 0, keepdims=True)

    row = lax.broadcasted_iota(jnp.int32, (SUBLANES, tm), 0)
    pos_ref[...] = jnp.where(row == 0, q0, jnp.where(row == 1, q1, 0.0)).astype(jnp.int32)
    grow = lax.broadcasted_iota(jnp.int32, (LANES, tm), 0)
    cols = jnp.where(grow == 0, p0 / psum, jnp.where(grow == 1, p1 / psum, jnp.where(
        grow == 2, q0, jnp.where(grow == 3, q1, 0.0))))
    gate_ref[...] = cols.T[:, :2 * TOP_K]
    count = jnp.sum(hot, 1, keepdims=True)
    lane = lax.broadcasted_iota(jnp.int32, (N_EXPERTS, LANES), 1)
    runs_ref[...] = jnp.where(lane == 0, count, jnp.where(
        lane == 1, base_ref[...], 0.0)).astype(jnp.int32)
    total = base_ref[...] + count
    base_ref[...] = total
    counts_ref[...] = total.astype(jnp.int32)


def _router(h_tiles, w_router, b_router):
    t = h_tiles.shape[0] // N_CHUNKS
    tm = TM_ROUTE
    return pl.pallas_call(
        _router_kernel,
        grid=(t // tm,),
        in_specs=[
            pl.BlockSpec((tm * N_CHUNKS, LANES), lambda i: (i, 0)),
            pl.BlockSpec((D_MODEL, LANES), lambda i: (0, 0)),
            pl.BlockSpec((N_EXPERTS, 1), lambda i: (0, 0)),
        ],
        out_specs=[
            pl.BlockSpec((None, SUBLANES, tm), lambda i: (i, 0, 0)),
            pl.BlockSpec((tm, 2 * TOP_K), lambda i: (i, 0)),
            pl.BlockSpec((None, N_EXPERTS, LANES), lambda i: (i, 0, 0)),
            pl.BlockSpec((N_EXPERTS, 1), lambda i: (0, 0)),
        ],
        out_shape=[
            jax.ShapeDtypeStruct((t // tm, SUBLANES, tm), jnp.int32),
            jax.ShapeDtypeStruct((t, 2 * TOP_K), F32),
            jax.ShapeDtypeStruct((t // tm, N_EXPERTS, LANES), jnp.int32),
            jax.ShapeDtypeStruct((N_EXPERTS, 1), jnp.int32),
        ],
        scratch_shapes=[pltpu.VMEM((N_EXPERTS, 1), F32)],
        compiler_params=pltpu.CompilerParams(
            dimension_semantics=("arbitrary",), vmem_limit_bytes=VMEM_LIMIT),
        name="router",
    )(h_tiles, w_router, b_router)


def _start_run_copies(runs_ref, start_ref, tile_ref, rows_ref, sem, to_rows):
    offset = 0
    for e in range(N_EXPERTS):
        count = runs_ref[e, 0]
        first_row = start_ref[e] + runs_ref[e, 1]
        for bit in reversed(range(RUN_BITS)):
            size = (1 << bit) * N_CHUNKS
            done = (count >> (bit + 1)) << (bit + 1)

            @pl.when(((count >> bit) & 1) == 1)
            def _():
                in_tile = tile_ref.at[
                    pl.ds(pl.multiple_of((offset + done) * N_CHUNKS, N_CHUNKS), size)]
                in_rows = rows_ref.at[
                    pl.ds(pl.multiple_of((first_row + done) * N_CHUNKS, N_CHUNKS), size)]
                if to_rows:
                    pltpu.make_async_copy(in_tile, in_rows, sem).start()
                else:
                    pltpu.make_async_copy(in_rows, in_tile, sem).start()
        offset = offset + count


def _wait_run_copies(tile_ref, rows_ref, sem):
    pltpu.make_async_copy(tile_ref, rows_ref.at[pl.ds(0, tile_ref.shape[0])], sem).wait()


def _dispatch_kernel(pos_ref, runs_ref, start_ref, end_ref, cnt_ref, h_ref, xs_ref,
                     zero_ref, sorted_ref, sem, zsem):
    i = pl.program_id(0)
    tm = TM_ROUTE

    @pl.when(i == 0)
    def _():
        zero_ref[...] = jnp.zeros_like(zero_ref)
        for e in range(N_EXPERTS):
            @pl.when(cnt_ref[e, 0] > 0)
            def _():
                first = pl.multiple_of((end_ref[e] - TB_MOE) * N_CHUNKS, N_CHUNKS)
                cp = pltpu.make_async_copy(
                    zero_ref, xs_ref.at[pl.ds(first, TB_MOE * N_CHUNKS)], zsem)
                cp.start()
                cp.wait()

        def zero_block(blk, carry):
            first = pl.multiple_of(blk * (TB_MOE * N_CHUNKS), TB_MOE * N_CHUNKS)
            cp = pltpu.make_async_copy(
                zero_ref, xs_ref.at[pl.ds(first, TB_MOE * N_CHUNKS)], zsem)
            cp.start()
            cp.wait()
            return carry

        n_blocks = xs_ref.shape[0] // (TB_MOE * N_CHUNKS)
        lax.fori_loop(end_ref[N_EXPERTS - 1] // TB_MOE, n_blocks, zero_block, 0)

    p = lax.broadcasted_iota(jnp.int32, (TOP_K * tm, tm), 0)
    onehot = (p == pos_ref[0:1, :]) | (p == pos_ref[1:2, :])
    sorted_rows = _dot(jnp.where(onehot, 1.0, 0.0).astype(BF16),
                       _from_tiles(h_ref, tm).astype(BF16))

    slot = i % 2
    buf = sorted_ref.at[slot]

    @pl.when(i >= 2)
    def _():
        _wait_run_copies(buf, xs_ref, sem.at[slot])

    _to_tiles(buf, sorted_rows, TOP_K * tm)
    _start_run_copies(runs_ref, start_ref, buf, xs_ref, sem.at[slot], to_rows=True)

    @pl.when(i == pl.num_programs(0) - 1)
    def _():
        _wait_run_copies(buf, xs_ref, sem.at[slot])
        _wait_run_copies(sorted_ref.at[1 - slot], xs_ref, sem.at[1 - slot])


def _dispatch(pos, runs, pad_starts, pad_ends, counts, h_tiles, n_rows):
    nt = pos.shape[0]
    tm = TM_ROUTE
    smem_vec = pl.BlockSpec(memory_space=pltpu.SMEM)
    return pl.pallas_call(
        _dispatch_kernel,
        grid=(nt,),
        in_specs=[
            pl.BlockSpec((None, SUBLANES, tm), lambda i: (i, 0, 0)),
            pl.BlockSpec((None, N_EXPERTS, LANES), lambda i: (i, 0, 0), memory_space=pltpu.SMEM),
            smem_vec, smem_vec, smem_vec,
            pl.BlockSpec((tm * N_CHUNKS, LANES), lambda i: (i, 0))],
        out_specs=pl.BlockSpec(memory_space=pl.ANY),
        out_shape=jax.ShapeDtypeStruct((n_rows * N_CHUNKS, LANES), F32),
        scratch_shapes=[
            pltpu.VMEM((TB_MOE * N_CHUNKS, LANES), F32),
            pltpu.VMEM((2, TOP_K * tm * N_CHUNKS, LANES), F32),
            pltpu.SemaphoreType.DMA((2,)),
            pltpu.SemaphoreType.DMA(()),
        ],
        compiler_params=pltpu.CompilerParams(
            dimension_semantics=("arbitrary",), has_side_effects=True,
            vmem_limit_bytes=VMEM_LIMIT),
        name="dispatch",
    )(pos, runs, pad_starts, pad_ends, counts, h_tiles)


def _experts_kernel(be_ref, nused_ref, xs_ref, wg_ref, wu_ref, wd_ref, ys_ref,
                    wg_bf, wu_bf, wd_bf):
    i = pl.program_id(0)
    nused = nused_ref[0]

    @pl.when(i < nused)
    def _():
        prev = be_ref[jnp.maximum(i - 1, 0)]

        @pl.when((i == 0) | (be_ref[i] != prev))
        def _():
            wg_bf[...] = wg_ref[...].astype(BF16)
            wu_bf[...] = wu_ref[...].astype(BF16)
            wd_bf[...] = wd_ref[...].astype(BF16)

        x = _from_tiles(xs_ref, TB_MOE).astype(BF16)
        width = wg_bf.shape[1] // HIDDEN_SLICES
        cols = [slice(k * width, (k + 1) * width) for k in range(HIDDEN_SLICES)]
        acts = []
        for c in cols:
            hg = _dot(x, wg_bf[:, c])
            acts.append((hg * jax.nn.sigmoid(hg) * _dot(x, wu_bf[:, c])).astype(BF16))
        y = functools.reduce(jnp.add, [_dot(act, wd_bf[c, :]) for act, c in zip(acts, cols)])
        _to_tiles(ys_ref, y, TB_MOE)

    @pl.when(i >= nused)
    def _():
        ys_ref[...] = jnp.zeros_like(ys_ref)


def _experts(block_expert, nused, xs, w_gate, w_up, w_down, layer):
    n_blocks = xs.shape[0] // (TB_MOE * N_CHUNKS)
    d, de = w_gate.shape[2], w_gate.shape[3]

    def x_map(i, be, nu):
        return (jnp.minimum(i, nu[0] - 1), 0)

    def w_map(i, be, nu):
        return (layer, be[jnp.minimum(i, nu[0] - 1)], 0, 0)

    return pl.pallas_call(
        _experts_kernel,
        grid_spec=pltpu.PrefetchScalarGridSpec(
            num_scalar_prefetch=2,
            grid=(n_blocks,),
            in_specs=[
                pl.BlockSpec((TB_MOE * N_CHUNKS, LANES), x_map),
                pl.BlockSpec((None, None, d, de), w_map),
                pl.BlockSpec((None, None, d, de), w_map),
                pl.BlockSpec((None, None, de, d), w_map),
            ],
            out_specs=pl.BlockSpec((TB_MOE * N_CHUNKS, LANES), lambda i, be, nu: (i, 0)),
            scratch_shapes=[
                pltpu.VMEM((d, de), BF16),
                pltpu.VMEM((d, de), BF16),
                pltpu.VMEM((de, d), BF16),
            ],
        ),
        out_shape=jax.ShapeDtypeStruct(xs.shape, F32),
        compiler_params=pltpu.CompilerParams(
            dimension_semantics=("arbitrary",), vmem_limit_bytes=VMEM_LIMIT),
        name="experts",
    )(block_expert, nused, xs, w_gate, w_up, w_down)


def _combine_kernel(runs_ref, next_runs_ref, start_ref, h_ref, gate_ref, g2_ref, b2_ref, ys_ref,
                    out_ref, sorted_ref, sem, *, chunk_major):
    tm = TM_ROUTE
    i = pl.program_id(0)
    slot = i % 2

    @pl.when(i == 0)
    def _():
        _start_run_copies(runs_ref, start_ref, sorted_ref.at[0], ys_ref, sem.at[0], to_rows=False)

    @pl.when(i + 1 < pl.num_programs(0))
    def _():
        _start_run_copies(next_runs_ref, start_ref, sorted_ref.at[1 - slot], ys_ref,
                          sem.at[1 - slot], to_rows=False)

    _wait_run_copies(sorted_ref.at[slot], ys_ref, sem.at[slot])

    gate = gate_ref[...]
    p = lax.broadcasted_iota(jnp.int32, (tm, TOP_K * tm), 1)
    onehot = jnp.concatenate([p == gate[:, 2:3].astype(jnp.int32),
                              p == gate[:, 3:4].astype(jnp.int32)], 0)
    picked = _dot(jnp.where(onehot, 1.0, 0.0).astype(BF16),
                  _from_tiles(sorted_ref.at[slot], TOP_K * tm).astype(BF16))
    ffn = picked[:tm] * gate[:, 0:1] + picked[tm:] * gate[:, 1:2]
    out = _layer_norm(ALPHA * _from_tiles(h_ref, tm) + ffn, g2_ref[...], b2_ref[...])
    if chunk_major:
        for c in range(N_CHUNKS):
            out_ref[c] = out[:, c * LANES:(c + 1) * LANES]
    else:
        out_ref[...] = out


def _combine(runs, pad_starts, h_tiles, gate, g2, b2, ys, batch, seq, chunk_major):
    tm = TM_ROUTE
    t = batch * seq
    nt = t // tm
    nts = seq // tm
    if chunk_major:
        out_spec = pl.BlockSpec((None, N_CHUNKS, tm, LANES), lambda i: (i // nts, 0, i % nts, 0))
        out_shape = jax.ShapeDtypeStruct((batch, N_CHUNKS, seq, LANES), F32)
    else:
        out_spec = pl.BlockSpec((None, tm, D_MODEL), lambda i: (i // nts, i % nts, 0))
        out_shape = jax.ShapeDtypeStruct((batch, seq, D_MODEL), F32)
    return pl.pallas_call(
        functools.partial(_combine_kernel, chunk_major=chunk_major),
        grid=(nt,),
        in_specs=[
            pl.BlockSpec((None, N_EXPERTS, LANES), lambda i: (i, 0, 0), memory_space=pltpu.SMEM),
            pl.BlockSpec((None, N_EXPERTS, LANES), lambda i: (jnp.minimum(i + 1, nt - 1), 0, 0),
                         memory_space=pltpu.SMEM),
            pl.BlockSpec(memory_space=pltpu.SMEM),
            pl.BlockSpec((tm * N_CHUNKS, LANES), lambda i: (i, 0)),
            pl.BlockSpec((tm, 2 * TOP_K), lambda i: (i, 0)),
            pl.BlockSpec((1, D_MODEL), lambda i: (0, 0)),
            pl.BlockSpec((1, D_MODEL), lambda i: (0, 0)),
            pl.BlockSpec(memory_space=pl.ANY),
        ],
        out_specs=out_spec,
        out_shape=out_shape,
        scratch_shapes=[
            pltpu.VMEM((2, TOP_K * tm * N_CHUNKS, LANES), F32),
            pltpu.SemaphoreType.DMA((2,)),
        ],
        compiler_params=pltpu.CompilerParams(
            dimension_semantics=("arbitrary",), vmem_limit_bytes=VMEM_LIMIT),
        name="combine",
    )(runs, runs, pad_starts, h_tiles, gate, g2, b2, ys)


def _moe(h_tiles, batch, seq, w_router, b_router, w_gate, w_up, w_down, g2, b2, layer,
         chunk_major):
    t = batch * seq
    pos, gate, runs, counts = _router(h_tiles, w_router, b_router)
    cnt = counts[:, 0]
    padded = (cnt + TB_MOE - 1) // TB_MOE * TB_MOE
    pad_ends = jnp.cumsum(padded).astype(jnp.int32)
    pad_starts = pad_ends - padded
    n_rows = t * TOP_K + N_EXPERTS * TB_MOE
    n_blocks = n_rows // TB_MOE
    block_start = jnp.arange(n_blocks, dtype=jnp.int32) * TB_MOE
    block_expert = jnp.minimum(
        jnp.sum(block_start[:, None] >= pad_ends[None, :], -1), N_EXPERTS - 1).astype(jnp.int32)
    nused = (pad_ends[-1:] // TB_MOE).astype(jnp.int32)
    xs = _dispatch(pos, runs, pad_starts, pad_ends, counts, h_tiles, n_rows)
    ys = _experts(block_expert, nused, xs, w_gate, w_up, w_down, layer)
    return _combine(runs, pad_starts, h_tiles, gate, g2, b2, ys, batch, seq, chunk_major)


def _residue_rows(x_ref, dilation):
    n = x_ref.shape[1]
    if dilation == 1:
        return _from_chunks(x_ref)
    per = n // dilation
    return jnp.concatenate(
        [jnp.concatenate([x_ref[c, pl.ds(r, per, stride=dilation), :] for r in range(dilation)], 0)
         for c in range(N_CHUNKS)], -1)


def _proj_b_kernel(x_ref, wqkv_ref, wqm_ref, qkv0_ref, qkv1_ref, qkv2_ref, qm_ref):
    ts = x_ref.shape[1]
    qscale = jnp.where(
        lax.broadcasted_iota(jnp.int32, (1, 3 * DIL_OUT_WIDTH), 1) < DIL_OUT_WIDTH, ATT_SCALE, 1.0)
    for g, (out_ref, (_, dilation)) in enumerate(zip((qkv0_ref, qkv1_ref, qkv2_ref), DIL_PAIRS)):
        xb = _residue_rows(x_ref, dilation).astype(BF16)
        qkv = (_dot(xb, wqkv_ref[g]) * qscale).astype(BF16)
        per = ts // dilation
        for r in range(dilation):
            out_ref[r] = qkv[r * per:(r + 1) * per]
    qm = _dot(_from_chunks(x_ref).astype(BF16), wqm_ref[...])
    qm_ref[...] = (qm * ATT_SCALE).astype(BF16)


def _proj_b(x_cm, wqkv, wqm):
    b, _, s, _ = x_cm.shape
    ts = TS_MIX
    width = 3 * DIL_OUT_WIDTH
    out_specs, out_shapes = [], []
    for _, dilation in DIL_PAIRS:
        out_specs.append(pl.BlockSpec((None, dilation, ts // dilation, width),
                                      lambda i, j: (i, 0, j, 0)))
        out_shapes.append(jax.ShapeDtypeStruct((b, dilation, s // dilation, width), BF16))
    out_specs.append(pl.BlockSpec((None, ts, MEM_WIDTH), lambda i, j: (i, j, 0)))
    out_shapes.append(jax.ShapeDtypeStruct((b, s, MEM_WIDTH), BF16))
    return pl.pallas_call(
        _proj_b_kernel,
        grid=(b, s // ts),
        in_specs=[
            pl.BlockSpec((None, N_CHUNKS, ts, LANES), lambda i, j: (i, 0, j, 0)),
            pl.BlockSpec(wqkv.shape, lambda i, j: (0, 0, 0)),
            pl.BlockSpec(wqm.shape, lambda i, j: (0, 0)),
        ],
        out_specs=out_specs,
        out_shape=out_shapes,
        compiler_params=pltpu.CompilerParams(vmem_limit_bytes=VMEM_LIMIT),
        name="proj_b",
    )(x_cm, wqkv, wqm)


def _dil_attn_kernel(qkv_ref, o_ref, lse_ref, *, dilation, slopes):
    nb = qkv_ref.shape[1] // CHUNK
    width = DIL_OUT_WIDTH
    heads = HEADS_PER_DIL
    row = lax.broadcasted_iota(jnp.int32, (heads * CHUNK, 2 * CHUNK), 0)
    jk = lax.broadcasted_iota(jnp.int32, (heads * CHUNK, 2 * CHUNK), 1)
    rel = CHUNK + row % CHUNK - jk
    slope = functools.reduce(
        lambda acc, h: jnp.where(row // CHUNK == h, slopes[h], acc), range(1, heads), slopes[0])
    bias = jnp.where((rel >= 0) & (rel <= CHUNK), -slope * (dilation * rel).astype(F32), -jnp.inf)
    before_start = jk < CHUNK
    head = lax.broadcasted_iota(jnp.int32, (1, width), 1) // HEAD_DIM

    def scores(idx):
        r = idx // nb
        jb = idx % nb
        cur = pl.ds(pl.multiple_of(jb * CHUNK, CHUNK), CHUNK)
        prev = pl.ds(pl.multiple_of(jnp.maximum(jb - 1, 0) * CHUNK, CHUNK), CHUNK)
        q = qkv_ref[r, cur, 0:width]
        kband = jnp.concatenate([qkv_ref[r, prev, width:2 * width],
                                 qkv_ref[r, cur, width:2 * width]], 0)
        stacked = jnp.concatenate(
            [jnp.where(head == h, q, jnp.zeros_like(q)) for h in range(heads)], 0)
        s = _dot_nt(stacked, kband) + bias
        return r, cur, prev, jnp.where(before_start & (jb == 0), -jnp.inf, s)

    def softmax(s):
        m = jnp.max(s, -1, keepdims=True)
        p = jnp.exp(s - m)
        l = jnp.sum(p, -1, keepdims=True)
        return (p * (1.0 / l)).astype(BF16), m + jnp.log(l)

    def finish(r, cur, prev, p, row_lse):
        vband = jnp.concatenate([qkv_ref[r, prev, 2 * width:3 * width],
                                 qkv_ref[r, cur, 2 * width:3 * width]], 0)
        o = _dot(p, vband)
        out = jnp.zeros((CHUNK, width), F32)
        lse = jnp.zeros((CHUNK, width), F32)
        for h in range(heads):
            rows = slice(h * CHUNK, (h + 1) * CHUNK)
            out = jnp.where(head == h, o[rows], out)
            lse = jnp.where(head == h, row_lse[rows], lse)
        o_ref[r, cur, :] = out
        lse_ref[r, cur, :] = lse

    def group(i, carry):
        blocks = [scores(DIL_INTERLEAVE * i + j) for j in range(DIL_INTERLEAVE)]
        probs = [softmax(s) for _, _, _, s in blocks]
        for (r, cur, prev, _), (p, row_lse) in zip(blocks, probs):
            finish(r, cur, prev, p, row_lse)
        return carry

    lax.fori_loop(0, dilation * nb // DIL_INTERLEAVE, group, 0)


def _dil_attn(qkv, group):
    b, dilation, length, _ = qkv.shape
    slopes = tuple(2.0 ** (-8.0 * (group * HEADS_PER_DIL + h + 1) / N_DIL_HEADS)
                   for h in range(HEADS_PER_DIL))
    out_spec = pl.BlockSpec((None, dilation, length, DIL_OUT_WIDTH), lambda i: (i, 0, 0, 0))
    out_shape = jax.ShapeDtypeStruct((b, dilation, length, DIL_OUT_WIDTH), F32)
    return pl.pallas_call(
        functools.partial(_dil_attn_kernel, dilation=dilation, slopes=slopes),
        grid=(b,),
        in_specs=[pl.BlockSpec((None, dilation, length, 3 * DIL_OUT_WIDTH),
                               lambda i: (i, 0, 0, 0))],
        out_specs=[out_spec, out_spec],
        out_shape=[out_shape, out_shape],
        compiler_params=pltpu.CompilerParams(vmem_limit_bytes=VMEM_LIMIT),
        name=f"dil_attn_{group}",
    )(qkv)


def _token_order(src_ref, scratch, dilation):
    if dilation == 1:
        return src_ref[0]
    per = src_ref.shape[1]
    halves = DIL_OUT_WIDTH // LANES
    for r in range(dilation):
        v = src_ref[r]
        for c in range(halves):
            scratch[c, pl.ds(r, per, stride=dilation), :] = v[:, c * LANES:(c + 1) * LANES]
    return jnp.concatenate([scratch[c] for c in range(halves)], -1)


def _mixer_b_kernel(x_ref, qm_ref, o0_ref, l0_ref, o1_ref, l1_ref, o2_ref, l2_ref,
                    mk_ref, mv_ref, wout_ref, g1_ref, b1_ref, h_ref,
                    so1, sl1, so2, sl2):
    ts = x_ref.shape[1]
    outs = [o0_ref[0], _token_order(o1_ref, so1, DIL_PAIRS[1][1]),
            _token_order(o2_ref, so2, DIL_PAIRS[2][1])]
    lses = [l0_ref[0], _token_order(l1_ref, sl1, DIL_PAIRS[1][1]),
            _token_order(l2_ref, sl2, DIL_PAIRS[2][1])]
    top = functools.reduce(jnp.maximum, lses)
    es = [jnp.exp(l - top) for l in lses]
    inv = 1.0 / functools.reduce(jnp.add, es)
    mix = functools.reduce(jnp.add, [e * inv * o for e, o in zip(es, outs)])
    mo = _mem_attention(qm_ref[...], mk_ref[...], mv_ref[...])
    att = _dot(jnp.concatenate([mix, mo], -1).astype(BF16), wout_ref[...])
    h = _layer_norm(ALPHA * _from_chunks(x_ref) + att, g1_ref[...], b1_ref[...])
    _to_tiles(h_ref, h, ts)


def _mixer_b(x_cm, qm, attn, mk, mv, w_out, g1, b1):
    b, _, s, _ = x_cm.shape
    ts = TS_MIX
    nts = s // ts
    m = mk.shape[0] // b
    in_specs = [
        pl.BlockSpec((None, N_CHUNKS, ts, LANES), lambda i, j: (i, 0, j, 0)),
        pl.BlockSpec((None, ts, MEM_WIDTH), lambda i, j: (i, j, 0)),
    ]
    args = [x_cm, qm]
    for (o, lse), (_, dilation) in zip(attn, DIL_PAIRS):
        spec = pl.BlockSpec((None, dilation, ts // dilation, DIL_OUT_WIDTH),
                            lambda i, j: (i, 0, j, 0))
        in_specs += [spec, spec]
        args += [o, lse]
    in_specs += [
        pl.BlockSpec((m, MEM_WIDTH), lambda i, j: (i, 0)),
        pl.BlockSpec((m, MEM_WIDTH), lambda i, j: (i, 0)),
        pl.BlockSpec(w_out.shape, lambda i, j: (0, 0)),
        pl.BlockSpec((1, D_MODEL), lambda i, j: (0, 0)),
        pl.BlockSpec((1, D_MODEL), lambda i, j: (0, 0)),
    ]
    args += [mk, mv, w_out, g1, b1]
    halves = DIL_OUT_WIDTH // LANES
    return pl.pallas_call(
        _mixer_b_kernel,
        grid=(b, nts),
        in_specs=in_specs,
        out_specs=pl.BlockSpec((ts * N_CHUNKS, LANES), lambda i, j: (i * nts + j, 0)),
        out_shape=jax.ShapeDtypeStruct((b * s * N_CHUNKS, LANES), F32),
        scratch_shapes=[pltpu.VMEM((halves, ts, LANES), F32)] * 4,
        compiler_params=pltpu.CompilerParams(vmem_limit_bytes=VMEM_LIMIT),
        name="mixer_b",
    )(*args)


def kernel(x, mem, w_in_a, w_out_a, sg_ln_g, sg_ln_b, sg_w, sg_b, w_in_b, w_out_b, w_k_shared,
           w_v_shared, w_mem_k, w_mem_v, ln1_g, ln1_b, ln2_g, ln2_b, w_router, b_router,
           w_gate, w_up, w_down):
    batch, seq, d = x.shape
    mk, mv = _memkv(mem.reshape(-1, d), w_mem_k, w_mem_v)
    row = lambda v: v.reshape(1, -1)

    sg_bias = jnp.repeat(sg_b[0].T, HEAD_DIM, axis=1)
    h = _mixer_a(x, w_in_a[0].astype(BF16), w_out_a[0].astype(BF16), row(sg_ln_g[0]),
                 row(sg_ln_b[0]), sg_w[0], sg_bias, mk[0], mv[0], row(ln1_g[0]), row(ln1_b[0]))
    w_router_pad = jnp.pad(w_router, ((0, 0), (0, LANES - N_EXPERTS)))
    b_router_col = b_router.reshape(N_EXPERTS, 1)
    x_cm = _moe(h, batch, seq, w_router_pad, b_router_col, w_gate, w_up, w_down,
                row(ln2_g[0]), row(ln2_b[0]), layer=0, chunk_major=True)

    wqkv = jnp.stack([
        jnp.concatenate([w[:, g * DIL_OUT_WIDTH:(g + 1) * DIL_OUT_WIDTH]
                         for w in (w_in_b[0], w_k_shared, w_v_shared)], -1)
        for g in range(len(DIL_PAIRS))]).astype(BF16)
    wqm = w_in_b[0][:, DIL_Q_WIDTH:].astype(BF16)
    qkv0, qkv1, qkv2, qm = _proj_b(x_cm, wqkv, wqm)
    attn = [_dil_attn(qkv, g) for g, qkv in enumerate((qkv0, qkv1, qkv2))]
    h = _mixer_b(x_cm, qm, attn, mk[1], mv[1], w_out_b[0].astype(BF16),
                 row(ln1_g[1]), row(ln1_b[1]))
    return _moe(h, batch, seq, w_router_pad, b_router_col, w_gate, w_up, w_down,
                row(ln2_g[1]), row(ln2_b[1]), layer=1, chunk_major=False)
```

```python
import functools
import math

import jax
import jax.numpy as jnp
from jax import lax
from jax.experimental import pallas as pl
from jax.experimental.pallas import tpu as pltpu

D_MODEL = 1024
HEAD_DIM = 64
CHUNK = 128
N_SG = 12
SG_WIDTH = N_SG * HEAD_DIM
DIL_PAIRS = ((128, 1), (512, 4), (2048, 16))
HEADS_PER_DIL = 4
N_DIL_HEADS = HEADS_PER_DIL * len(DIL_PAIRS)
DIL_Q_WIDTH = N_DIL_HEADS * HEAD_DIM
DIL_OUT_WIDTH = HEADS_PER_DIL * HEAD_DIM
MEM_HEADS = 4
MEM_WIDTH = MEM_HEADS * HEAD_DIM
N_EXPERTS = 16
N_EXPERT_GROUPS = 4
EXPERTS_PER_GROUP = N_EXPERTS // N_EXPERT_GROUPS
TOP_K = 2
DEPTH = 2
ALPHA = (2 * DEPTH) ** 0.25
LN_EPS = 1e-5
ATT_SCALE = 1.0 / math.sqrt(HEAD_DIM)

LANES = 128
SUBLANES = 8
N_CHUNKS = D_MODEL // LANES

TS_MIX = 512
MXU_DEPTH = 256
TM_ROUTE = MXU_DEPTH
RUN_BITS = TM_ROUTE.bit_length()
TB_MOE = 512
HIDDEN_SLICES = 2
DIL_INTERLEAVE = 4
VMEM_LIMIT = 56 * 1024 * 1024

F32 = jnp.float32
BF16 = jnp.bfloat16


def _dot(a, b):
    return jnp.dot(a, b, preferred_element_type=F32)


def _dot_nt(a, b):
    return lax.dot_general(a, b, (((1,), (1,)), ((), ())), preferred_element_type=F32)


def _layer_norm(x, g, b):
    mu = jnp.mean(x, -1, keepdims=True)
    xc = x - mu
    var = jnp.mean(xc * xc, -1, keepdims=True)
    return xc * lax.rsqrt(var + LN_EPS) * g + b


def _gelu(x):
    return 0.5 * x * (1.0 + lax.erf(x * (1.0 / math.sqrt(2.0))))


def _from_tiles(ref, n):
    return jnp.concatenate(
        [ref[pl.ds(c, n, stride=N_CHUNKS), :] for c in range(N_CHUNKS)], -1)


def _to_tiles(ref, val, n):
    for c in range(N_CHUNKS):
        ref[pl.ds(c, n, stride=N_CHUNKS), :] = val[:, c * LANES:(c + 1) * LANES]


def _from_chunks(ref):
    return jnp.concatenate([ref[c] for c in range(N_CHUNKS)], -1)


def _mem_attention(q, mk, mv):
    n = q.shape[0]
    head = lax.broadcasted_iota(jnp.int32, (1, MEM_WIDTH), 1) // HEAD_DIM
    stacked = jnp.concatenate(
        [jnp.where(head == h, q, jnp.zeros_like(q)) for h in range(MEM_HEADS)], 0)
    s = _dot_nt(stacked, mk)
    m = jnp.max(s, -1, keepdims=True)
    p = jnp.exp(s - m)
    l = jnp.sum(p, -1, keepdims=True)
    o = _dot((p * (1.0 / l)).astype(BF16), mv)
    out = jnp.zeros((n, MEM_WIDTH), F32)
    for h in range(MEM_HEADS):
        out = jnp.where(head == h, o[h * n:(h + 1) * n], out)
    return out


def _memkv_kernel(mem_ref, wk_ref, wv_ref, mk_ref, mv_ref):
    m = mem_ref[...].astype(BF16)
    mk_ref[...] = _dot(m, wk_ref[...].astype(BF16)).astype(BF16)
    mv_ref[...] = _dot(m, wv_ref[...].astype(BF16)).astype(BF16)


def _memkv(mem2d, w_mem_k, w_mem_v):
    n = mem2d.shape[0]
    return pl.pallas_call(
        _memkv_kernel,
        grid=(DEPTH,),
        in_specs=[
            pl.BlockSpec((n, D_MODEL), lambda l: (0, 0)),
            pl.BlockSpec((None, D_MODEL, MEM_WIDTH), lambda l: (l, 0, 0)),
            pl.BlockSpec((None, D_MODEL, MEM_WIDTH), lambda l: (l, 0, 0)),
        ],
        out_specs=[
            pl.BlockSpec((None, n, MEM_WIDTH), lambda l: (l, 0, 0)),
            pl.BlockSpec((None, n, MEM_WIDTH), lambda l: (l, 0, 0)),
        ],
        out_shape=[jax.ShapeDtypeStruct((DEPTH, n, MEM_WIDTH), BF16)] * 2,
        compiler_params=pltpu.CompilerParams(vmem_limit_bytes=VMEM_LIMIT),
        name="memkv",
    )(mem2d, w_mem_k, w_mem_v)


def _mixer_a_kernel(x_ref, win_ref, wout_ref, sgg_ref, sgb_ref, sgw_ref, sgbias_ref,
                    mk_ref, mv_ref, g1_ref, b1_ref, h_ref):
    ts = x_ref.shape[0]
    x = x_ref[...]
    proj = _dot(x.astype(BF16), win_ref[...])
    u = _gelu(proj[:, :SG_WIDTH])
    gv = _gelu(proj[:, SG_WIDTH:2 * SG_WIDTH])
    gv = _layer_norm(gv, sgg_ref[...], sgb_ref[...]).astype(BF16)

    row = lax.broadcasted_iota(jnp.int32, (CHUNK, CHUNK), 0)
    col = lax.broadcasted_iota(jnp.int32, (CHUNK, CHUNK), 1)
    ws = [jnp.where(row >= col, sgw_ref[g], 0.0).astype(BF16) for g in range(N_SG)]
    n_chunks = ts // CHUNK
    low_half = (lax.broadcasted_iota(jnp.int32, (CHUNK, n_chunks * LANES), 1) % LANES) < HEAD_DIM
    gated = []
    for j in range(N_SG // 2):
        slabs = jnp.concatenate(
            [gv[c * CHUNK:(c + 1) * CHUNK, j * LANES:(j + 1) * LANES] for c in range(n_chunks)], -1)
        both = _dot(jnp.concatenate([ws[2 * j], ws[2 * j + 1]], 0), slabs)
        gated.append(jnp.where(low_half, both[:CHUNK], both[CHUNK:]))
    rows = [jnp.concatenate([g[:, c * LANES:(c + 1) * LANES] for g in gated], -1)
            + sgbias_ref[...] for c in range(n_chunks)]
    mix = u * jnp.concatenate(rows, 0)

    qm = (proj[:, 2 * SG_WIDTH:] * ATT_SCALE).astype(BF16)
    mo = _mem_attention(qm, mk_ref[...], mv_ref[...])
    att = _dot(jnp.concatenate([mix, mo], -1).astype(BF16), wout_ref[...])
    h = _layer_norm(ALPHA * x + att, g1_ref[...], b1_ref[...])
    _to_tiles(h_ref, h, ts)


def _mixer_a(x, w_in, w_out, sg_g, sg_b, sg_w, sg_bias, mk, mv, g1, b1):
    b, s, d = x.shape
    nts = s // TS_MIX
    m = mk.shape[0] // b
    full = lambda *shape: pl.BlockSpec(shape, lambda i, j: (0,) * len(shape))
    return pl.pallas_call(
        _mixer_a_kernel,
        grid=(b, nts),
        in_specs=[
            pl.BlockSpec((None, TS_MIX, d), lambda i, j: (i, j, 0)),
            full(d, 2 * SG_WIDTH + MEM_WIDTH),
            full(SG_WIDTH + MEM_WIDTH, d),
            full(1, SG_WIDTH),
            full(1, SG_WIDTH),
            full(N_SG, CHUNK, CHUNK),
            full(CHUNK, SG_WIDTH),
            pl.BlockSpec((m, MEM_WIDTH), lambda i, j: (i, 0)),
            pl.BlockSpec((m, MEM_WIDTH), lambda i, j: (i, 0)),
            full(1, d),
            full(1, d),
        ],
        out_specs=pl.BlockSpec((TS_MIX * N_CHUNKS, LANES), lambda i, j: (i * nts + j, 0)),
        out_shape=jax.ShapeDtypeStruct((b * s * N_CHUNKS, LANES), F32),
        compiler_params=pltpu.CompilerParams(vmem_limit_bytes=VMEM_LIMIT),
        name="mixer_a",
    )(x, w_in, w_out, sg_g, sg_b, sg_w, sg_bias, mk, mv, g1, b1)


def _router_kernel(h_ref, wr_ref, br_ref, pos_ref, gate_ref, runs_ref, counts_ref, base_ref):
    tm = TM_ROUTE

    @pl.when(pl.program_id(0) == 0)
    def _():
        base_ref[...] = jnp.zeros_like(base_ref)

    h = _from_tiles(h_ref, tm)
    w = wr_ref[...]
    h_hi = h.astype(BF16)
    h_lo = (h - h_hi.astype(F32)).astype(BF16)
    w_hi = w.astype(BF16)
    w_lo = (w - w_hi.astype(F32)).astype(BF16)
    logits = _dot(h_hi, w_hi) + (_dot(h_hi, w_lo) + _dot(h_lo, w_hi))
    lt = logits.T[:N_EXPERTS]

    ex = jnp.exp(lt - jnp.max(lt, 0, keepdims=True))
    probs = ex / jnp.sum(ex, 0, keepdims=True)
    sel = probs + br_ref[...]
    eid = lax.broadcasted_iota(jnp.int32, (N_EXPERTS, tm), 0)
    group = eid // EXPERTS_PER_GROUP

    def top2(mask):
        v = jnp.where(mask, sel, -jnp.inf)
        m1 = jnp.max(v, 0, keepdims=True)
        i1 = jnp.min(jnp.where(v == m1, eid, N_EXPERTS), 0, keepdims=True)
        v2 = jnp.where(eid == i1, -jnp.inf, v)
        m2 = jnp.max(v2, 0, keepdims=True)
        i2 = jnp.min(jnp.where(v2 == m2, eid, N_EXPERTS), 0, keepdims=True)
        return m1, i1, m2, i2

    scores = []
    for g in range(N_EXPERT_GROUPS):
        m1, _, m2, _ = top2(group == g)
        scores.append(m1 + m2)
    best = functools.reduce(jnp.maximum, scores)
    g_idx = jnp.full((1, tm), N_EXPERT_GROUPS - 1, jnp.int32)
    for g in reversed(range(N_EXPERT_GROUPS - 1)):
        g_idx = jnp.where(scores[g] == best, g, g_idx)
    _, e0, _, e1 = top2(group == g_idx)

    hot0 = eid == e0
    hot1 = eid == e1
    p0 = jnp.sum(jnp.where(hot0, probs, 0.0), 0, keepdims=True)
    p1 = jnp.sum(jnp.where(hot1, probs, 0.0), 0, keepdims=True)
    psum = p0 + p1

    hot = jnp.where(hot0 | hot1, 1.0, 0.0)
    earlier = (lax.broadcasted_iota(jnp.int32, (tm, tm), 0)
               < lax.broadcasted_iota(jnp.int32, (tm, tm), 1))
    hot_bf = hot.astype(BF16)
    before = _dot(hot_bf, jnp.where(earlier, 1.0, 0.0).astype(BF16))
    below = (lax.broadcasted_iota(jnp.int32, (N_EXPERTS, N_EXPERTS), 1)
             < lax.broadcasted_iota(jnp.int32, (N_EXPERTS, N_EXPERTS), 0))
    run_start = jnp.sum(_dot(jnp.where(below, 1.0, 0.0).astype(BF16), hot_bf), 1, keepdims=True)
    local = run_start + before
    q0 = jnp.sum(jnp.where(hot0, local, 0.0), 0, keepdims=True)
    q1 = jnp.sum(jnp.where(hot1, local, 0.0), 0, keepdims=True)

    row = lax.broadcasted_iota(jnp.int32, (SUBLANES, tm), 0)
    pos_ref[...] = jnp.where(row == 0, q0, jnp.where(row == 1, q1, 0.0)).astype(jnp.int32)
    grow = lax.broadcasted_iota(jnp.int32, (LANES, tm), 0)
    cols = jnp.where(grow == 0, p0 / psum, jnp.where(grow == 1, p1 / psum, jnp.where(
        grow == 2, q0, jnp.where(grow == 3, q1, 0.0))))
    gate_ref[...] = cols.T[:, :2 * TOP_K]
    count = jnp.sum(hot, 1, keepdims=True)
    lane = lax.broadcasted_iota(jnp.int32, (N_EXPERTS, LANES), 1)
    runs_ref[...] = jnp.where(lane == 0, count, jnp.where(
        lane == 1, base_ref[...], 0.0)).astype(jnp.int32)
    total = base_ref[...] + count
    base_ref[...] = total
    counts_ref[...] = total.astype(jnp.int32)


def _router(h_tiles, w_router, b_router):
    t = h_tiles.shape[0] // N_CHUNKS
    tm = TM_ROUTE
    return pl.pallas_call(
        _router_kernel,
        grid=(t // tm,),
        in_specs=[
            pl.BlockSpec((tm * N_CHUNKS, LANES), lambda i: (i, 0)),
            pl.BlockSpec((D_MODEL, LANES), lambda i: (0, 0)),
            pl.BlockSpec((N_EXPERTS, 1), lambda i: (0, 0)),
        ],
        out_specs=[
            pl.BlockSpec((None, SUBLANES, tm), lambda i: (i, 0, 0)),
            pl.BlockSpec((tm, 2 * TOP_K), lambda i: (i, 0)),
            pl.BlockSpec((None, N_EXPERTS, LANES), lambda i: (i, 0, 0)),
            pl.BlockSpec((N_EXPERTS, 1), lambda i: (0, 0)),
        ],
        out_shape=[
            jax.ShapeDtypeStruct((t // tm, SUBLANES, tm), jnp.int32),
            jax.ShapeDtypeStruct((t, 2 * TOP_K), F32),
            jax.ShapeDtypeStruct((t // tm, N_EXPERTS, LANES), jnp.int32),
            jax.ShapeDtypeStruct((N_EXPERTS, 1), jnp.int32),
        ],
        scratch_shapes=[pltpu.VMEM((N_EXPERTS, 1), F32)],
        compiler_params=pltpu.CompilerParams(
            dimension_semantics=("arbitrary",), vmem_limit_bytes=VMEM_LIMIT),
        name="router",
    )(h_tiles, w_router, b_router)


def _start_run_copies(runs_ref, start_ref, tile_ref, rows_ref, sem, to_rows):
    offset = 0
    for e in range(N_EXPERTS):
        count = runs_ref[e, 0]
        first_row = start_ref[e] + runs_ref[e, 1]
        for bit in reversed(range(RUN_BITS)):
            size = (1 << bit) * N_CHUNKS
            done = (count >> (bit + 1)) << (bit + 1)

            @pl.when(((count >> bit) & 1) == 1)
            def _():
                in_tile = tile_ref.at[
                    pl.ds(pl.multiple_of((offset + done) * N_CHUNKS, N_CHUNKS), size)]
                in_rows = rows_ref.at[
                    pl.ds(pl.multiple_of((first_row + done) * N_CHUNKS, N_CHUNKS), size)]
                if to_rows:
                    pltpu.make_async_copy(in_tile, in_rows, sem).start()
                else:
                    pltpu.make_async_copy(in_rows, in_tile, sem).start()
        offset = offset + count


def _wait_run_copies(tile_ref, rows_ref, sem):
    pltpu.make_async_copy(tile_ref, rows_ref.at[pl.ds(0, tile_ref.shape[0])], sem).wait()


def _dispatch_kernel(pos_ref, runs_ref, start_ref, end_ref, cnt_ref, h_ref, xs_ref,
                     zero_ref, sorted_ref, sem, zsem):
    i = pl.program_id(0)
    tm = TM_ROUTE

    @pl.when(i == 0)
    def _():
        zero_ref[...] = jnp.zeros_like(zero_ref)

        def zero_copy(first_row):
            first = pl.multiple_of(first_row * N_CHUNKS, N_CHUNKS)
            return pltpu.make_async_copy(
                zero_ref, xs_ref.at[pl.ds(first, TB_MOE * N_CHUNKS)], zsem)

        for wait in (False, True):
            for e in range(N_EXPERTS):
                @pl.when(cnt_ref[e, 0] > 0)
                def _():
                    cp = zero_copy(end_ref[e] - TB_MOE)
                    cp.wait() if wait else cp.start()

        def zero_block(blk, carry):
            cp = zero_copy(blk * TB_MOE)
            cp.start()
            cp.wait()
            return carry

        n_blocks = xs_ref.shape[0] // (TB_MOE * N_CHUNKS)
        lax.fori_loop(end_ref[N_EXPERTS - 1] // TB_MOE, n_blocks, zero_block, 0)

    p = lax.broadcasted_iota(jnp.int32, (TOP_K * tm, tm), 0)
    onehot = (p == pos_ref[0:1, :]) | (p == pos_ref[1:2, :])
    sorted_rows = _dot(jnp.where(onehot, 1.0, 0.0).astype(BF16),
                       _from_tiles(h_ref, tm).astype(BF16))

    slot = i % 2
    buf = sorted_ref.at[slot]

    @pl.when(i >= 2)
    def _():
        _wait_run_copies(buf, xs_ref, sem.at[slot])

    _to_tiles(buf, sorted_rows, TOP_K * tm)
    _start_run_copies(runs_ref, start_ref, buf, xs_ref, sem.at[slot], to_rows=True)

    @pl.when(i == pl.num_programs(0) - 1)
    def _():
        _wait_run_copies(buf, xs_ref, sem.at[slot])
        _wait_run_copies(sorted_ref.at[1 - slot], xs_ref, sem.at[1 - slot])


def _dispatch(pos, runs, pad_starts, pad_ends, counts, h_tiles, n_rows):
    nt = pos.shape[0]
    tm = TM_ROUTE
    smem_vec = pl.BlockSpec(memory_space=pltpu.SMEM)
    return pl.pallas_call(
        _dispatch_kernel,
        grid=(nt,),
        in_specs=[
            pl.BlockSpec((None, SUBLANES, tm), lambda i: (i, 0, 0)),
            pl.BlockSpec((None, N_EXPERTS, LANES), lambda i: (i, 0, 0), memory_space=pltpu.SMEM),
            smem_vec, smem_vec, smem_vec,
            pl.BlockSpec((tm * N_CHUNKS, LANES), lambda i: (i, 0))],
        out_specs=pl.BlockSpec(memory_space=pl.ANY),
        out_shape=jax.ShapeDtypeStruct((n_rows * N_CHUNKS, LANES), F32),
        scratch_shapes=[
            pltpu.VMEM((TB_MOE * N_CHUNKS, LANES), F32),
            pltpu.VMEM((2, TOP_K * tm * N_CHUNKS, LANES), F32),
            pltpu.SemaphoreType.DMA((2,)),
            pltpu.SemaphoreType.DMA(()),
        ],
        compiler_params=pltpu.CompilerParams(
            dimension_semantics=("arbitrary",), has_side_effects=True,
            vmem_limit_bytes=VMEM_LIMIT),
        name="dispatch",
    )(pos, runs, pad_starts, pad_ends, counts, h_tiles)


def _experts_kernel(be_ref, nused_ref, valid_ref, xs_ref, wg_ref, wu_ref, wd_ref, ys_ref,
                    wg_bf, wu_bf, wd_bf):
    i = pl.program_id(0)
    nused = nused_ref[0]
    half = TB_MOE // 2

    def ffn(nrows):
        x = _from_tiles(xs_ref, nrows).astype(BF16)
        width = wg_bf.shape[1] // HIDDEN_SLICES
        cols = [slice(k * width, (k + 1) * width) for k in range(HIDDEN_SLICES)]
        acts = []
        for c in cols:
            hg = _dot(x, wg_bf[:, c])
            acts.append((hg * jax.nn.sigmoid(hg) * _dot(x, wu_bf[:, c])).astype(BF16))
        y = functools.reduce(jnp.add, [_dot(act, wd_bf[c, :]) for act, c in zip(acts, cols)])
        _to_tiles(ys_ref, y, nrows)

    @pl.when(i < nused)
    def _():
        prev = be_ref[jnp.maximum(i - 1, 0)]

        @pl.when((i == 0) | (be_ref[i] != prev))
        def _():
            wg_bf[...] = wg_ref[...].astype(BF16)
            wu_bf[...] = wu_ref[...].astype(BF16)
            wd_bf[...] = wd_ref[...].astype(BF16)

        @pl.when(valid_ref[i] > half)
        def _():
            ffn(TB_MOE)

        @pl.when(valid_ref[i] <= half)
        def _():
            ffn(half)
            padding = ys_ref.at[pl.ds(half * N_CHUNKS, half * N_CHUNKS)]
            padding[...] = jnp.zeros_like(padding)

    @pl.when(i >= nused)
    def _():
        ys_ref[...] = jnp.zeros_like(ys_ref)


def _experts(block_expert, nused, block_valid, xs, w_gate, w_up, w_down, layer):
    n_blocks = xs.shape[0] // (TB_MOE * N_CHUNKS)
    d, de = w_gate.shape[2], w_gate.shape[3]

    def x_map(i, be, nu, bv):
        return (jnp.minimum(i, nu[0] - 1), 0)

    def w_map(i, be, nu, bv):
        return (layer, be[jnp.minimum(i, nu[0] - 1)], 0, 0)

    return pl.pallas_call(
        _experts_kernel,
        grid_spec=pltpu.PrefetchScalarGridSpec(
            num_scalar_prefetch=3,
            grid=(n_blocks,),
            in_specs=[
                pl.BlockSpec((TB_MOE * N_CHUNKS, LANES), x_map),
                pl.BlockSpec((None, None, d, de), w_map),
                pl.BlockSpec((None, None, d, de), w_map),
                pl.BlockSpec((None, None, de, d), w_map),
            ],
            out_specs=pl.BlockSpec((TB_MOE * N_CHUNKS, LANES), lambda i, be, nu, bv: (i, 0)),
            scratch_shapes=[
                pltpu.VMEM((d, de), BF16),
                pltpu.VMEM((d, de), BF16),
                pltpu.VMEM((de, d), BF16),
            ],
        ),
        out_shape=jax.ShapeDtypeStruct(xs.shape, F32),
        compiler_params=pltpu.CompilerParams(
            dimension_semantics=("arbitrary",), vmem_limit_bytes=VMEM_LIMIT),
        name="experts",
    )(block_expert, nused, block_valid, xs, w_gate, w_up, w_down)


def _combine_kernel(runs_ref, next_runs_ref, start_ref, h_ref, gate_ref, g2_ref, b2_ref, ys_ref,
                    out_ref, sorted_ref, sem, *, chunk_major):
    tm = TM_ROUTE
    i = pl.program_id(0)
    slot = i % 2

    @pl.when(i == 0)
    def _():
        _start_run_copies(runs_ref, start_ref, sorted_ref.at[0], ys_ref, sem.at[0], to_rows=False)

    @pl.when(i + 1 < pl.num_programs(0))
    def _():
        _start_run_copies(next_runs_ref, start_ref, sorted_ref.at[1 - slot], ys_ref,
                          sem.at[1 - slot], to_rows=False)

    _wait_run_copies(sorted_ref.at[slot], ys_ref, sem.at[slot])

    gate = gate_ref[...]
    p = lax.broadcasted_iota(jnp.int32, (tm, TOP_K * tm), 1)
    onehot = jnp.concatenate([p == gate[:, 2:3].astype(jnp.int32),
                              p == gate[:, 3:4].astype(jnp.int32)], 0)
    picked = _dot(jnp.where(onehot, 1.0, 0.0).astype(BF16),
                  _from_tiles(sorted_ref.at[slot], TOP_K * tm).astype(BF16))
    ffn = picked[:tm] * gate[:, 0:1] + picked[tm:] * gate[:, 1:2]
    out = _layer_norm(ALPHA * _from_tiles(h_ref, tm) + ffn, g2_ref[...], b2_ref[...])
    if chunk_major:
        for c in range(N_CHUNKS):
            out_ref[c] = out[:, c * LANES:(c + 1) * LANES]
    else:
        out_ref[...] = out


def _combine(runs, pad_starts, h_tiles, gate, g2, b2, ys, batch, seq, chunk_major):
    tm = TM_ROUTE
    t = batch * seq
    nt = t // tm
    nts = seq // tm
    if chunk_major:
        out_spec = pl.BlockSpec((None, N_CHUNKS, tm, LANES), lambda i: (i // nts, 0, i % nts, 0))
        out_shape = jax.ShapeDtypeStruct((batch, N_CHUNKS, seq, LANES), F32)
    else:
        out_spec = pl.BlockSpec((None, tm, D_MODEL), lambda i: (i // nts, i % nts, 0))
        out_shape = jax.ShapeDtypeStruct((batch, seq, D_MODEL), F32)
    return pl.pallas_call(
        functools.partial(_combine_kernel, chunk_major=chunk_major),
        grid=(nt,),
        in_specs=[
            pl.BlockSpec((None, N_EXPERTS, LANES), lambda i: (i, 0, 0), memory_space=pltpu.SMEM),
            pl.BlockSpec((None, N_EXPERTS, LANES), lambda i: (jnp.minimum(i + 1, nt - 1), 0, 0),
                         memory_space=pltpu.SMEM),
            pl.BlockSpec(memory_space=pltpu.SMEM),
            pl.BlockSpec((tm * N_CHUNKS, LANES), lambda i: (i, 0)),
            pl.BlockSpec((tm, 2 * TOP_K), lambda i: (i, 0)),
            pl.BlockSpec((1, D_MODEL), lambda i: (0, 0)),
            pl.BlockSpec((1, D_MODEL), lambda i: (0, 0)),
            pl.BlockSpec(memory_space=pl.ANY),
        ],
        out_specs=out_spec,
        out_shape=out_shape,
        scratch_shapes=[
            pltpu.VMEM((2, TOP_K * tm * N_CHUNKS, LANES), F32),
            pltpu.SemaphoreType.DMA((2,)),
        ],
        compiler_params=pltpu.CompilerParams(
            dimension_semantics=("arbitrary",), vmem_limit_bytes=VMEM_LIMIT),
        name="combine",
    )(runs, runs, pad_starts, h_tiles, gate, g2, b2, ys)


def _moe(h_tiles, batch, seq, w_router, b_router, w_gate, w_up, w_down, g2, b2, layer,
         chunk_major):
    t = batch * seq
    pos, gate, runs, counts = _router(h_tiles, w_router, b_router)
    cnt = counts[:, 0]
    padded = (cnt + TB_MOE - 1) // TB_MOE * TB_MOE
    pad_ends = jnp.cumsum(padded).astype(jnp.int32)
    pad_starts = pad_ends - padded
    n_rows = t * TOP_K + N_EXPERTS * TB_MOE
    n_blocks = n_rows // TB_MOE
    block_start = jnp.arange(n_blocks, dtype=jnp.int32) * TB_MOE
    block_expert = jnp.minimum(
        jnp.sum(block_start[:, None] >= pad_ends[None, :], -1), N_EXPERTS - 1).astype(jnp.int32)
    nused = (pad_ends[-1:] // TB_MOE).astype(jnp.int32)
    block_valid = jnp.clip((pad_starts + cnt)[block_expert] - block_start, 0, TB_MOE)
    xs = _dispatch(pos, runs, pad_starts, pad_ends, counts, h_tiles, n_rows)
    ys = _experts(block_expert, nused, block_valid.astype(jnp.int32), xs, w_gate, w_up, w_down,
                  layer)
    return _combine(runs, pad_starts, h_tiles, gate, g2, b2, ys, batch, seq, chunk_major)


def _residue_rows(x_ref, dilation):
    n = x_ref.shape[1]
    if dilation == 1:
        return _from_chunks(x_ref)
    per = n // dilation
    return jnp.concatenate(
        [jnp.concatenate([x_ref[c, pl.ds(r, per, stride=dilation), :] for r in range(dilation)], 0)
         for c in range(N_CHUNKS)], -1)


def _proj_b_kernel(x_ref, wqkv_ref, wqm_ref, qkv0_ref, qkv1_ref, qkv2_ref, qm_ref):
    ts = x_ref.shape[1]
    qscale = jnp.where(
        lax.broadcasted_iota(jnp.int32, (1, 3 * DIL_OUT_WIDTH), 1) < DIL_OUT_WIDTH, ATT_SCALE, 1.0)
    for g, (out_ref, (_, dilation)) in enumerate(zip((qkv0_ref, qkv1_ref, qkv2_ref), DIL_PAIRS)):
        xb = _residue_rows(x_ref, dilation).astype(BF16)
        qkv = (_dot(xb, wqkv_ref[g]) * qscale).astype(BF16)
        per = ts // dilation
        for r in range(dilation):
            out_ref[r] = qkv[r * per:(r + 1) * per]
    qm = _dot(_from_chunks(x_ref).astype(BF16), wqm_ref[...])
    qm_ref[...] = (qm * ATT_SCALE).astype(BF16)


def _proj_b(x_cm, wqkv, wqm):
    b, _, s, _ = x_cm.shape
    ts = TS_MIX
    width = 3 * DIL_OUT_WIDTH
    out_specs, out_shapes = [], []
    for _, dilation in DIL_PAIRS:
        out_specs.append(pl.BlockSpec((None, dilation, ts // dilation, width),
                                      lambda i, j: (i, 0, j, 0)))
        out_shapes.append(jax.ShapeDtypeStruct((b, dilation, s // dilation, width), BF16))
    out_specs.append(pl.BlockSpec((None, ts, MEM_WIDTH), lambda i, j: (i, j, 0)))
    out_shapes.append(jax.ShapeDtypeStruct((b, s, MEM_WIDTH), BF16))
    return pl.pallas_call(
        _proj_b_kernel,
        grid=(b, s // ts),
        in_specs=[
            pl.BlockSpec((None, N_CHUNKS, ts, LANES), lambda i, j: (i, 0, j, 0)),
            pl.BlockSpec(wqkv.shape, lambda i, j: (0, 0, 0)),
            pl.BlockSpec(wqm.shape, lambda i, j: (0, 0)),
        ],
        out_specs=out_specs,
        out_shape=out_shapes,
        compiler_params=pltpu.CompilerParams(vmem_limit_bytes=VMEM_LIMIT),
        name="proj_b",
    )(x_cm, wqkv, wqm)


def _dil_attn_kernel(qkv_ref, o_ref, lse_ref, *, dilation, slopes):
    nb = qkv_ref.shape[1] // CHUNK
    width = DIL_OUT_WIDTH
    heads = HEADS_PER_DIL
    row = lax.broadcasted_iota(jnp.int32, (heads * CHUNK, 2 * CHUNK), 0)
    jk = lax.broadcasted_iota(jnp.int32, (heads * CHUNK, 2 * CHUNK), 1)
    rel = CHUNK + row % CHUNK - jk
    slope = functools.reduce(
        lambda acc, h: jnp.where(row // CHUNK == h, slopes[h], acc), range(1, heads), slopes[0])
    bias = jnp.where((rel >= 0) & (rel <= CHUNK), -slope * (dilation * rel).astype(F32), -jnp.inf)
    before_start = jk < CHUNK
    head = lax.broadcasted_iota(jnp.int32, (1, width), 1) // HEAD_DIM

    def scores(idx):
        r = idx // nb
        jb = idx % nb
        cur = pl.ds(pl.multiple_of(jb * CHUNK, CHUNK), CHUNK)
        prev = pl.ds(pl.multiple_of(jnp.maximum(jb - 1, 0) * CHUNK, CHUNK), CHUNK)
        q = qkv_ref[r, cur, 0:width]
        kband = jnp.concatenate([qkv_ref[r, prev, width:2 * width],
                                 qkv_ref[r, cur, width:2 * width]], 0)
        stacked = jnp.concatenate(
            [jnp.where(head == h, q, jnp.zeros_like(q)) for h in range(heads)], 0)
        s = _dot_nt(stacked, kband) + bias
        return r, cur, prev, jnp.where(before_start & (jb == 0), -jnp.inf, s)

    def softmax(s):
        m = jnp.max(s, -1, keepdims=True)
        p = jnp.exp(s - m)
        l = jnp.sum(p, -1, keepdims=True)
        return (p * (1.0 / l)).astype(BF16), m + jnp.log(l)

    def finish(r, cur, prev, p, row_lse):
        vband = jnp.concatenate([qkv_ref[r, prev, 2 * width:3 * width],
                                 qkv_ref[r, cur, 2 * width:3 * width]], 0)
        o = _dot(p, vband)
        out = jnp.zeros((CHUNK, width), F32)
        lse = jnp.zeros((CHUNK, width), F32)
        for h in range(heads):
            rows = slice(h * CHUNK, (h + 1) * CHUNK)
            out = jnp.where(head == h, o[rows], out)
            lse = jnp.where(head == h, row_lse[rows], lse)
        o_ref[r, cur, :] = out
        lse_ref[r, cur, :] = lse

    def group(i, carry):
        blocks = [scores(DIL_INTERLEAVE * i + j) for j in range(DIL_INTERLEAVE)]
        probs = [softmax(s) for _, _, _, s in blocks]
        for (r, cur, prev, _), (p, row_lse) in zip(blocks, probs):
            finish(r, cur, prev, p, row_lse)
        return carry

    lax.fori_loop(0, dilation * nb // DIL_INTERLEAVE, group, 0)


def _dil_attn(qkv, group):
    b, dilation, length, _ = qkv.shape
    slopes = tuple(2.0 ** (-8.0 * (group * HEADS_PER_DIL + h + 1) / N_DIL_HEADS)
                   for h in range(HEADS_PER_DIL))
    out_spec = pl.BlockSpec((None, dilation, length, DIL_OUT_WIDTH), lambda i: (i, 0, 0, 0))
    out_shape = jax.ShapeDtypeStruct((b, dilation, length, DIL_OUT_WIDTH), F32)
    return pl.pallas_call(
        functools.partial(_dil_attn_kernel, dilation=dilation, slopes=slopes),
        grid=(b,),
        in_specs=[pl.BlockSpec((None, dilation, length, 3 * DIL_OUT_WIDTH),
                               lambda i: (i, 0, 0, 0))],
        out_specs=[out_spec, out_spec],
        out_shape=[out_shape, out_shape],
        compiler_params=pltpu.CompilerParams(vmem_limit_bytes=VMEM_LIMIT),
        name=f"dil_attn_{group}",
    )(qkv)


def _token_order(src_ref, scratch, dilation):
    if dilation == 1:
        return src_ref[0]
    per = src_ref.shape[1]
    halves = DIL_OUT_WIDTH // LANES
    for r in range(dilation):
        v = src_ref[r]
        for c in range(halves):
            scratch[c, pl.ds(r, per, stride=dilation), :] = v[:, c * LANES:(c + 1) * LANES]
    return jnp.concatenate([scratch[c] for c in range(halves)], -1)


def _mixer_b_kernel(x_ref, qm_ref, o0_ref, l0_ref, o1_ref, l1_ref, o2_ref, l2_ref,
                    mk_ref, mv_ref, wout_ref, g1_ref, b1_ref, h_ref,
                    so1, sl1, so2, sl2):
    ts = x_ref.shape[1]
    outs = [o0_ref[0], _token_order(o1_ref, so1, DIL_PAIRS[1][1]),
            _token_order(o2_ref, so2, DIL_PAIRS[2][1])]
    lses = [l0_ref[0], _token_order(l1_ref, sl1, DIL_PAIRS[1][1]),
            _token_order(l2_ref, sl2, DIL_PAIRS[2][1])]
    top = functools.reduce(jnp.maximum, lses)
    es = [jnp.exp(l - top) for l in lses]
    inv = 1.0 / functools.reduce(jnp.add, es)
    mix = functools.reduce(jnp.add, [e * inv * o for e, o in zip(es, outs)])
    mo = _mem_attention(qm_ref[...], mk_ref[...], mv_ref[...])
    att = _dot(jnp.concatenate([mix, mo], -1).astype(BF16), wout_ref[...])
    h = _layer_norm(ALPHA * _from_chunks(x_ref) + att, g1_ref[...], b1_ref[...])
    _to_tiles(h_ref, h, ts)


def _mixer_b(x_cm, qm, attn, mk, mv, w_out, g1, b1):
    b, _, s, _ = x_cm.shape
    ts = TS_MIX
    nts = s // ts
    m = mk.shape[0] // b
    in_specs = [
        pl.BlockSpec((None, N_CHUNKS, ts, LANES), lambda i, j: (i, 0, j, 0)),
        pl.BlockSpec((None, ts, MEM_WIDTH), lambda i, j: (i, j, 0)),
    ]
    args = [x_cm, qm]
    for (o, lse), (_, dilation) in zip(attn, DIL_PAIRS):
        spec = pl.BlockSpec((None, dilation, ts // dilation, DIL_OUT_WIDTH),
                            lambda i, j: (i, 0, j, 0))
        in_specs += [spec, spec]
        args += [o, lse]
    in_specs += [
        pl.BlockSpec((m, MEM_WIDTH), lambda i, j: (i, 0)),
        pl.BlockSpec((m, MEM_WIDTH), lambda i, j: (i, 0)),
        pl.BlockSpec(w_out.shape, lambda i, j: (0, 0)),
        pl.BlockSpec((1, D_MODEL), lambda i, j: (0, 0)),
        pl.BlockSpec((1, D_MODEL), lambda i, j: (0, 0)),
    ]
    args += [mk, mv, w_out, g1, b1]
    halves = DIL_OUT_WIDTH // LANES
    return pl.pallas_call(
        _mixer_b_kernel,
        grid=(b, nts),
        in_specs=in_specs,
        out_specs=pl.BlockSpec((ts * N_CHUNKS, LANES), lambda i, j: (i * nts + j, 0)),
        out_shape=jax.ShapeDtypeStruct((b * s * N_CHUNKS, LANES), F32),
        scratch_shapes=[pltpu.VMEM((halves, ts, LANES), F32)] * 4,
        compiler_params=pltpu.CompilerParams(vmem_limit_bytes=VMEM_LIMIT),
        name="mixer_b",
    )(*args)


def kernel(x, mem, w_in_a, w_out_a, sg_ln_g, sg_ln_b, sg_w, sg_b, w_in_b, w_out_b, w_k_shared,
           w_v_shared, w_mem_k, w_mem_v, ln1_g, ln1_b, ln2_g, ln2_b, w_router, b_router,
           w_gate, w_up, w_down):
    batch, seq, d = x.shape
    mk, mv = _memkv(mem.reshape(-1, d), w_mem_k, w_mem_v)
    row = lambda v: v.reshape(1, -1)

    sg_bias = jnp.repeat(sg_b[0].T, HEAD_DIM, axis=1)
    h = _mixer_a(x, w_in_a[0].astype(BF16), w_out_a[0].astype(BF16), row(sg_ln_g[0]),
                 row(sg_ln_b[0]), sg_w[0], sg_bias, mk[0], mv[0], row(ln1_g[0]), row(ln1_b[0]))
    w_router_pad = jnp.pad(w_router, ((0, 0), (0, LANES - N_EXPERTS)))
    b_router_col = b_router.reshape(N_EXPERTS, 1)
    x_cm = _moe(h, batch, seq, w_router_pad, b_router_col, w_gate, w_up, w_down,
                row(ln2_g[0]), row(ln2_b[0]), layer=0, chunk_major=True)

    wqkv = jnp.stack([
        jnp.concatenate([w[:, g * DIL_OUT_WIDTH:(g + 1) * DIL_OUT_WIDTH]
                         for w in (w_in_b[0], w_k_shared, w_v_shared)], -1)
        for g in range(len(DIL_PAIRS))]).astype(BF16)
    wqm = w_in_b[0][:, DIL_Q_WIDTH:].astype(BF16)
    qkv0, qkv1, qkv2, qm = _proj_b(x_cm, wqkv, wqm)
    attn = [_dil_attn(qkv, g) for g, qkv in enumerate((qkv0, qkv1, qkv2))]
    h = _mixer_b(x_cm, qm, attn, mk[1], mv[1], w_out_b[0].astype(BF16),
                 row(ln1_g[1]), row(ln1_b[1]))
    return _moe(h, batch, seq, w_router_pad, b_router_col, w_gate, w_up, w_down,
                row(ln2_g[1]), row(ln2_b[1]), layer=1, chunk_major=False)
```

```python
import functools
import math

import jax
import jax.numpy as jnp
from jax import lax
from jax.experimental import pallas as pl
from jax.experimental.pallas import tpu as pltpu

D_MODEL = 1024
HEAD_DIM = 64
CHUNK = 128
N_SG = 12
SG_WIDTH = N_SG * HEAD_DIM
DIL_PAIRS = ((128, 1), (512, 4), (2048, 16))
HEADS_PER_DIL = 4
N_DIL_HEADS = HEADS_PER_DIL * len(DIL_PAIRS)
DIL_Q_WIDTH = N_DIL_HEADS * HEAD_DIM
DIL_OUT_WIDTH = HEADS_PER_DIL * HEAD_DIM
MEM_HEADS = 4
MEM_WIDTH = MEM_HEADS * HEAD_DIM
N_EXPERTS = 16
N_EXPERT_GROUPS = 4
EXPERTS_PER_GROUP = N_EXPERTS // N_EXPERT_GROUPS
TOP_K = 2
DEPTH = 2
ALPHA = (2 * DEPTH) ** 0.25
LN_EPS = 1e-5
ATT_SCALE = 1.0 / math.sqrt(HEAD_DIM)

LANES = 128
SUBLANES = 8
N_CHUNKS = D_MODEL // LANES

TS_MIX = 512
MXU_DEPTH = 256
TM_ROUTE = MXU_DEPTH
RUN_BITS = TM_ROUTE.bit_length()
TB_MOE = 512
HIDDEN_SLICES = 2
DIL_INTERLEAVE = 4
VMEM_LIMIT = 56 * 1024 * 1024

F32 = jnp.float32
BF16 = jnp.bfloat16


def _dot(a, b):
    return jnp.dot(a, b, preferred_element_type=F32)


def _dot_nt(a, b):
    return lax.dot_general(a, b, (((1,), (1,)), ((), ())), preferred_element_type=F32)


def _layer_norm(x, g, b):
    mu = jnp.mean(x, -1, keepdims=True)
    xc = x - mu
    var = jnp.mean(xc * xc, -1, keepdims=True)
    return xc * lax.rsqrt(var + LN_EPS) * g + b


def _gelu(x):
    return 0.5 * x * (1.0 + lax.erf(x * (1.0 / math.sqrt(2.0))))


def _from_tiles(ref, n):
    return jnp.concatenate(
        [ref[pl.ds(c, n, stride=N_CHUNKS), :] for c in range(N_CHUNKS)], -1)


def _to_tiles(ref, val, n):
    for c in range(N_CHUNKS):
        ref[pl.ds(c, n, stride=N_CHUNKS), :] = val[:, c * LANES:(c + 1) * LANES]


def _from_chunks(ref):
    return jnp.concatenate([ref[c] for c in range(N_CHUNKS)], -1)


def _to_chunks(ref, val):
    for c in range(N_CHUNKS):
        ref[c] = val[:, c * LANES:(c + 1) * LANES]


def _chunk_tile_spec(rows, tiles_per_seq):
    return pl.BlockSpec((None, N_CHUNKS, rows, LANES),
                        lambda i: (i // tiles_per_seq, 0, i % tiles_per_seq, 0))


def _mem_attention(q, mk, mv):
    n = q.shape[0]
    head = lax.broadcasted_iota(jnp.int32, (1, MEM_WIDTH), 1) // HEAD_DIM
    stacked = jnp.concatenate(
        [jnp.where(head == h, q, jnp.zeros_like(q)) for h in range(MEM_HEADS)], 0)
    s = _dot_nt(stacked, mk)
    m = jnp.max(s, -1, keepdims=True)
    p = jnp.exp(s - m)
    l = jnp.sum(p, -1, keepdims=True)
    o = _dot((p * (1.0 / l)).astype(BF16), mv)
    out = jnp.zeros((n, MEM_WIDTH), F32)
    for h in range(MEM_HEADS):
        out = jnp.where(head == h, o[h * n:(h + 1) * n], out)
    return out


def _memkv_kernel(mem_ref, wk_ref, wv_ref, mk_ref, mv_ref):
    m = mem_ref[...].astype(BF16)
    mk_ref[...] = _dot(m, wk_ref[...].astype(BF16)).astype(BF16)
    mv_ref[...] = _dot(m, wv_ref[...].astype(BF16)).astype(BF16)


def _memkv(mem2d, w_mem_k, w_mem_v):
    n = mem2d.shape[0]
    return pl.pallas_call(
        _memkv_kernel,
        grid=(DEPTH,),
        in_specs=[
            pl.BlockSpec((n, D_MODEL), lambda l: (0, 0)),
            pl.BlockSpec((None, D_MODEL, MEM_WIDTH), lambda l: (l, 0, 0)),
            pl.BlockSpec((None, D_MODEL, MEM_WIDTH), lambda l: (l, 0, 0)),
        ],
        out_specs=[
            pl.BlockSpec((None, n, MEM_WIDTH), lambda l: (l, 0, 0)),
            pl.BlockSpec((None, n, MEM_WIDTH), lambda l: (l, 0, 0)),
        ],
        out_shape=[jax.ShapeDtypeStruct((DEPTH, n, MEM_WIDTH), BF16)] * 2,
        compiler_params=pltpu.CompilerParams(vmem_limit_bytes=VMEM_LIMIT),
        name="memkv",
    )(mem2d, w_mem_k, w_mem_v)


def _mixer_a_kernel(x_ref, win_ref, wout_ref, sgg_ref, sgb_ref, sgw_ref, sgbias_ref,
                    mk_ref, mv_ref, g1_ref, b1_ref, h_ref):
    ts = x_ref.shape[0]
    x = x_ref[...]
    proj = _dot(x.astype(BF16), win_ref[...])
    u = _gelu(proj[:, :SG_WIDTH])
    gv = _gelu(proj[:, SG_WIDTH:2 * SG_WIDTH])
    gv = _layer_norm(gv, sgg_ref[...], sgb_ref[...]).astype(BF16)

    row = lax.broadcasted_iota(jnp.int32, (CHUNK, CHUNK), 0)
    col = lax.broadcasted_iota(jnp.int32, (CHUNK, CHUNK), 1)
    ws = [jnp.where(row >= col, sgw_ref[g], 0.0).astype(BF16) for g in range(N_SG)]
    n_chunks = ts // CHUNK
    low_half = (lax.broadcasted_iota(jnp.int32, (CHUNK, n_chunks * LANES), 1) % LANES) < HEAD_DIM
    gated = []
    for j in range(N_SG // 2):
        slabs = jnp.concatenate(
            [gv[c * CHUNK:(c + 1) * CHUNK, j * LANES:(j + 1) * LANES] for c in range(n_chunks)], -1)
        both = _dot(jnp.concatenate([ws[2 * j], ws[2 * j + 1]], 0), slabs)
        gated.append(jnp.where(low_half, both[:CHUNK], both[CHUNK:]))
    rows = [jnp.concatenate([g[:, c * LANES:(c + 1) * LANES] for g in gated], -1)
            + sgbias_ref[...] for c in range(n_chunks)]
    mix = u * jnp.concatenate(rows, 0)

    qm = (proj[:, 2 * SG_WIDTH:] * ATT_SCALE).astype(BF16)
    mo = _mem_attention(qm, mk_ref[...], mv_ref[...])
    att = _dot(jnp.concatenate([mix, mo], -1).astype(BF16), wout_ref[...])
    _to_chunks(h_ref, _layer_norm(ALPHA * x + att, g1_ref[...], b1_ref[...]))


def _mixer_a(x, w_in, w_out, sg_g, sg_b, sg_w, sg_bias, mk, mv, g1, b1):
    b, s, d = x.shape
    nts = s // TS_MIX
    m = mk.shape[0] // b
    full = lambda *shape: pl.BlockSpec(shape, lambda i, j: (0,) * len(shape))
    return pl.pallas_call(
        _mixer_a_kernel,
        grid=(b, nts),
        in_specs=[
            pl.BlockSpec((None, TS_MIX, d), lambda i, j: (i, j, 0)),
            full(d, 2 * SG_WIDTH + MEM_WIDTH),
            full(SG_WIDTH + MEM_WIDTH, d),
            full(1, SG_WIDTH),
            full(1, SG_WIDTH),
            full(N_SG, CHUNK, CHUNK),
            full(CHUNK, SG_WIDTH),
            pl.BlockSpec((m, MEM_WIDTH), lambda i, j: (i, 0)),
            pl.BlockSpec((m, MEM_WIDTH), lambda i, j: (i, 0)),
            full(1, d),
            full(1, d),
        ],
        out_specs=pl.BlockSpec((None, N_CHUNKS, TS_MIX, LANES), lambda i, j: (i, 0, j, 0)),
        out_shape=jax.ShapeDtypeStruct((b, N_CHUNKS, s, LANES), F32),
        compiler_params=pltpu.CompilerParams(vmem_limit_bytes=VMEM_LIMIT),
        name="mixer_a",
    )(x, w_in, w_out, sg_g, sg_b, sg_w, sg_bias, mk, mv, g1, b1)


def _router_kernel(h_ref, wr_ref, br_ref, pos_ref, gate_ref, runs_ref, counts_ref, base_ref,
                   wsplit_ref):
    tm = TM_ROUTE

    @pl.when(pl.program_id(0) == 0)
    def _():
        base_ref[...] = jnp.zeros_like(base_ref)
        w = wr_ref[...]
        w_hi = w.astype(BF16)
        wsplit_ref[...] = jnp.concatenate([w_hi, (w - w_hi.astype(F32)).astype(BF16)], -1)

    h = _from_chunks(h_ref)
    h_hi = h.astype(BF16)
    h_lo = (h - h_hi.astype(F32)).astype(BF16)
    prod = _dot(jnp.concatenate([h_hi, h_lo], 0), wsplit_ref[...])
    logits = prod[:tm, :LANES] + (prod[:tm, LANES:] + prod[tm:, :LANES])
    lt = logits.T[:N_EXPERTS]

    ex = jnp.exp(lt - jnp.max(lt, 0, keepdims=True))
    probs = ex / jnp.sum(ex, 0, keepdims=True)
    sel = probs + br_ref[...]
    eid = lax.broadcasted_iota(jnp.int32, (N_EXPERTS, tm), 0)
    group = eid // EXPERTS_PER_GROUP

    def top2(mask):
        v = jnp.where(mask, sel, -jnp.inf)
        m1 = jnp.max(v, 0, keepdims=True)
        i1 = jnp.min(jnp.where(v == m1, eid, N_EXPERTS), 0, keepdims=True)
        v2 = jnp.where(eid == i1, -jnp.inf, v)
        m2 = jnp.max(v2, 0, keepdims=True)
        i2 = jnp.min(jnp.where(v2 == m2, eid, N_EXPERTS), 0, keepdims=True)
        return m1, i1, m2, i2

    scores = []
    for g in range(N_EXPERT_GROUPS):
        m1, _, m2, _ = top2(group == g)
        scores.append(m1 + m2)
    best = functools.reduce(jnp.maximum, scores)
    g_idx = jnp.full((1, tm), N_EXPERT_GROUPS - 1, jnp.int32)
    for g in reversed(range(N_EXPERT_GROUPS - 1)):
        g_idx = jnp.where(scores[g] == best, g, g_idx)
    _, e0, _, e1 = top2(group == g_idx)

    hot0 = eid == e0
    hot1 = eid == e1
    p0 = jnp.sum(jnp.where(hot0, probs, 0.0), 0, keepdims=True)
    p1 = jnp.sum(jnp.where(hot1, probs, 0.0), 0, keepdims=True)
    psum = p0 + p1

    hot = jnp.where(hot0 | hot1, 1.0, 0.0)
    earlier = (lax.broadcasted_iota(jnp.int32, (tm, tm), 0)
               < lax.broadcasted_iota(jnp.int32, (tm, tm), 1))
    hot_bf = hot.astype(BF16)
    before = _dot(hot_bf, jnp.where(earlier, 1.0, 0.0).astype(BF16))
    below = (lax.broadcasted_iota(jnp.int32, (N_EXPERTS, N_EXPERTS), 1)
             < lax.broadcasted_iota(jnp.int32, (N_EXPERTS, N_EXPERTS), 0))
    run_start = jnp.sum(_dot(jnp.where(below, 1.0, 0.0).astype(BF16), hot_bf), 1, keepdims=True)
    local = run_start + before
    q0 = jnp.sum(jnp.where(hot0, local, 0.0), 0, keepdims=True)
    q1 = jnp.sum(jnp.where(hot1, local, 0.0), 0, keepdims=True)

    row = lax.broadcasted_iota(jnp.int32, (SUBLANES, tm), 0)
    pos_ref[...] = jnp.where(row == 0, q0, jnp.where(row == 1, q1, 0.0)).astype(jnp.int32)
    grow = lax.broadcasted_iota(jnp.int32, (LANES, tm), 0)
    cols = jnp.where(grow == 0, p0 / psum, jnp.where(grow == 1, p1 / psum, jnp.where(
        grow == 2, q0, jnp.where(grow == 3, q1, 0.0))))
    gate_ref[...] = cols.T[:, :2 * TOP_K]
    count = jnp.sum(hot, 1, keepdims=True)
    lane = lax.broadcasted_iota(jnp.int32, (N_EXPERTS, LANES), 1)
    runs_ref[...] = jnp.where(lane == 0, count, jnp.where(
        lane == 1, base_ref[...], 0.0)).astype(jnp.int32)
    total = base_ref[...] + count
    base_ref[...] = total
    counts_ref[...] = total.astype(jnp.int32)


def _router(h_cm, w_router, b_router):
    batch, _, seq, _ = h_cm.shape
    t = batch * seq
    tm = TM_ROUTE
    return pl.pallas_call(
        _router_kernel,
        grid=(t // tm,),
        in_specs=[
            _chunk_tile_spec(tm, seq // tm),
            pl.BlockSpec((D_MODEL, LANES), lambda i: (0, 0)),
            pl.BlockSpec((N_EXPERTS, 1), lambda i: (0, 0)),
        ],
        out_specs=[
            pl.BlockSpec((None, SUBLANES, tm), lambda i: (i, 0, 0)),
            pl.BlockSpec((tm, 2 * TOP_K), lambda i: (i, 0)),
            pl.BlockSpec((None, N_EXPERTS, LANES), lambda i: (i, 0, 0)),
            pl.BlockSpec((N_EXPERTS, 1), lambda i: (0, 0)),
        ],
        out_shape=[
            jax.ShapeDtypeStruct((t // tm, SUBLANES, tm), jnp.int32),
            jax.ShapeDtypeStruct((t, 2 * TOP_K), F32),
            jax.ShapeDtypeStruct((t // tm, N_EXPERTS, LANES), jnp.int32),
            jax.ShapeDtypeStruct((N_EXPERTS, 1), jnp.int32),
        ],
        scratch_shapes=[pltpu.VMEM((N_EXPERTS, 1), F32), pltpu.VMEM((D_MODEL, 2 * LANES), BF16)],
        compiler_params=pltpu.CompilerParams(
            dimension_semantics=("arbitrary",), vmem_limit_bytes=VMEM_LIMIT),
        name="router",
    )(h_cm, w_router, b_router)


def _start_run_copies(runs_ref, start_ref, tile_ref, rows_ref, sem, to_rows):
    offset = 0
    for e in range(N_EXPERTS):
        count = runs_ref[e, 0]
        first_row = start_ref[e] + runs_ref[e, 1]
        for bit in reversed(range(RUN_BITS)):
            size = (1 << bit) * N_CHUNKS
            done = (count >> (bit + 1)) << (bit + 1)

            @pl.when(((count >> bit) & 1) == 1)
            def _():
                in_tile = tile_ref.at[
                    pl.ds(pl.multiple_of((offset + done) * N_CHUNKS, N_CHUNKS), size)]
                in_rows = rows_ref.at[
                    pl.ds(pl.multiple_of((first_row + done) * N_CHUNKS, N_CHUNKS), size)]
                if to_rows:
                    pltpu.make_async_copy(in_tile, in_rows, sem).start()
                else:
                    pltpu.make_async_copy(in_rows, in_tile, sem).start()
        offset = offset + count


def _wait_run_copies(tile_ref, rows_ref, sem):
    pltpu.make_async_copy(tile_ref, rows_ref.at[pl.ds(0, tile_ref.shape[0])], sem).wait()


def _dispatch_kernel(pos_ref, runs_ref, start_ref, end_ref, cnt_ref, h_ref, xs_ref,
                     zero_ref, sorted_ref, sem, zsem):
    i = pl.program_id(0)
    tm = TM_ROUTE

    @pl.when(i == 0)
    def _():
        zero_ref[...] = jnp.zeros_like(zero_ref)

        def zero_copy(first_row):
            first = pl.multiple_of(first_row * N_CHUNKS, N_CHUNKS)
            return pltpu.make_async_copy(
                zero_ref, xs_ref.at[pl.ds(first, TB_MOE * N_CHUNKS)], zsem)

        for wait in (False, True):
            for e in range(N_EXPERTS):
                @pl.when(cnt_ref[e, 0] > 0)
                def _():
                    cp = zero_copy(end_ref[e] - TB_MOE)
                    cp.wait() if wait else cp.start()

        def zero_block(blk, carry):
            cp = zero_copy(blk * TB_MOE)
            cp.start()
            cp.wait()
            return carry

        n_blocks = xs_ref.shape[0] // (TB_MOE * N_CHUNKS)
        lax.fori_loop(end_ref[N_EXPERTS - 1] // TB_MOE, n_blocks, zero_block, 0)

    p = lax.broadcasted_iota(jnp.int32, (TOP_K * tm, tm), 0)
    onehot = (p == pos_ref[0:1, :]) | (p == pos_ref[1:2, :])
    sorted_rows = _dot(jnp.where(onehot, 1.0, 0.0).astype(BF16),
                       _from_chunks(h_ref).astype(BF16))

    slot = i % 2
    buf = sorted_ref.at[slot]

    @pl.when(i >= 2)
    def _():
        _wait_run_copies(buf, xs_ref, sem.at[slot])

    _to_tiles(buf, sorted_rows, TOP_K * tm)
    _start_run_copies(runs_ref, start_ref, buf, xs_ref, sem.at[slot], to_rows=True)

    @pl.when(i == pl.num_programs(0) - 1)
    def _():
        _wait_run_copies(buf, xs_ref, sem.at[slot])
        _wait_run_copies(sorted_ref.at[1 - slot], xs_ref, sem.at[1 - slot])


def _dispatch(pos, runs, pad_starts, pad_ends, counts, h_cm, n_rows):
    nt = pos.shape[0]
    tm = TM_ROUTE
    seq = h_cm.shape[2]
    smem_vec = pl.BlockSpec(memory_space=pltpu.SMEM)
    return pl.pallas_call(
        _dispatch_kernel,
        grid=(nt,),
        in_specs=[
            pl.BlockSpec((None, SUBLANES, tm), lambda i: (i, 0, 0)),
            pl.BlockSpec((None, N_EXPERTS, LANES), lambda i: (i, 0, 0), memory_space=pltpu.SMEM),
            smem_vec, smem_vec, smem_vec,
            _chunk_tile_spec(tm, seq // tm)],
        out_specs=pl.BlockSpec(memory_space=pl.ANY),
        out_shape=jax.ShapeDtypeStruct((n_rows * N_CHUNKS, LANES), F32),
        scratch_shapes=[
            pltpu.VMEM((TB_MOE * N_CHUNKS, LANES), F32),
            pltpu.VMEM((2, TOP_K * tm * N_CHUNKS, LANES), F32),
            pltpu.SemaphoreType.DMA((2,)),
            pltpu.SemaphoreType.DMA(()),
        ],
        compiler_params=pltpu.CompilerParams(
            dimension_semantics=("arbitrary",), has_side_effects=True,
            vmem_limit_bytes=VMEM_LIMIT),
        name="dispatch",
    )(pos, runs, pad_starts, pad_ends, counts, h_cm)


def _experts_kernel(be_ref, nused_ref, valid_ref, xs_ref, wg_ref, wu_ref, wd_ref, ys_ref,
                    wg_bf, wu_bf, wd_bf):
    i = pl.program_id(0)
    nused = nused_ref[0]
    half = TB_MOE // 2

    def ffn(nrows):
        x = _from_tiles(xs_ref, nrows).astype(BF16)
        width = wg_bf.shape[1] // HIDDEN_SLICES
        cols = [slice(k * width, (k + 1) * width) for k in range(HIDDEN_SLICES)]
        acts = []
        for c in cols:
            hg = _dot(x, wg_bf[:, c])
            acts.append((hg * jax.nn.sigmoid(hg) * _dot(x, wu_bf[:, c])).astype(BF16))
        y = functools.reduce(jnp.add, [_dot(act, wd_bf[c, :]) for act, c in zip(acts, cols)])
        _to_tiles(ys_ref, y, nrows)

    @pl.when(i < nused)
    def _():
        prev = be_ref[jnp.maximum(i - 1, 0)]

        @pl.when((i == 0) | (be_ref[i] != prev))
        def _():
            wg_bf[...] = wg_ref[...].astype(BF16)
            wu_bf[...] = wu_ref[...].astype(BF16)
            wd_bf[...] = wd_ref[...].astype(BF16)

        @pl.when(valid_ref[i] > half)
        def _():
            ffn(TB_MOE)

        @pl.when(valid_ref[i] <= half)
        def _():
            ffn(half)
            padding = ys_ref.at[pl.ds(half * N_CHUNKS, half * N_CHUNKS)]
            padding[...] = jnp.zeros_like(padding)

    @pl.when(i >= nused)
    def _():
        ys_ref[...] = jnp.zeros_like(ys_ref)


def _experts(block_expert, nused, block_valid, xs, w_gate, w_up, w_down, layer):
    n_blocks = xs.shape[0] // (TB_MOE * N_CHUNKS)
    d, de = w_gate.shape[2], w_gate.shape[3]

    def x_map(i, be, nu, bv):
        return (jnp.minimum(i, nu[0] - 1), 0)

    def w_map(i, be, nu, bv):
        return (layer, be[jnp.minimum(i, nu[0] - 1)], 0, 0)

    return pl.pallas_call(
        _experts_kernel,
        grid_spec=pltpu.PrefetchScalarGridSpec(
            num_scalar_prefetch=3,
            grid=(n_blocks,),
            in_specs=[
                pl.BlockSpec((TB_MOE * N_CHUNKS, LANES), x_map),
                pl.BlockSpec((None, None, d, de), w_map),
                pl.BlockSpec((None, None, d, de), w_map),
                pl.BlockSpec((None, None, de, d), w_map),
            ],
            out_specs=pl.BlockSpec((TB_MOE * N_CHUNKS, LANES), lambda i, be, nu, bv: (i, 0)),
            scratch_shapes=[
                pltpu.VMEM((d, de), BF16),
                pltpu.VMEM((d, de), BF16),
                pltpu.VMEM((de, d), BF16),
            ],
        ),
        out_shape=jax.ShapeDtypeStruct(xs.shape, F32),
        compiler_params=pltpu.CompilerParams(
            dimension_semantics=("arbitrary",), vmem_limit_bytes=VMEM_LIMIT),
        name="experts",
    )(block_expert, nused, block_valid, xs, w_gate, w_up, w_down)


def _combine_kernel(runs_ref, next_runs_ref, start_ref, h_ref, gate_ref, g2_ref, b2_ref, ys_ref,
                    out_ref, sorted_ref, sem, *, chunk_major):
    tm = TM_ROUTE
    i = pl.program_id(0)
    slot = i % 2

    @pl.when(i == 0)
    def _():
        _start_run_copies(runs_ref, start_ref, sorted_ref.at[0], ys_ref, sem.at[0], to_rows=False)

    @pl.when(i + 1 < pl.num_programs(0))
    def _():
        _start_run_copies(next_runs_ref, start_ref, sorted_ref.at[1 - slot], ys_ref,
                          sem.at[1 - slot], to_rows=False)

    _wait_run_copies(sorted_ref.at[slot], ys_ref, sem.at[slot])

    gate = gate_ref[...]
    p = lax.broadcasted_iota(jnp.int32, (tm, TOP_K * tm), 1)
    onehot = jnp.concatenate([p == gate[:, 2:3].astype(jnp.int32),
                              p == gate[:, 3:4].astype(jnp.int32)], 0)
    picked = _dot(jnp.where(onehot, 1.0, 0.0).astype(BF16),
                  _from_tiles(sorted_ref.at[slot], TOP_K * tm).astype(BF16))
    ffn = picked[:tm] * gate[:, 0:1] + picked[tm:] * gate[:, 1:2]
    out = _layer_norm(ALPHA * _from_chunks(h_ref) + ffn, g2_ref[...], b2_ref[...])
    if chunk_major:
        _to_chunks(out_ref, out)
    else:
        out_ref[...] = out


def _combine(runs, pad_starts, h_cm, gate, g2, b2, ys, chunk_major):
    batch, _, seq, _ = h_cm.shape
    tm = TM_ROUTE
    t = batch * seq
    nt = t // tm
    nts = seq // tm
    if chunk_major:
        out_spec = _chunk_tile_spec(tm, nts)
        out_shape = jax.ShapeDtypeStruct((batch, N_CHUNKS, seq, LANES), F32)
    else:
        out_spec = pl.BlockSpec((None, tm, D_MODEL), lambda i: (i // nts, i % nts, 0))
        out_shape = jax.ShapeDtypeStruct((batch, seq, D_MODEL), F32)
    return pl.pallas_call(
        functools.partial(_combine_kernel, chunk_major=chunk_major),
        grid=(nt,),
        in_specs=[
            pl.BlockSpec((None, N_EXPERTS, LANES), lambda i: (i, 0, 0), memory_space=pltpu.SMEM),
            pl.BlockSpec((None, N_EXPERTS, LANES), lambda i: (jnp.minimum(i + 1, nt - 1), 0, 0),
                         memory_space=pltpu.SMEM),
            pl.BlockSpec(memory_space=pltpu.SMEM),
            _chunk_tile_spec(tm, nts),
            pl.BlockSpec((tm, 2 * TOP_K), lambda i: (i, 0)),
            pl.BlockSpec((1, D_MODEL), lambda i: (0, 0)),
            pl.BlockSpec((1, D_MODEL), lambda i: (0, 0)),
            pl.BlockSpec(memory_space=pl.ANY),
        ],
        out_specs=out_spec,
        out_shape=out_shape,
        scratch_shapes=[
            pltpu.VMEM((2, TOP_K * tm * N_CHUNKS, LANES), F32),
            pltpu.SemaphoreType.DMA((2,)),
        ],
        compiler_params=pltpu.CompilerParams(
            dimension_semantics=("arbitrary",), vmem_limit_bytes=VMEM_LIMIT),
        name="combine",
    )(runs, runs, pad_starts, h_cm, gate, g2, b2, ys)


def _moe(h_cm, w_router, b_router, w_gate, w_up, w_down, g2, b2, layer, chunk_major):
    t = h_cm.shape[0] * h_cm.shape[2]
    pos, gate, runs, counts = _router(h_cm, w_router, b_router)
    cnt = counts[:, 0]
    padded = (cnt + TB_MOE - 1) // TB_MOE * TB_MOE
    pad_ends = jnp.cumsum(padded).astype(jnp.int32)
    pad_starts = pad_ends - padded
    n_rows = t * TOP_K + N_EXPERTS * TB_MOE
    n_blocks = n_rows // TB_MOE
    block_start = jnp.arange(n_blocks, dtype=jnp.int32) * TB_MOE
    block_expert = jnp.minimum(
        jnp.sum(block_start[:, None] >= pad_ends[None, :], -1), N_EXPERTS - 1).astype(jnp.int32)
    nused = (pad_ends[-1:] // TB_MOE).astype(jnp.int32)
    block_valid = jnp.clip((pad_starts + cnt)[block_expert] - block_start, 0, TB_MOE)
    xs = _dispatch(pos, runs, pad_starts, pad_ends, counts, h_cm, n_rows)
    ys = _experts(block_expert, nused, block_valid.astype(jnp.int32), xs, w_gate, w_up, w_down,
                  layer)
    return _combine(runs, pad_starts, h_cm, gate, g2, b2, ys, chunk_major)


def _residue_rows(x_ref, dilation):
    n = x_ref.shape[1]
    if dilation == 1:
        return _from_chunks(x_ref)
    per = n // dilation
    return jnp.concatenate(
        [jnp.concatenate([x_ref[c, pl.ds(r, per, stride=dilation), :] for r in range(dilation)], 0)
         for c in range(N_CHUNKS)], -1)


def _proj_b_kernel(x_ref, wqkv_ref, wqm_ref, qkv0_ref, qkv1_ref, qkv2_ref, qm_ref):
    ts = x_ref.shape[1]
    qscale = jnp.where(
        lax.broadcasted_iota(jnp.int32, (1, 3 * DIL_OUT_WIDTH), 1) < DIL_OUT_WIDTH, ATT_SCALE, 1.0)
    for g, (out_ref, (_, dilation)) in enumerate(zip((qkv0_ref, qkv1_ref, qkv2_ref), DIL_PAIRS)):
        xb = _residue_rows(x_ref, dilation).astype(BF16)
        qkv = (_dot(xb, wqkv_ref[g]) * qscale).astype(BF16)
        per = ts // dilation
        for r in range(dilation):
            out_ref[r] = qkv[r * per:(r + 1) * per]
    qm = _dot(_from_chunks(x_ref).astype(BF16), wqm_ref[...])
    qm_ref[...] = (qm * ATT_SCALE).astype(BF16)


def _proj_b(x_cm, wqkv, wqm):
    b, _, s, _ = x_cm.shape
    ts = TS_MIX
    width = 3 * DIL_OUT_WIDTH
    out_specs, out_shapes = [], []
    for _, dilation in DIL_PAIRS:
        out_specs.append(pl.BlockSpec((None, dilation, ts // dilation, width),
                                      lambda i, j: (i, 0, j, 0)))
        out_shapes.append(jax.ShapeDtypeStruct((b, dilation, s // dilation, width), BF16))
    out_specs.append(pl.BlockSpec((None, ts, MEM_WIDTH), lambda i, j: (i, j, 0)))
    out_shapes.append(jax.ShapeDtypeStruct((b, s, MEM_WIDTH), BF16))
    return pl.pallas_call(
        _proj_b_kernel,
        grid=(b, s // ts),
        in_specs=[
            pl.BlockSpec((None, N_CHUNKS, ts, LANES), lambda i, j: (i, 0, j, 0)),
            pl.BlockSpec(wqkv.shape, lambda i, j: (0, 0, 0)),
            pl.BlockSpec(wqm.shape, lambda i, j: (0, 0)),
        ],
        out_specs=out_specs,
        out_shape=out_shapes,
        compiler_params=pltpu.CompilerParams(vmem_limit_bytes=VMEM_LIMIT),
        name="proj_b",
    )(x_cm, wqkv, wqm)


def _dil_attn_kernel(qkv_ref, o_ref, lse_ref, *, dilation, slopes):
    nb = qkv_ref.shape[1] // CHUNK
    width = DIL_OUT_WIDTH
    heads = HEADS_PER_DIL
    row = lax.broadcasted_iota(jnp.int32, (heads * CHUNK, 2 * CHUNK), 0)
    jk = lax.broadcasted_iota(jnp.int32, (heads * CHUNK, 2 * CHUNK), 1)
    rel = CHUNK + row % CHUNK - jk
    slope = functools.reduce(
        lambda acc, h: jnp.where(row // CHUNK == h, slopes[h], acc), range(1, heads), slopes[0])
    bias = jnp.where((rel >= 0) & (rel <= CHUNK), -slope * (dilation * rel).astype(F32), -jnp.inf)
    before_start = jk < CHUNK
    head = lax.broadcasted_iota(jnp.int32, (1, width), 1) // HEAD_DIM

    def scores(idx):
        r = idx // nb
        jb = idx % nb
        cur = pl.ds(pl.multiple_of(jb * CHUNK, CHUNK), CHUNK)
        prev = pl.ds(pl.multiple_of(jnp.maximum(jb - 1, 0) * CHUNK, CHUNK), CHUNK)
        q = qkv_ref[r, cur, 0:width]
        kband = jnp.concatenate([qkv_ref[r, prev, width:2 * width],
                                 qkv_ref[r, cur, width:2 * width]], 0)
        stacked = jnp.concatenate(
            [jnp.where(head == h, q, jnp.zeros_like(q)) for h in range(heads)], 0)
        s = _dot_nt(stacked, kband) + bias
        return r, cur, prev, jnp.where(before_start & (jb == 0), -jnp.inf, s)

    def softmax(s):
        m = jnp.max(s, -1, keepdims=True)
        p = jnp.exp(s - m)
        l = jnp.sum(p, -1, keepdims=True)
        return (p * (1.0 / l)).astype(BF16), m + jnp.log(l)

    def finish(r, cur, prev, p, row_lse):
        vband = jnp.concatenate([qkv_ref[r, prev, 2 * width:3 * width],
                                 qkv_ref[r, cur, 2 * width:3 * width]], 0)
        o = _dot(p, vband)
        out = jnp.zeros((CHUNK, width), F32)
        lse = jnp.zeros((CHUNK, width), F32)
        for h in range(heads):
            rows = slice(h * CHUNK, (h + 1) * CHUNK)
            out = jnp.where(head == h, o[rows], out)
            lse = jnp.where(head == h, row_lse[rows], lse)
        o_ref[r, cur, :] = out
        lse_ref[r, cur, :] = lse

    def group(i, carry):
        blocks = [scores(DIL_INTERLEAVE * i + j) for j in range(DIL_INTERLEAVE)]
        probs = [softmax(s) for _, _, _, s in blocks]
        for (r, cur, prev, _), (p, row_lse) in zip(blocks, probs):
            finish(r, cur, prev, p, row_lse)
        return carry

    lax.fori_loop(0, dilation * nb // DIL_INTERLEAVE, group, 0)


def _dil_attn(qkv, group):
    b, dilation, length, _ = qkv.shape
    slopes = tuple(2.0 ** (-8.0 * (group * HEADS_PER_DIL + h + 1) / N_DIL_HEADS)
                   for h in range(HEADS_PER_DIL))
    out_spec = pl.BlockSpec((None, dilation, length, DIL_OUT_WIDTH), lambda i: (i, 0, 0, 0))
    out_shape = jax.ShapeDtypeStruct((b, dilation, length, DIL_OUT_WIDTH), F32)
    return pl.pallas_call(
        functools.partial(_dil_attn_kernel, dilation=dilation, slopes=slopes),
        grid=(b,),
        in_specs=[pl.BlockSpec((None, dilation, length, 3 * DIL_OUT_WIDTH),
                               lambda i: (i, 0, 0, 0))],
        out_specs=[out_spec, out_spec],
        out_shape=[out_shape, out_shape],
        compiler_params=pltpu.CompilerParams(vmem_limit_bytes=VMEM_LIMIT),
        name=f"dil_attn_{group}",
    )(qkv)


def _token_order(src_ref, scratch, dilation):
    if dilation == 1:
        return src_ref[0]
    per = src_ref.shape[1]
    halves = DIL_OUT_WIDTH // LANES
    for r in range(dilation):
        v = src_ref[r]
        for c in range(halves):
            scratch[c, pl.ds(r, per, stride=dilation), :] = v[:, c * LANES:(c + 1) * LANES]
    return jnp.concatenate([scratch[c] for c in range(halves)], -1)


def _mixer_b_kernel(x_ref, qm_ref, o0_ref, l0_ref, o1_ref, l1_ref, o2_ref, l2_ref,
                    mk_ref, mv_ref, wout_ref, g1_ref, b1_ref, h_ref,
                    so1, sl1, so2, sl2):
    outs = [o0_ref[0], _token_order(o1_ref, so1, DIL_PAIRS[1][1]),
            _token_order(o2_ref, so2, DIL_PAIRS[2][1])]
    lses = [l0_ref[0], _token_order(l1_ref, sl1, DIL_PAIRS[1][1]),
            _token_order(l2_ref, sl2, DIL_PAIRS[2][1])]
    top = functools.reduce(jnp.maximum, lses)
    es = [jnp.exp(l - top) for l in lses]
    inv = 1.0 / functools.reduce(jnp.add, es)
    mix = functools.reduce(jnp.add, [e * inv * o for e, o in zip(es, outs)])
    mo = _mem_attention(qm_ref[...], mk_ref[...], mv_ref[...])
    att = _dot(jnp.concatenate([mix, mo], -1).astype(BF16), wout_ref[...])
    _to_chunks(h_ref, _layer_norm(ALPHA * _from_chunks(x_ref) + att, g1_ref[...], b1_ref[...]))


def _mixer_b(x_cm, qm, attn, mk, mv, w_out, g1, b1):
    b, _, s, _ = x_cm.shape
    ts = TS_MIX
    nts = s // ts
    m = mk.shape[0] // b
    in_specs = [
        pl.BlockSpec((None, N_CHUNKS, ts, LANES), lambda i, j: (i, 0, j, 0)),
        pl.BlockSpec((None, ts, MEM_WIDTH), lambda i, j: (i, j, 0)),
    ]
    args = [x_cm, qm]
    for (o, lse), (_, dilation) in zip(attn, DIL_PAIRS):
        spec = pl.BlockSpec((None, dilation, ts // dilation, DIL_OUT_WIDTH),
                            lambda i, j: (i, 0, j, 0))
        in_specs += [spec, spec]
        args += [o, lse]
    in_specs += [
        pl.BlockSpec((m, MEM_WIDTH), lambda i, j: (i, 0)),
        pl.BlockSpec((m, MEM_WIDTH), lambda i, j: (i, 0)),
        pl.BlockSpec(w_out.shape, lambda i, j: (0, 0)),
        pl.BlockSpec((1, D_MODEL), lambda i, j: (0, 0)),
        pl.BlockSpec((1, D_MODEL), lambda i, j: (0, 0)),
    ]
    args += [mk, mv, w_out, g1, b1]
    halves = DIL_OUT_WIDTH // LANES
    return pl.pallas_call(
        _mixer_b_kernel,
        grid=(b, nts),
        in_specs=in_specs,
        out_specs=pl.BlockSpec((None, N_CHUNKS, ts, LANES), lambda i, j: (i, 0, j, 0)),
        out_shape=jax.ShapeDtypeStruct((b, N_CHUNKS, s, LANES), F32),
        scratch_shapes=[pltpu.VMEM((halves, ts, LANES), F32)] * 4,
        compiler_params=pltpu.CompilerParams(vmem_limit_bytes=VMEM_LIMIT),
        name="mixer_b",
    )(*args)


def kernel(x, mem, w_in_a, w_out_a, sg_ln_g, sg_ln_b, sg_w, sg_b, w_in_b, w_out_b, w_k_shared,
           w_v_shared, w_mem_k, w_mem_v, ln1_g, ln1_b, ln2_g, ln2_b, w_router, b_router,
           w_gate, w_up, w_down):
    batch, seq, d = x.shape
    mk, mv = _memkv(mem.reshape(-1, d), w_mem_k, w_mem_v)
    row = lambda v: v.reshape(1, -1)

    sg_bias = jnp.repeat(sg_b[0].T, HEAD_DIM, axis=1)
    h = _mixer_a(x, w_in_a[0].astype(BF16), w_out_a[0].astype(BF16), row(sg_ln_g[0]),
                 row(sg_ln_b[0]), sg_w[0], sg_bias, mk[0], mv[0], row(ln1_g[0]), row(ln1_b[0]))
    w_router_pad = jnp.pad(w_router, ((0, 0), (0, LANES - N_EXPERTS)))
    b_router_col = b_router.reshape(N_EXPERTS, 1)
    x_cm = _moe(h, w_router_pad, b_router_col, w_gate, w_up, w_down,
                row(ln2_g[0]), row(ln2_b[0]), layer=0, chunk_major=True)

    wqkv = jnp.stack([
        jnp.concatenate([w[:, g * DIL_OUT_WIDTH:(g + 1) * DIL_OUT_WIDTH]
                         for w in (w_in_b[0], w_k_shared, w_v_shared)], -1)
        for g in range(len(DIL_PAIRS))]).astype(BF16)
    wqm = w_in_b[0][:, DIL_Q_WIDTH:].astype(BF16)
    qkv0, qkv1, qkv2, qm = _proj_b(x_cm, wqkv, wqm)
    attn = [_dil_attn(qkv, g) for g, qkv in enumerate((qkv0, qkv1, qkv2))]
    h = _mixer_b(x_cm, qm, attn, mk[1], mv[1], w_out_b[0].astype(BF16),
                 row(ln1_g[1]), row(ln1_b[1]))
    return _moe(h, w_router_pad, b_router_col, w_gate, w_up, w_down,
                row(ln2_g[1]), row(ln2_b[1]), layer=1, chunk_major=False)
```

```python
import functools
import math

import jax
import jax.numpy as jnp
from jax import lax
from jax.experimental import pallas as pl
from jax.experimental.pallas import tpu as pltpu

D_MODEL = 1024
HEAD_DIM = 64
CHUNK = 128
N_SG = 12
SG_WIDTH = N_SG * HEAD_DIM
DIL_PAIRS = ((128, 1), (512, 4), (2048, 16))
HEADS_PER_DIL = 4
N_DIL_HEADS = HEADS_PER_DIL * len(DIL_PAIRS)
DIL_Q_WIDTH = N_DIL_HEADS * HEAD_DIM
DIL_OUT_WIDTH = HEADS_PER_DIL * HEAD_DIM
MEM_HEADS = 4
MEM_WIDTH = MEM_HEADS * HEAD_DIM
N_EXPERTS = 16
N_EXPERT_GROUPS = 4
EXPERTS_PER_GROUP = N_EXPERTS // N_EXPERT_GROUPS
TOP_K = 2
DEPTH = 2
ALPHA = (2 * DEPTH) ** 0.25
LN_EPS = 1e-5
ATT_SCALE = 1.0 / math.sqrt(HEAD_DIM)

LANES = 128
SUBLANES = 8
N_CHUNKS = D_MODEL // LANES

TS_MIX = 512
MXU_DEPTH = 256
TM_ROUTE = MXU_DEPTH
RUN_BITS = TM_ROUTE.bit_length()
TB_MOE = 512
HIDDEN_SLICES = 2
ROUTE_TILES = 8
DIL_INTERLEAVE = 4
VMEM_LIMIT = 56 * 1024 * 1024

F32 = jnp.float32
BF16 = jnp.bfloat16


def _dot(a, b):
    return jnp.dot(a, b, preferred_element_type=F32)


def _dot_nt(a, b):
    return lax.dot_general(a, b, (((1,), (1,)), ((), ())), preferred_element_type=F32)


def _layer_norm(x, g, b):
    mu = jnp.mean(x, -1, keepdims=True)
    xc = x - mu
    var = jnp.mean(xc * xc, -1, keepdims=True)
    return xc * lax.rsqrt(var + LN_EPS) * g + b


def _gelu(x):
    return 0.5 * x * (1.0 + lax.erf(x * (1.0 / math.sqrt(2.0))))


def _from_tiles(ref, n):
    return jnp.concatenate(
        [ref[pl.ds(c, n, stride=N_CHUNKS), :] for c in range(N_CHUNKS)], -1)


def _to_tiles(ref, val, n):
    for c in range(N_CHUNKS):
        ref[pl.ds(c, n, stride=N_CHUNKS), :] = val[:, c * LANES:(c + 1) * LANES]


def _from_chunks(ref):
    return jnp.concatenate([ref[c] for c in range(N_CHUNKS)], -1)


def _to_chunks(ref, val):
    for c in range(N_CHUNKS):
        ref[c] = val[:, c * LANES:(c + 1) * LANES]


def _chunk_tile_spec(rows, tiles_per_seq):
    return pl.BlockSpec((None, N_CHUNKS, rows, LANES),
                        lambda i: (i // tiles_per_seq, 0, i % tiles_per_seq, 0))


def _mem_attention(q, mk, mv):
    n = q.shape[0]
    head = lax.broadcasted_iota(jnp.int32, (1, MEM_WIDTH), 1) // HEAD_DIM
    stacked = jnp.concatenate(
        [jnp.where(head == h, q, jnp.zeros_like(q)) for h in range(MEM_HEADS)], 0)
    s = _dot_nt(stacked, mk)
    m = jnp.max(s, -1, keepdims=True)
    p = jnp.exp(s - m)
    l = jnp.sum(p, -1, keepdims=True)
    o = _dot((p * (1.0 / l)).astype(BF16), mv)
    out = jnp.zeros((n, MEM_WIDTH), F32)
    for h in range(MEM_HEADS):
        out = jnp.where(head == h, o[h * n:(h + 1) * n], out)
    return out


def _memkv_kernel(mem_ref, wk_ref, wv_ref, mk_ref, mv_ref):
    m = mem_ref[...].astype(BF16)
    mk_ref[...] = _dot(m, wk_ref[...].astype(BF16)).astype(BF16)
    mv_ref[...] = _dot(m, wv_ref[...].astype(BF16)).astype(BF16)


def _memkv(mem2d, w_mem_k, w_mem_v):
    n = mem2d.shape[0]
    return pl.pallas_call(
        _memkv_kernel,
        grid=(DEPTH,),
        in_specs=[
            pl.BlockSpec((n, D_MODEL), lambda l: (0, 0)),
            pl.BlockSpec((None, D_MODEL, MEM_WIDTH), lambda l: (l, 0, 0)),
            pl.BlockSpec((None, D_MODEL, MEM_WIDTH), lambda l: (l, 0, 0)),
        ],
        out_specs=[
            pl.BlockSpec((None, n, MEM_WIDTH), lambda l: (l, 0, 0)),
            pl.BlockSpec((None, n, MEM_WIDTH), lambda l: (l, 0, 0)),
        ],
        out_shape=[jax.ShapeDtypeStruct((DEPTH, n, MEM_WIDTH), BF16)] * 2,
        compiler_params=pltpu.CompilerParams(vmem_limit_bytes=VMEM_LIMIT),
        name="memkv",
    )(mem2d, w_mem_k, w_mem_v)


def _mixer_a_kernel(x_ref, win_ref, wout_ref, sgg_ref, sgb_ref, sgw_ref, sgbias_ref,
                    mk_ref, mv_ref, g1_ref, b1_ref, h_ref):
    ts = x_ref.shape[0]
    x = x_ref[...]
    proj = _dot(x.astype(BF16), win_ref[...])
    u = _gelu(proj[:, :SG_WIDTH])
    gv = _gelu(proj[:, SG_WIDTH:2 * SG_WIDTH])
    gv = _layer_norm(gv, sgg_ref[...], sgb_ref[...]).astype(BF16)

    row = lax.broadcasted_iota(jnp.int32, (CHUNK, CHUNK), 0)
    col = lax.broadcasted_iota(jnp.int32, (CHUNK, CHUNK), 1)
    ws = [jnp.where(row >= col, sgw_ref[g], 0.0).astype(BF16) for g in range(N_SG)]
    n_chunks = ts // CHUNK
    low_half = (lax.broadcasted_iota(jnp.int32, (CHUNK, n_chunks * LANES), 1) % LANES) < HEAD_DIM
    gated = []
    for j in range(N_SG // 2):
        slabs = jnp.concatenate(
            [gv[c * CHUNK:(c + 1) * CHUNK, j * LANES:(j + 1) * LANES] for c in range(n_chunks)], -1)
        both = _dot(jnp.concatenate([ws[2 * j], ws[2 * j + 1]], 0), slabs)
        gated.append(jnp.where(low_half, both[:CHUNK], both[CHUNK:]))
    rows = [jnp.concatenate([g[:, c * LANES:(c + 1) * LANES] for g in gated], -1)
            + sgbias_ref[...] for c in range(n_chunks)]
    mix = u * jnp.concatenate(rows, 0)

    qm = (proj[:, 2 * SG_WIDTH:] * ATT_SCALE).astype(BF16)
    mo = _mem_attention(qm, mk_ref[...], mv_ref[...])
    att = _dot(jnp.concatenate([mix, mo], -1).astype(BF16), wout_ref[...])
    _to_chunks(h_ref, _layer_norm(ALPHA * x + att, g1_ref[...], b1_ref[...]))


def _mixer_a(x, w_in, w_out, sg_g, sg_b, sg_w, sg_bias, mk, mv, g1, b1):
    b, s, d = x.shape
    nts = s // TS_MIX
    m = mk.shape[0] // b
    full = lambda *shape: pl.BlockSpec(shape, lambda i, j: (0,) * len(shape))
    return pl.pallas_call(
        _mixer_a_kernel,
        grid=(b, nts),
        in_specs=[
            pl.BlockSpec((None, TS_MIX, d), lambda i, j: (i, j, 0)),
            full(d, 2 * SG_WIDTH + MEM_WIDTH),
            full(SG_WIDTH + MEM_WIDTH, d),
            full(1, SG_WIDTH),
            full(1, SG_WIDTH),
            full(N_SG, CHUNK, CHUNK),
            full(CHUNK, SG_WIDTH),
            pl.BlockSpec((m, MEM_WIDTH), lambda i, j: (i, 0)),
            pl.BlockSpec((m, MEM_WIDTH), lambda i, j: (i, 0)),
            full(1, d),
            full(1, d),
        ],
        out_specs=pl.BlockSpec((None, N_CHUNKS, TS_MIX, LANES), lambda i, j: (i, 0, j, 0)),
        out_shape=jax.ShapeDtypeStruct((b, N_CHUNKS, s, LANES), F32),
        compiler_params=pltpu.CompilerParams(vmem_limit_bytes=VMEM_LIMIT),
        name="mixer_a",
    )(x, w_in, w_out, sg_g, sg_b, sg_w, sg_bias, mk, mv, g1, b1)


def _router_kernel(h_ref, wr_ref, br_ref, pos_ref, gate_ref, runs_ref, counts_ref, base_ref,
                   wsplit_ref):
    tm = TM_ROUTE

    @pl.when(pl.program_id(0) == 0)
    def _():
        base_ref[...] = jnp.zeros_like(base_ref)
        w = wr_ref[...]
        w_hi = w.astype(BF16)
        wsplit_ref[...] = jnp.concatenate([w_hi, (w - w_hi.astype(F32)).astype(BF16)], -1)

    eid = lax.broadcasted_iota(jnp.int32, (N_EXPERTS, tm), 0)
    group = eid // EXPERTS_PER_GROUP

    def probabilities(k):
        h = jnp.concatenate([h_ref[c, pl.ds(k * tm, tm), :] for c in range(N_CHUNKS)], -1)
        h_hi = h.astype(BF16)
        h_lo = (h - h_hi.astype(F32)).astype(BF16)
        prod = _dot(jnp.concatenate([h_hi, h_lo], 0), wsplit_ref[...])
        logits = prod[:tm, :LANES] + (prod[:tm, LANES:] + prod[tm:, :LANES])
        lt = logits.T[:N_EXPERTS]
        ex = jnp.exp(lt - jnp.max(lt, 0, keepdims=True))
        probs = ex / jnp.sum(ex, 0, keepdims=True)
        return probs, probs + br_ref[...]

    def top2(sel, mask):
        v = jnp.where(mask, sel, -jnp.inf)
        m1 = jnp.max(v, 0, keepdims=True)
        i1 = jnp.min(jnp.where(v == m1, eid, N_EXPERTS), 0, keepdims=True)
        v2 = jnp.where(eid == i1, -jnp.inf, v)
        m2 = jnp.max(v2, 0, keepdims=True)
        i2 = jnp.min(jnp.where(v2 == m2, eid, N_EXPERTS), 0, keepdims=True)
        return m1, i1, m2, i2

    def best_group(sel):
        scores = []
        for g in range(N_EXPERT_GROUPS):
            m1, _, m2, _ = top2(sel, group == g)
            scores.append(m1 + m2)
        best = functools.reduce(jnp.maximum, scores)
        g_idx = jnp.full((1, tm), N_EXPERT_GROUPS - 1, jnp.int32)
        for g in reversed(range(N_EXPERT_GROUPS - 1)):
            g_idx = jnp.where(scores[g] == best, g, g_idx)
        return g_idx

    def choose(probs, sel, g_idx):
        _, e0, _, e1 = top2(sel, group == g_idx)
        hot0 = eid == e0
        hot1 = eid == e1
        p0 = jnp.sum(jnp.where(hot0, probs, 0.0), 0, keepdims=True)
        p1 = jnp.sum(jnp.where(hot1, probs, 0.0), 0, keepdims=True)
        return hot0, hot1, p0, p1

    earlier = jnp.where(lax.broadcasted_iota(jnp.int32, (tm, tm), 0)
                        < lax.broadcasted_iota(jnp.int32, (tm, tm), 1), 1.0, 0.0).astype(BF16)
    below = jnp.where(lax.broadcasted_iota(jnp.int32, (N_EXPERTS, N_EXPERTS), 1)
                      < lax.broadcasted_iota(jnp.int32, (N_EXPERTS, N_EXPERTS), 0),
                      1.0, 0.0).astype(BF16)

    def positions(hot0, hot1):
        hot = jnp.where(hot0 | hot1, 1.0, 0.0)
        hot_bf = hot.astype(BF16)
        before = _dot(hot_bf, earlier)
        run_start = jnp.sum(_dot(below, hot_bf), 1, keepdims=True)
        local = run_start + before
        q0 = jnp.sum(jnp.where(hot0, local, 0.0), 0, keepdims=True)
        q1 = jnp.sum(jnp.where(hot1, local, 0.0), 0, keepdims=True)
        return q0, q1, jnp.sum(hot, 1, keepdims=True)

    tiles = range(ROUTE_TILES)
    scored = [probabilities(k) for k in tiles]
    groups = [best_group(sel) for _, sel in scored]
    chosen = [choose(probs, sel, g_idx) for (probs, sel), g_idx in zip(scored, groups)]
    placed = [positions(hot0, hot1) for hot0, hot1, _, _ in chosen]

    row = lax.broadcasted_iota(jnp.int32, (SUBLANES, tm), 0)
    grow = lax.broadcasted_iota(jnp.int32, (LANES, tm), 0)
    lane = lax.broadcasted_iota(jnp.int32, (N_EXPERTS, LANES), 1)
    base = base_ref[...]
    for k, ((_, _, p0, p1), (q0, q1, count)) in enumerate(zip(chosen, placed)):
        psum = p0 + p1
        pos_ref[k] = jnp.where(row == 0, q0, jnp.where(row == 1, q1, 0.0)).astype(jnp.int32)
        cols = jnp.where(grow == 0, p0 / psum, jnp.where(grow == 1, p1 / psum, jnp.where(
            grow == 2, q0, jnp.where(grow == 3, q1, 0.0))))
        gate_ref[pl.ds(k * tm, tm), :] = cols.T[:, :2 * TOP_K]
        runs_ref[k] = jnp.where(lane == 0, count, jnp.where(lane == 1, base, 0.0)).astype(jnp.int32)
        base = base + count
    base_ref[...] = base
    counts_ref[...] = base.astype(jnp.int32)


def _router(h_cm, w_router, b_router):
    batch, _, seq, _ = h_cm.shape
    t = batch * seq
    tm = TM_ROUTE
    rows = ROUTE_TILES * tm
    return pl.pallas_call(
        _router_kernel,
        grid=(t // rows,),
        in_specs=[
            _chunk_tile_spec(rows, seq // rows),
            pl.BlockSpec((D_MODEL, LANES), lambda i: (0, 0)),
            pl.BlockSpec((N_EXPERTS, 1), lambda i: (0, 0)),
        ],
        out_specs=[
            pl.BlockSpec((ROUTE_TILES, SUBLANES, tm), lambda i: (i, 0, 0)),
            pl.BlockSpec((rows, 2 * TOP_K), lambda i: (i, 0)),
            pl.BlockSpec((ROUTE_TILES, N_EXPERTS, LANES), lambda i: (i, 0, 0)),
            pl.BlockSpec((N_EXPERTS, 1), lambda i: (0, 0)),
        ],
        out_shape=[
            jax.ShapeDtypeStruct((t // tm, SUBLANES, tm), jnp.int32),
            jax.ShapeDtypeStruct((t, 2 * TOP_K), F32),
            jax.ShapeDtypeStruct((t // tm, N_EXPERTS, LANES), jnp.int32),
            jax.ShapeDtypeStruct((N_EXPERTS, 1), jnp.int32),
        ],
        scratch_shapes=[pltpu.VMEM((N_EXPERTS, 1), F32), pltpu.VMEM((D_MODEL, 2 * LANES), BF16)],
        compiler_params=pltpu.CompilerParams(
            dimension_semantics=("arbitrary",), vmem_limit_bytes=VMEM_LIMIT),
        name="router",
    )(h_cm, w_router, b_router)


def _start_run_copies(runs_ref, start_ref, tile_ref, rows_ref, sem, to_rows):
    offset = 0
    for e in range(N_EXPERTS):
        count = runs_ref[e, 0]
        first_row = start_ref[e] + runs_ref[e, 1]
        for bit in reversed(range(RUN_BITS)):
            size = (1 << bit) * N_CHUNKS
            done = (count >> (bit + 1)) << (bit + 1)

            @pl.when(((count >> bit) & 1) == 1)
            def _():
                in_tile = tile_ref.at[
                    pl.ds(pl.multiple_of((offset + done) * N_CHUNKS, N_CHUNKS), size)]
                in_rows = rows_ref.at[
                    pl.ds(pl.multiple_of((first_row + done) * N_CHUNKS, N_CHUNKS), size)]
                if to_rows:
                    pltpu.make_async_copy(in_tile, in_rows, sem).start(priority=bit % 2)
                else:
                    pltpu.make_async_copy(in_rows, in_tile, sem).start(priority=bit % 2)
        offset = offset + count


def _wait_run_copies(tile_ref, rows_ref, sem):
    pltpu.make_async_copy(tile_ref, rows_ref.at[pl.ds(0, tile_ref.shape[0])], sem).wait()


def _dispatch_kernel(pos_ref, runs_ref, start_ref, end_ref, cnt_ref, h_ref, xs_ref,
                     zero_ref, sorted_ref, sem, zsem):
    i = pl.program_id(0)
    tm = TM_ROUTE

    @pl.when(i == 0)
    def _():
        zero_ref[...] = jnp.zeros_like(zero_ref)

        def zero_copy(first_row):
            first = pl.multiple_of(first_row * N_CHUNKS, N_CHUNKS)
            return pltpu.make_async_copy(
                zero_ref, xs_ref.at[pl.ds(first, TB_MOE * N_CHUNKS)], zsem)

        for wait in (False, True):
            for e in range(N_EXPERTS):
                @pl.when(cnt_ref[e, 0] > 0)
                def _():
                    cp = zero_copy(end_ref[e] - TB_MOE)
                    cp.wait() if wait else cp.start()

        def zero_block(blk, carry):
            cp = zero_copy(blk * TB_MOE)
            cp.start()
            cp.wait()
            return carry

        n_blocks = xs_ref.shape[0] // (TB_MOE * N_CHUNKS)
        lax.fori_loop(end_ref[N_EXPERTS - 1] // TB_MOE, n_blocks, zero_block, 0)

    p = lax.broadcasted_iota(jnp.int32, (TOP_K * tm, tm), 0)
    onehot = (p == pos_ref[0:1, :]) | (p == pos_ref[1:2, :])
    sorted_rows = _dot(jnp.where(onehot, 1.0, 0.0).astype(BF16),
                       _from_chunks(h_ref).astype(BF16))

    slot = i % 2
    buf = sorted_ref.at[slot]

    @pl.when(i >= 2)
    def _():
        _wait_run_copies(buf, xs_ref, sem.at[slot])

    _to_tiles(buf, sorted_rows, TOP_K * tm)
    _start_run_copies(runs_ref, start_ref, buf, xs_ref, sem.at[slot], to_rows=True)

    @pl.when(i == pl.num_programs(0) - 1)
    def _():
        _wait_run_copies(buf, xs_ref, sem.at[slot])
        _wait_run_copies(sorted_ref.at[1 - slot], xs_ref, sem.at[1 - slot])


def _dispatch(pos, runs, pad_starts, pad_ends, counts, h_cm, n_rows):
    nt = pos.shape[0]
    tm = TM_ROUTE
    seq = h_cm.shape[2]
    smem_vec = pl.BlockSpec(memory_space=pltpu.SMEM)
    return pl.pallas_call(
        _dispatch_kernel,
        grid=(nt,),
        in_specs=[
            pl.BlockSpec((None, SUBLANES, tm), lambda i: (i, 0, 0)),
            pl.BlockSpec((None, N_EXPERTS, LANES), lambda i: (i, 0, 0), memory_space=pltpu.SMEM),
            smem_vec, smem_vec, smem_vec,
            _chunk_tile_spec(tm, seq // tm)],
        out_specs=pl.BlockSpec(memory_space=pl.ANY),
        out_shape=jax.ShapeDtypeStruct((n_rows * N_CHUNKS, LANES), F32),
        scratch_shapes=[
            pltpu.VMEM((TB_MOE * N_CHUNKS, LANES), F32),
            pltpu.VMEM((2, TOP_K * tm * N_CHUNKS, LANES), F32),
            pltpu.SemaphoreType.DMA((2,)),
            pltpu.SemaphoreType.DMA(()),
        ],
        compiler_params=pltpu.CompilerParams(
            dimension_semantics=("arbitrary",), has_side_effects=True,
            vmem_limit_bytes=VMEM_LIMIT),
        name="dispatch",
    )(pos, runs, pad_starts, pad_ends, counts, h_cm)


def _experts_kernel(be_ref, nused_ref, valid_ref, xs_ref, wg_ref, wu_ref, wd_ref, ys_ref,
                    wg_bf, wu_bf, wd_bf):
    i = pl.program_id(0)
    nused = nused_ref[0]
    half = TB_MOE // 2

    def ffn(nrows):
        x = _from_tiles(xs_ref, nrows).astype(BF16)
        width = wg_bf.shape[1] // HIDDEN_SLICES
        cols = [slice(k * width, (k + 1) * width) for k in range(HIDDEN_SLICES)]
        acts = []
        for c in cols:
            hg = _dot(x, wg_bf[:, c])
            acts.append((hg * jax.nn.sigmoid(hg) * _dot(x, wu_bf[:, c])).astype(BF16))
        y = functools.reduce(jnp.add, [_dot(act, wd_bf[c, :]) for act, c in zip(acts, cols)])
        _to_tiles(ys_ref, y, nrows)

    @pl.when(i < nused)
    def _():
        prev = be_ref[jnp.maximum(i - 1, 0)]

        @pl.when((i == 0) | (be_ref[i] != prev))
        def _():
            wg_bf[...] = wg_ref[...].astype(BF16)
            wu_bf[...] = wu_ref[...].astype(BF16)
            wd_bf[...] = wd_ref[...].astype(BF16)

        @pl.when(valid_ref[i] > half)
        def _():
            ffn(TB_MOE)

        @pl.when(valid_ref[i] <= half)
        def _():
            ffn(half)
            padding = ys_ref.at[pl.ds(half * N_CHUNKS, half * N_CHUNKS)]
            padding[...] = jnp.zeros_like(padding)

    @pl.when(i >= nused)
    def _():
        ys_ref[...] = jnp.zeros_like(ys_ref)


def _experts(block_expert, nused, block_valid, xs, w_gate, w_up, w_down, layer):
    n_blocks = xs.shape[0] // (TB_MOE * N_CHUNKS)
    d, de = w_gate.shape[2], w_gate.shape[3]

    def x_map(i, be, nu, bv):
        return (jnp.minimum(i, nu[0] - 1), 0)

    def w_map(i, be, nu, bv):
        return (layer, be[jnp.minimum(i, nu[0] - 1)], 0, 0)

    return pl.pallas_call(
        _experts_kernel,
        grid_spec=pltpu.PrefetchScalarGridSpec(
            num_scalar_prefetch=3,
            grid=(n_blocks,),
            in_specs=[
                pl.BlockSpec((TB_MOE * N_CHUNKS, LANES), x_map),
                pl.BlockSpec((None, None, d, de), w_map),
                pl.BlockSpec((None, None, d, de), w_map),
                pl.BlockSpec((None, None, de, d), w_map),
            ],
            out_specs=pl.BlockSpec((TB_MOE * N_CHUNKS, LANES), lambda i, be, nu, bv: (i, 0)),
            scratch_shapes=[
                pltpu.VMEM((d, de), BF16),
                pltpu.VMEM((d, de), BF16),
                pltpu.VMEM((de, d), BF16),
            ],
        ),
        out_shape=jax.ShapeDtypeStruct(xs.shape, F32),
        compiler_params=pltpu.CompilerParams(
            dimension_semantics=("arbitrary",), vmem_limit_bytes=VMEM_LIMIT),
        name="experts",
    )(block_expert, nused, block_valid, xs, w_gate, w_up, w_down)


def _combine_kernel(runs_ref, next_runs_ref, start_ref, h_ref, gate_ref, g2_ref, b2_ref, ys_ref,
                    out_ref, sorted_ref, sem, *, chunk_major):
    tm = TM_ROUTE
    i = pl.program_id(0)
    slot = i % 2

    @pl.when(i == 0)
    def _():
        _start_run_copies(runs_ref, start_ref, sorted_ref.at[0], ys_ref, sem.at[0], to_rows=False)

    @pl.when(i + 1 < pl.num_programs(0))
    def _():
        _start_run_copies(next_runs_ref, start_ref, sorted_ref.at[1 - slot], ys_ref,
                          sem.at[1 - slot], to_rows=False)

    _wait_run_copies(sorted_ref.at[slot], ys_ref, sem.at[slot])

    gate = gate_ref[...]
    p = lax.broadcasted_iota(jnp.int32, (tm, TOP_K * tm), 1)
    onehot = jnp.concatenate([p == gate[:, 2:3].astype(jnp.int32),
                              p == gate[:, 3:4].astype(jnp.int32)], 0)
    picked = _dot(jnp.where(onehot, 1.0, 0.0).astype(BF16),
                  _from_tiles(sorted_ref.at[slot], TOP_K * tm).astype(BF16))
    ffn = picked[:tm] * gate[:, 0:1] + picked[tm:] * gate[:, 1:2]
    out = _layer_norm(ALPHA * _from_chunks(h_ref) + ffn, g2_ref[...], b2_ref[...])
    if chunk_major:
        _to_chunks(out_ref, out)
    else:
        out_ref[...] = out


def _combine(runs, pad_starts, h_cm, gate, g2, b2, ys, chunk_major):
    batch, _, seq, _ = h_cm.shape
    tm = TM_ROUTE
    t = batch * seq
    nt = t // tm
    nts = seq // tm
    if chunk_major:
        out_spec = _chunk_tile_spec(tm, nts)
        out_shape = jax.ShapeDtypeStruct((batch, N_CHUNKS, seq, LANES), F32)
    else:
        out_spec = pl.BlockSpec((None, tm, D_MODEL), lambda i: (i // nts, i % nts, 0))
        out_shape = jax.ShapeDtypeStruct((batch, seq, D_MODEL), F32)
    return pl.pallas_call(
        functools.partial(_combine_kernel, chunk_major=chunk_major),
        grid=(nt,),
        in_specs=[
            pl.BlockSpec((None, N_EXPERTS, LANES), lambda i: (i, 0, 0), memory_space=pltpu.SMEM),
            pl.BlockSpec((None, N_EXPERTS, LANES), lambda i: (jnp.minimum(i + 1, nt - 1), 0, 0),
                         memory_space=pltpu.SMEM),
            pl.BlockSpec(memory_space=pltpu.SMEM),
            _chunk_tile_spec(tm, nts),
            pl.BlockSpec((tm, 2 * TOP_K), lambda i: (i, 0)),
            pl.BlockSpec((1, D_MODEL), lambda i: (0, 0)),
            pl.BlockSpec((1, D_MODEL), lambda i: (0, 0)),
            pl.BlockSpec(memory_space=pl.ANY),
        ],
        out_specs=out_spec,
        out_shape=out_shape,
        scratch_shapes=[
            pltpu.VMEM((2, TOP_K * tm * N_CHUNKS, LANES), F32),
            pltpu.SemaphoreType.DMA((2,)),
        ],
        compiler_params=pltpu.CompilerParams(
            dimension_semantics=("arbitrary",), vmem_limit_bytes=VMEM_LIMIT),
        name="combine",
    )(runs, runs, pad_starts, h_cm, gate, g2, b2, ys)


def _moe(h_cm, w_router, b_router, w_gate, w_up, w_down, g2, b2, layer, chunk_major):
    t = h_cm.shape[0] * h_cm.shape[2]
    pos, gate, runs, counts = _router(h_cm, w_router, b_router)
    cnt = counts[:, 0]
    padded = (cnt + TB_MOE - 1) // TB_MOE * TB_MOE
    pad_ends = jnp.cumsum(padded).astype(jnp.int32)
    pad_starts = pad_ends - padded
    n_rows = t * TOP_K + N_EXPERTS * TB_MOE
    n_blocks = n_rows // TB_MOE
    block_start = jnp.arange(n_blocks, dtype=jnp.int32) * TB_MOE
    block_expert = jnp.minimum(
        jnp.sum(block_start[:, None] >= pad_ends[None, :], -1), N_EXPERTS - 1).astype(jnp.int32)
    nused = (pad_ends[-1:] // TB_MOE).astype(jnp.int32)
    block_valid = jnp.clip((pad_starts + cnt)[block_expert] - block_start, 0, TB_MOE)
    xs = _dispatch(pos, runs, pad_starts, pad_ends, counts, h_cm, n_rows)
    ys = _experts(block_expert, nused, block_valid.astype(jnp.int32), xs, w_gate, w_up, w_down,
                  layer)
    return _combine(runs, pad_starts, h_cm, gate, g2, b2, ys, chunk_major)


def _residue_rows(x_ref, dilation):
    n = x_ref.shape[1]
    if dilation == 1:
        return _from_chunks(x_ref)
    per = n // dilation
    return jnp.concatenate(
        [jnp.concatenate([x_ref[c, pl.ds(r, per, stride=dilation), :] for r in range(dilation)], 0)
         for c in range(N_CHUNKS)], -1)


def _proj_b_kernel(x_ref, wqkv_ref, wqm_ref, qkv0_ref, qkv1_ref, qkv2_ref, qm_ref):
    ts = x_ref.shape[1]
    qscale = jnp.where(
        lax.broadcasted_iota(jnp.int32, (1, 3 * DIL_OUT_WIDTH), 1) < DIL_OUT_WIDTH, ATT_SCALE, 1.0)
    for g, (out_ref, (_, dilation)) in enumerate(zip((qkv0_ref, qkv1_ref, qkv2_ref), DIL_PAIRS)):
        xb = _residue_rows(x_ref, dilation).astype(BF16)
        qkv = (_dot(xb, wqkv_ref[g]) * qscale).astype(BF16)
        per = ts // dilation
        for r in range(dilation):
            out_ref[r] = qkv[r * per:(r + 1) * per]
    qm = _dot(_from_chunks(x_ref).astype(BF16), wqm_ref[...])
    qm_ref[...] = (qm * ATT_SCALE).astype(BF16)


def _proj_b(x_cm, wqkv, wqm):
    b, _, s, _ = x_cm.shape
    ts = TS_MIX
    width = 3 * DIL_OUT_WIDTH
    out_specs, out_shapes = [], []
    for _, dilation in DIL_PAIRS:
        out_specs.append(pl.BlockSpec((None, dilation, ts // dilation, width),
                                      lambda i, j: (i, 0, j, 0)))
        out_shapes.append(jax.ShapeDtypeStruct((b, dilation, s // dilation, width), BF16))
    out_specs.append(pl.BlockSpec((None, ts, MEM_WIDTH), lambda i, j: (i, j, 0)))
    out_shapes.append(jax.ShapeDtypeStruct((b, s, MEM_WIDTH), BF16))
    return pl.pallas_call(
        _proj_b_kernel,
        grid=(b, s // ts),
        in_specs=[
            pl.BlockSpec((None, N_CHUNKS, ts, LANES), lambda i, j: (i, 0, j, 0)),
            pl.BlockSpec(wqkv.shape, lambda i, j: (0, 0, 0)),
            pl.BlockSpec(wqm.shape, lambda i, j: (0, 0)),
        ],
        out_specs=out_specs,
        out_shape=out_shapes,
        compiler_params=pltpu.CompilerParams(vmem_limit_bytes=VMEM_LIMIT),
        name="proj_b",
    )(x_cm, wqkv, wqm)


def _dil_attn_kernel(qkv_ref, o_ref, lse_ref, *, dilation, slopes):
    nb = qkv_ref.shape[1] // CHUNK
    width = DIL_OUT_WIDTH
    heads = HEADS_PER_DIL
    row = lax.broadcasted_iota(jnp.int32, (heads * CHUNK, 2 * CHUNK), 0)
    jk = lax.broadcasted_iota(jnp.int32, (heads * CHUNK, 2 * CHUNK), 1)
    rel = CHUNK + row % CHUNK - jk
    slope = functools.reduce(
        lambda acc, h: jnp.where(row // CHUNK == h, slopes[h], acc), range(1, heads), slopes[0])
    bias = jnp.where((rel >= 0) & (rel <= CHUNK), -slope * (dilation * rel).astype(F32), -jnp.inf)
    before_start = jk < CHUNK
    head = lax.broadcasted_iota(jnp.int32, (1, width), 1) // HEAD_DIM

    def scores(idx):
        r = idx // nb
        jb = idx % nb
        cur = pl.ds(pl.multiple_of(jb * CHUNK, CHUNK), CHUNK)
        prev = pl.ds(pl.multiple_of(jnp.maximum(jb - 1, 0) * CHUNK, CHUNK), CHUNK)
        q = qkv_ref[r, cur, 0:width]
        kband = jnp.concatenate([qkv_ref[r, prev, width:2 * width],
                                 qkv_ref[r, cur, width:2 * width]], 0)
        stacked = jnp.concatenate(
            [jnp.where(head == h, q, jnp.zeros_like(q)) for h in range(heads)], 0)
        s = _dot_nt(stacked, kband) + bias
        return r, cur, prev, jnp.where(before_start & (jb == 0), -jnp.inf, s)

    def softmax(s):
        m = jnp.max(s, -1, keepdims=True)
        p = jnp.exp(s - m)
        l = jnp.sum(p, -1, keepdims=True)
        return (p * (1.0 / l)).astype(BF16), m + jnp.log(l)

    def finish(r, cur, prev, p, row_lse):
        vband = jnp.concatenate([qkv_ref[r, prev, 2 * width:3 * width],
                                 qkv_ref[r, cur, 2 * width:3 * width]], 0)
        o = _dot(p, vband)
        out = jnp.zeros((CHUNK, width), F32)
        lse = jnp.zeros((CHUNK, width), F32)
        for h in range(heads):
            rows = slice(h * CHUNK, (h + 1) * CHUNK)
            out = jnp.where(head == h, o[rows], out)
            lse = jnp.where(head == h, row_lse[rows], lse)
        o_ref[r, cur, :] = out
        lse_ref[r, cur, :] = lse

    def group(i, carry):
        blocks = [scores(DIL_INTERLEAVE * i + j) for j in range(DIL_INTERLEAVE)]
        probs = [softmax(s) for _, _, _, s in blocks]
        for (r, cur, prev, _), (p, row_lse) in zip(blocks, probs):
            finish(r, cur, prev, p, row_lse)
        return carry

    lax.fori_loop(0, dilation * nb // DIL_INTERLEAVE, group, 0)


def _dil_attn(qkv, group):
    b, dilation, length, _ = qkv.shape
    slopes = tuple(2.0 ** (-8.0 * (group * HEADS_PER_DIL + h + 1) / N_DIL_HEADS)
                   for h in range(HEADS_PER_DIL))
    out_spec = pl.BlockSpec((None, dilation, length, DIL_OUT_WIDTH), lambda i: (i, 0, 0, 0))
    out_shape = jax.ShapeDtypeStruct((b, dilation, length, DIL_OUT_WIDTH), F32)
    return pl.pallas_call(
        functools.partial(_dil_attn_kernel, dilation=dilation, slopes=slopes),
        grid=(b,),
        in_specs=[pl.BlockSpec((None, dilation, length, 3 * DIL_OUT_WIDTH),
                               lambda i: (i, 0, 0, 0))],
        out_specs=[out_spec, out_spec],
        out_shape=[out_shape, out_shape],
        compiler_params=pltpu.CompilerParams(vmem_limit_bytes=VMEM_LIMIT),
        name=f"dil_attn_{group}",
    )(qkv)


def _token_order(src_ref, scratch, dilation):
    if dilation == 1:
        return src_ref[0]
    per = src_ref.shape[1]
    halves = DIL_OUT_WIDTH // LANES
    for r in range(dilation):
        v = src_ref[r]
        for c in range(halves):
            scratch[c, pl.ds(r, per, stride=dilation), :] = v[:, c * LANES:(c + 1) * LANES]
    return jnp.concatenate([scratch[c] for c in range(halves)], -1)


def _mixer_b_kernel(x_ref, qm_ref, o0_ref, l0_ref, o1_ref, l1_ref, o2_ref, l2_ref,
                    mk_ref, mv_ref, wout_ref, g1_ref, b1_ref, h_ref,
                    so1, sl1, so2, sl2):
    outs = [o0_ref[0], _token_order(o1_ref, so1, DIL_PAIRS[1][1]),
            _token_order(o2_ref, so2, DIL_PAIRS[2][1])]
    lses = [l0_ref[0], _token_order(l1_ref, sl1, DIL_PAIRS[1][1]),
            _token_order(l2_ref, sl2, DIL_PAIRS[2][1])]
    top = functools.reduce(jnp.maximum, lses)
    es = [jnp.exp(l - top) for l in lses]
    inv = 1.0 / functools.reduce(jnp.add, es)
    mix = functools.reduce(jnp.add, [e * inv * o for e, o in zip(es, outs)])
    mo = _mem_attention(qm_ref[...], mk_ref[...], mv_ref[...])
    att = _dot(jnp.concatenate([mix, mo], -1).astype(BF16), wout_ref[...])
    _to_chunks(h_ref, _layer_norm(ALPHA * _from_chunks(x_ref) + att, g1_ref[...], b1_ref[...]))


def _mixer_b(x_cm, qm, attn, mk, mv, w_out, g1, b1):
    b, _, s, _ = x_cm.shape
    ts = TS_MIX
    nts = s // ts
    m = mk.shape[0] // b
    in_specs = [
        pl.BlockSpec((None, N_CHUNKS, ts, LANES), lambda i, j: (i, 0, j, 0)),
        pl.BlockSpec((None, ts, MEM_WIDTH), lambda i, j: (i, j, 0)),
    ]
    args = [x_cm, qm]
    for (o, lse), (_, dilation) in zip(attn, DIL_PAIRS):
        spec = pl.BlockSpec((None, dilation, ts // dilation, DIL_OUT_WIDTH),
                            lambda i, j: (i, 0, j, 0))
        in_specs += [spec, spec]
        args += [o, lse]
    in_specs += [
        pl.BlockSpec((m, MEM_WIDTH), lambda i, j: (i, 0)),
        pl.BlockSpec((m, MEM_WIDTH), lambda i, j: (i, 0)),
        pl.BlockSpec(w_out.shape, lambda i, j: (0, 0)),
        pl.BlockSpec((1, D_MODEL), lambda i, j: (0, 0)),
        pl.BlockSpec((1, D_MODEL), lambda i, j: (0, 0)),
    ]
    args += [mk, mv, w_out, g1, b1]
    halves = DIL_OUT_WIDTH // LANES
    return pl.pallas_call(
        _mixer_b_kernel,
        grid=(b, nts),
        in_specs=in_specs,
        out_specs=pl.BlockSpec((None, N_CHUNKS, ts, LANES), lambda i, j: (i, 0, j, 0)),
        out_shape=jax.ShapeDtypeStruct((b, N_CHUNKS, s, LANES), F32),
        scratch_shapes=[pltpu.VMEM((halves, ts, LANES), F32)] * 4,
        compiler_params=pltpu.CompilerParams(vmem_limit_bytes=VMEM_LIMIT),
        name="mixer_b",
    )(*args)


def kernel(x, mem, w_in_a, w_out_a, sg_ln_g, sg_ln_b, sg_w, sg_b, w_in_b, w_out_b, w_k_shared,
           w_v_shared, w_mem_k, w_mem_v, ln1_g, ln1_b, ln2_g, ln2_b, w_router, b_router,
           w_gate, w_up, w_down):
    batch, seq, d = x.shape
    mk, mv = _memkv(mem.reshape(-1, d), w_mem_k, w_mem_v)
    row = lambda v: v.reshape(1, -1)

    sg_bias = jnp.repeat(sg_b[0].T, HEAD_DIM, axis=1)
    h = _mixer_a(x, w_in_a[0].astype(BF16), w_out_a[0].astype(BF16), row(sg_ln_g[0]),
                 row(sg_ln_b[0]), sg_w[0], sg_bias, mk[0], mv[0], row(ln1_g[0]), row(ln1_b[0]))
    w_router_pad = jnp.pad(w_router, ((0, 0), (0, LANES - N_EXPERTS)))
    b_router_col = b_router.reshape(N_EXPERTS, 1)
    x_cm = _moe(h, w_router_pad, b_router_col, w_gate, w_up, w_down,
                row(ln2_g[0]), row(ln2_b[0]), layer=0, chunk_major=True)

    wqkv = jnp.stack([
        jnp.concatenate([w[:, g * DIL_OUT_WIDTH:(g + 1) * DIL_OUT_WIDTH]
                         for w in (w_in_b[0], w_k_shared, w_v_shared)], -1)
        for g in range(len(DIL_PAIRS))]).astype(BF16)
    wqm = w_in_b[0][:, DIL_Q_WIDTH:].astype(BF16)
    qkv0, qkv1, qkv2, qm = _proj_b(x_cm, wqkv, wqm)
    attn = [_dil_attn(qkv, g) for g, qkv in enumerate((qkv0, qkv1, qkv2))]
    h = _mixer_b(x_cm, qm, attn, mk[1], mv[1], w_out_b[0].astype(BF16),
                 row(ln1_g[1]), row(ln1_b[1]))
    return _moe(h, w_router_pad, b_router_col, w_gate, w_up, w_down,
                row(ln2_g[1]), row(ln2_b[1]), layer=1, chunk_major=False)
```

```python
import functools
import math

import jax
import jax.numpy as jnp
from jax import lax
from jax.experimental import pallas as pl
from jax.experimental.pallas import tpu as pltpu

D_MODEL = 1024
HEAD_DIM = 64
CHUNK = 128
N_SG = 12
SG_WIDTH = N_SG * HEAD_DIM
DIL_PAIRS = ((128, 1), (512, 4), (2048, 16))
HEADS_PER_DIL = 4
N_DIL_HEADS = HEADS_PER_DIL * len(DIL_PAIRS)
DIL_Q_WIDTH = N_DIL_HEADS * HEAD_DIM
DIL_OUT_WIDTH = HEADS_PER_DIL * HEAD_DIM
MEM_HEADS = 4
MEM_WIDTH = MEM_HEADS * HEAD_DIM
N_EXPERTS = 16
N_EXPERT_GROUPS = 4
EXPERTS_PER_GROUP = N_EXPERTS // N_EXPERT_GROUPS
TOP_K = 2
DEPTH = 2
ALPHA = (2 * DEPTH) ** 0.25
LN_EPS = 1e-5
ATT_SCALE = 1.0 / math.sqrt(HEAD_DIM)

LANES = 128
SUBLANES = 8
N_CHUNKS = D_MODEL // LANES

TS_MIX = 512
MXU_DEPTH = 256
TM_ROUTE = MXU_DEPTH
RUN_BITS = TM_ROUTE.bit_length()
TB_MOE = 1024
HIDDEN_SLICES = 4
ROUTE_TILES = 8
DIL_INTERLEAVE = 4
VMEM_LIMIT = 56 * 1024 * 1024

F32 = jnp.float32
BF16 = jnp.bfloat16


def _dot(a, b):
    return jnp.dot(a, b, preferred_element_type=F32)


def _dot_nt(a, b):
    return lax.dot_general(a, b, (((1,), (1,)), ((), ())), preferred_element_type=F32)


def _layer_norm(x, g, b):
    mu = jnp.mean(x, -1, keepdims=True)
    xc = x - mu
    var = jnp.mean(xc * xc, -1, keepdims=True)
    return xc * lax.rsqrt(var + LN_EPS) * g + b


def _gelu(x):
    return 0.5 * x * (1.0 + lax.erf(x * (1.0 / math.sqrt(2.0))))


def _from_tiles(ref, n):
    return jnp.concatenate(
        [ref[pl.ds(c, n, stride=N_CHUNKS), :] for c in range(N_CHUNKS)], -1)


def _to_tiles(ref, val, n):
    for c in range(N_CHUNKS):
        ref[pl.ds(c, n, stride=N_CHUNKS), :] = val[:, c * LANES:(c + 1) * LANES]


def _from_chunks(ref):
    return jnp.concatenate([ref[c] for c in range(N_CHUNKS)], -1)


def _to_chunks(ref, val):
    for c in range(N_CHUNKS):
        ref[c] = val[:, c * LANES:(c + 1) * LANES]


def _chunk_tile_spec(rows, tiles_per_seq):
    return pl.BlockSpec((None, N_CHUNKS, rows, LANES),
                        lambda i: (i // tiles_per_seq, 0, i % tiles_per_seq, 0))


def _mem_attention(q, mk, mv):
    n = q.shape[0]
    head = lax.broadcasted_iota(jnp.int32, (1, MEM_WIDTH), 1) // HEAD_DIM
    stacked = jnp.concatenate(
        [jnp.where(head == h, q, jnp.zeros_like(q)) for h in range(MEM_HEADS)], 0)
    s = _dot_nt(stacked, mk)
    m = jnp.max(s, -1, keepdims=True)
    p = jnp.exp(s - m)
    l = jnp.sum(p, -1, keepdims=True)
    o = _dot((p * (1.0 / l)).astype(BF16), mv)
    out = jnp.zeros((n, MEM_WIDTH), F32)
    for h in range(MEM_HEADS):
        out = jnp.where(head == h, o[h * n:(h + 1) * n], out)
    return out


def _memkv_kernel(mem_ref, wk_ref, wv_ref, mk_ref, mv_ref):
    m = mem_ref[...].astype(BF16)
    mk_ref[...] = _dot(m, wk_ref[...].astype(BF16)).astype(BF16)
    mv_ref[...] = _dot(m, wv_ref[...].astype(BF16)).astype(BF16)


def _memkv(mem2d, w_mem_k, w_mem_v):
    n = mem2d.shape[0]
    return pl.pallas_call(
        _memkv_kernel,
        grid=(DEPTH,),
        in_specs=[
            pl.BlockSpec((n, D_MODEL), lambda l: (0, 0)),
            pl.BlockSpec((None, D_MODEL, MEM_WIDTH), lambda l: (l, 0, 0)),
            pl.BlockSpec((None, D_MODEL, MEM_WIDTH), lambda l: (l, 0, 0)),
        ],
        out_specs=[
            pl.BlockSpec((None, n, MEM_WIDTH), lambda l: (l, 0, 0)),
            pl.BlockSpec((None, n, MEM_WIDTH), lambda l: (l, 0, 0)),
        ],
        out_shape=[jax.ShapeDtypeStruct((DEPTH, n, MEM_WIDTH), BF16)] * 2,
        compiler_params=pltpu.CompilerParams(vmem_limit_bytes=VMEM_LIMIT),
        name="memkv",
    )(mem2d, w_mem_k, w_mem_v)


def _mixer_a_kernel(x_ref, win_ref, wout_ref, sgg_ref, sgb_ref, sgw_ref, sgbias_ref,
                    mk_ref, mv_ref, g1_ref, b1_ref, h_ref):
    ts = x_ref.shape[0]
    x = x_ref[...]
    proj = _dot(x.astype(BF16), win_ref[...])
    u = _gelu(proj[:, :SG_WIDTH])
    gv = _gelu(proj[:, SG_WIDTH:2 * SG_WIDTH])
    gv = _layer_norm(gv, sgg_ref[...], sgb_ref[...]).astype(BF16)

    row = lax.broadcasted_iota(jnp.int32, (CHUNK, CHUNK), 0)
    col = lax.broadcasted_iota(jnp.int32, (CHUNK, CHUNK), 1)
    ws = [jnp.where(row >= col, sgw_ref[g], 0.0).astype(BF16) for g in range(N_SG)]
    n_chunks = ts // CHUNK
    low_half = (lax.broadcasted_iota(jnp.int32, (CHUNK, n_chunks * LANES), 1) % LANES) < HEAD_DIM
    gated = []
    for j in range(N_SG // 2):
        slabs = jnp.concatenate(
            [gv[c * CHUNK:(c + 1) * CHUNK, j * LANES:(j + 1) * LANES] for c in range(n_chunks)], -1)
        both = _dot(jnp.concatenate([ws[2 * j], ws[2 * j + 1]], 0), slabs)
        gated.append(jnp.where(low_half, both[:CHUNK], both[CHUNK:]))
    rows = [jnp.concatenate([g[:, c * LANES:(c + 1) * LANES] for g in gated], -1)
            + sgbias_ref[...] for c in range(n_chunks)]
    mix = u * jnp.concatenate(rows, 0)

    qm = (proj[:, 2 * SG_WIDTH:] * ATT_SCALE).astype(BF16)
    mo = _mem_attention(qm, mk_ref[...], mv_ref[...])
    att = _dot(jnp.concatenate([mix, mo], -1).astype(BF16), wout_ref[...])
    _to_chunks(h_ref, _layer_norm(ALPHA * x + att, g1_ref[...], b1_ref[...]))


def _mixer_a(x, w_in, w_out, sg_g, sg_b, sg_w, sg_bias, mk, mv, g1, b1):
    b, s, d = x.shape
    nts = s // TS_MIX
    m = mk.shape[0] // b
    full = lambda *shape: pl.BlockSpec(shape, lambda i, j: (0,) * len(shape))
    return pl.pallas_call(
        _mixer_a_kernel,
        grid=(b, nts),
        in_specs=[
            pl.BlockSpec((None, TS_MIX, d), lambda i, j: (i, j, 0)),
            full(d, 2 * SG_WIDTH + MEM_WIDTH),
            full(SG_WIDTH + MEM_WIDTH, d),
            full(1, SG_WIDTH),
            full(1, SG_WIDTH),
            full(N_SG, CHUNK, CHUNK),
            full(CHUNK, SG_WIDTH),
            pl.BlockSpec((m, MEM_WIDTH), lambda i, j: (i, 0)),
            pl.BlockSpec((m, MEM_WIDTH), lambda i, j: (i, 0)),
            full(1, d),
            full(1, d),
        ],
        out_specs=pl.BlockSpec((None, N_CHUNKS, TS_MIX, LANES), lambda i, j: (i, 0, j, 0)),
        out_shape=jax.ShapeDtypeStruct((b, N_CHUNKS, s, LANES), F32),
        compiler_params=pltpu.CompilerParams(vmem_limit_bytes=VMEM_LIMIT),
        name="mixer_a",
    )(x, w_in, w_out, sg_g, sg_b, sg_w, sg_bias, mk, mv, g1, b1)


def _router_kernel(h_ref, wr_ref, br_ref, pos_ref, gate_ref, runs_ref, counts_ref, base_ref,
                   wsplit_ref):
    tm = TM_ROUTE

    @pl.when(pl.program_id(0) == 0)
    def _():
        base_ref[...] = jnp.zeros_like(base_ref)
        w = wr_ref[...]
        w_hi = w.astype(BF16)
        wsplit_ref[...] = jnp.concatenate([w_hi, (w - w_hi.astype(F32)).astype(BF16)], -1)

    eid = lax.broadcasted_iota(jnp.int32, (N_EXPERTS, tm), 0)
    group = eid // EXPERTS_PER_GROUP

    def probabilities(k):
        h = jnp.concatenate([h_ref[c, pl.ds(k * tm, tm), :] for c in range(N_CHUNKS)], -1)
        h_hi = h.astype(BF16)
        h_lo = (h - h_hi.astype(F32)).astype(BF16)
        prod = _dot(jnp.concatenate([h_hi, h_lo], 0), wsplit_ref[...])
        logits = prod[:tm, :LANES] + (prod[:tm, LANES:] + prod[tm:, :LANES])
        lt = logits.T[:N_EXPERTS]
        ex = jnp.exp(lt - jnp.max(lt, 0, keepdims=True))
        probs = ex / jnp.sum(ex, 0, keepdims=True)
        return probs, probs + br_ref[...]

    def top2(sel, mask):
        v = jnp.where(mask, sel, -jnp.inf)
        m1 = jnp.max(v, 0, keepdims=True)
        i1 = jnp.min(jnp.where(v == m1, eid, N_EXPERTS), 0, keepdims=True)
        v2 = jnp.where(eid == i1, -jnp.inf, v)
        m2 = jnp.max(v2, 0, keepdims=True)
        i2 = jnp.min(jnp.where(v2 == m2, eid, N_EXPERTS), 0, keepdims=True)
        return m1, i1, m2, i2

    def best_group(sel):
        scores = []
        for g in range(N_EXPERT_GROUPS):
            m1, _, m2, _ = top2(sel, group == g)
            scores.append(m1 + m2)
        best = functools.reduce(jnp.maximum, scores)
        g_idx = jnp.full((1, tm), N_EXPERT_GROUPS - 1, jnp.int32)
        for g in reversed(range(N_EXPERT_GROUPS - 1)):
            g_idx = jnp.where(scores[g] == best, g, g_idx)
        return g_idx

    def choose(probs, sel, g_idx):
        _, e0, _, e1 = top2(sel, group == g_idx)
        hot0 = eid == e0
        hot1 = eid == e1
        p0 = jnp.sum(jnp.where(hot0, probs, 0.0), 0, keepdims=True)
        p1 = jnp.sum(jnp.where(hot1, probs, 0.0), 0, keepdims=True)
        return hot0, hot1, p0, p1

    earlier = jnp.where(lax.broadcasted_iota(jnp.int32, (tm, tm), 0)
                        < lax.broadcasted_iota(jnp.int32, (tm, tm), 1), 1.0, 0.0).astype(BF16)
    below = jnp.where(lax.broadcasted_iota(jnp.int32, (N_EXPERTS, N_EXPERTS), 1)
                      < lax.broadcasted_iota(jnp.int32, (N_EXPERTS, N_EXPERTS), 0),
                      1.0, 0.0).astype(BF16)

    def positions(hot0, hot1):
        hot = jnp.where(hot0 | hot1, 1.0, 0.0)
        hot_bf = hot.astype(BF16)
        before = _dot(hot_bf, earlier)
        run_start = jnp.sum(_dot(below, hot_bf), 1, keepdims=True)
        local = run_start + before
        q0 = jnp.sum(jnp.where(hot0, local, 0.0), 0, keepdims=True)
        q1 = jnp.sum(jnp.where(hot1, local, 0.0), 0, keepdims=True)
        return q0, q1, jnp.sum(hot, 1, keepdims=True)

    tiles = range(ROUTE_TILES)
    scored = [probabilities(k) for k in tiles]
    groups = [best_group(sel) for _, sel in scored]
    chosen = [choose(probs, sel, g_idx) for (probs, sel), g_idx in zip(scored, groups)]
    placed = [positions(hot0, hot1) for hot0, hot1, _, _ in chosen]

    row = lax.broadcasted_iota(jnp.int32, (SUBLANES, tm), 0)
    grow = lax.broadcasted_iota(jnp.int32, (LANES, tm), 0)
    lane = lax.broadcasted_iota(jnp.int32, (N_EXPERTS, LANES), 1)
    base = base_ref[...]
    for k, ((_, _, p0, p1), (q0, q1, count)) in enumerate(zip(chosen, placed)):
        psum = p0 + p1
        pos_ref[k] = jnp.where(row == 0, q0, jnp.where(row == 1, q1, 0.0)).astype(jnp.int32)
        cols = jnp.where(grow == 0, p0 / psum, jnp.where(grow == 1, p1 / psum, jnp.where(
            grow == 2, q0, jnp.where(grow == 3, q1, 0.0))))
        gate_ref[pl.ds(k * tm, tm), :] = cols.T[:, :2 * TOP_K]
        runs_ref[k] = jnp.where(lane == 0, count, jnp.where(lane == 1, base, 0.0)).astype(jnp.int32)
        base = base + count
    base_ref[...] = base
    counts_ref[...] = base.astype(jnp.int32)


def _router(h_cm, w_router, b_router):
    batch, _, seq, _ = h_cm.shape
    t = batch * seq
    tm = TM_ROUTE
    rows = ROUTE_TILES * tm
    return pl.pallas_call(
        _router_kernel,
        grid=(t // rows,),
        in_specs=[
            _chunk_tile_spec(rows, seq // rows),
            pl.BlockSpec((D_MODEL, LANES), lambda i: (0, 0)),
            pl.BlockSpec((N_EXPERTS, 1), lambda i: (0, 0)),
        ],
        out_specs=[
            pl.BlockSpec((ROUTE_TILES, SUBLANES, tm), lambda i: (i, 0, 0)),
            pl.BlockSpec((rows, 2 * TOP_K), lambda i: (i, 0)),
            pl.BlockSpec((ROUTE_TILES, N_EXPERTS, LANES), lambda i: (i, 0, 0)),
            pl.BlockSpec((N_EXPERTS, 1), lambda i: (0, 0)),
        ],
        out_shape=[
            jax.ShapeDtypeStruct((t // tm, SUBLANES, tm), jnp.int32),
            jax.ShapeDtypeStruct((t, 2 * TOP_K), F32),
            jax.ShapeDtypeStruct((t // tm, N_EXPERTS, LANES), jnp.int32),
            jax.ShapeDtypeStruct((N_EXPERTS, 1), jnp.int32),
        ],
        scratch_shapes=[pltpu.VMEM((N_EXPERTS, 1), F32), pltpu.VMEM((D_MODEL, 2 * LANES), BF16)],
        compiler_params=pltpu.CompilerParams(
            dimension_semantics=("arbitrary",), vmem_limit_bytes=VMEM_LIMIT),
        name="router",
    )(h_cm, w_router, b_router)


def _start_run_copies(runs_ref, start_ref, tile_ref, rows_ref, sem, to_rows):
    offset = 0
    for e in range(N_EXPERTS):
        count = runs_ref[e, 0]
        first_row = start_ref[e] + runs_ref[e, 1]
        for bit in reversed(range(RUN_BITS)):
            size = (1 << bit) * N_CHUNKS
            done = (count >> (bit + 1)) << (bit + 1)

            @pl.when(((count >> bit) & 1) == 1)
            def _():
                in_tile = tile_ref.at[
                    pl.ds(pl.multiple_of((offset + done) * N_CHUNKS, N_CHUNKS), size)]
                in_rows = rows_ref.at[
                    pl.ds(pl.multiple_of((first_row + done) * N_CHUNKS, N_CHUNKS), size)]
                if to_rows:
                    pltpu.make_async_copy(in_tile, in_rows, sem).start(priority=bit % 2)
                else:
                    pltpu.make_async_copy(in_rows, in_tile, sem).start(priority=bit % 2)
        offset = offset + count


def _wait_run_copies(tile_ref, rows_ref, sem):
    pltpu.make_async_copy(tile_ref, rows_ref.at[pl.ds(0, tile_ref.shape[0])], sem).wait()


def _dispatch_kernel(pos_ref, runs_ref, start_ref, end_ref, cnt_ref, h_ref, xs_ref,
                     zero_ref, sorted_ref, sem, zsem):
    i = pl.program_id(0)
    tm = TM_ROUTE

    @pl.when(i == 0)
    def _():
        zero_ref[...] = jnp.zeros_like(zero_ref)

        def zero_copy(first_row):
            first = pl.multiple_of(first_row * N_CHUNKS, N_CHUNKS)
            return pltpu.make_async_copy(
                zero_ref, xs_ref.at[pl.ds(first, TB_MOE * N_CHUNKS)], zsem)

        for wait in (False, True):
            for e in range(N_EXPERTS):
                @pl.when(cnt_ref[e, 0] > 0)
                def _():
                    cp = zero_copy(end_ref[e] - TB_MOE)
                    cp.wait() if wait else cp.start()

        def zero_block(blk, carry):
            cp = zero_copy(blk * TB_MOE)
            cp.start()
            cp.wait()
            return carry

        n_blocks = xs_ref.shape[0] // (TB_MOE * N_CHUNKS)
        lax.fori_loop(end_ref[N_EXPERTS - 1] // TB_MOE, n_blocks, zero_block, 0)

    p = lax.broadcasted_iota(jnp.int32, (TOP_K * tm, tm), 0)
    onehot = (p == pos_ref[0:1, :]) | (p == pos_ref[1:2, :])
    sorted_rows = _dot(jnp.where(onehot, 1.0, 0.0).astype(BF16),
                       _from_chunks(h_ref).astype(BF16))

    slot = i % 2
    buf = sorted_ref.at[slot]

    @pl.when(i >= 2)
    def _():
        _wait_run_copies(buf, xs_ref, sem.at[slot])

    _to_tiles(buf, sorted_rows, TOP_K * tm)
    _start_run_copies(runs_ref, start_ref, buf, xs_ref, sem.at[slot], to_rows=True)

    @pl.when(i == pl.num_programs(0) - 1)
    def _():
        _wait_run_copies(buf, xs_ref, sem.at[slot])
        _wait_run_copies(sorted_ref.at[1 - slot], xs_ref, sem.at[1 - slot])


def _dispatch(pos, runs, pad_starts, pad_ends, counts, h_cm, n_rows):
    nt = pos.shape[0]
    tm = TM_ROUTE
    seq = h_cm.shape[2]
    smem_vec = pl.BlockSpec(memory_space=pltpu.SMEM)
    return pl.pallas_call(
        _dispatch_kernel,
        grid=(nt,),
        in_specs=[
            pl.BlockSpec((None, SUBLANES, tm), lambda i: (i, 0, 0)),
            pl.BlockSpec((None, N_EXPERTS, LANES), lambda i: (i, 0, 0), memory_space=pltpu.SMEM),
            smem_vec, smem_vec, smem_vec,
            _chunk_tile_spec(tm, seq // tm)],
        out_specs=pl.BlockSpec(memory_space=pl.ANY),
        out_shape=jax.ShapeDtypeStruct((n_rows * N_CHUNKS, LANES), F32),
        scratch_shapes=[
            pltpu.VMEM((TB_MOE * N_CHUNKS, LANES), F32),
            pltpu.VMEM((2, TOP_K * tm * N_CHUNKS, LANES), F32),
            pltpu.SemaphoreType.DMA((2,)),
            pltpu.SemaphoreType.DMA(()),
        ],
        compiler_params=pltpu.CompilerParams(
            dimension_semantics=("arbitrary",), has_side_effects=True,
            vmem_limit_bytes=VMEM_LIMIT),
        name="dispatch",
    )(pos, runs, pad_starts, pad_ends, counts, h_cm)


def _experts_kernel(be_ref, nused_ref, valid_ref, xs_ref, wg_ref, wu_ref, wd_ref, ys_ref,
                    wg_bf, wu_bf, wd_bf):
    i = pl.program_id(0)
    nused = nused_ref[0]
    half = TB_MOE // 2

    def ffn(nrows):
        x = _from_tiles(xs_ref, nrows).astype(BF16)
        width = wg_bf.shape[1] // HIDDEN_SLICES
        cols = [slice(k * width, (k + 1) * width) for k in range(HIDDEN_SLICES)]
        acts = []
        for c in cols:
            hg = _dot(x, wg_bf[:, c])
            acts.append((hg * jax.nn.sigmoid(hg) * _dot(x, wu_bf[:, c])).astype(BF16))
        y = functools.reduce(jnp.add, [_dot(act, wd_bf[c, :]) for act, c in zip(acts, cols)])
        _to_tiles(ys_ref, y, nrows)

    @pl.when(i < nused)
    def _():
        prev = be_ref[jnp.maximum(i - 1, 0)]

        @pl.when((i == 0) | (be_ref[i] != prev))
        def _():
            wg_bf[...] = wg_ref[...].astype(BF16)
            wu_bf[...] = wu_ref[...].astype(BF16)
            wd_bf[...] = wd_ref[...].astype(BF16)

        @pl.when(valid_ref[i] > half)
        def _():
            ffn(TB_MOE)

        @pl.when(valid_ref[i] <= half)
        def _():
            ffn(half)
            padding = ys_ref.at[pl.ds(half * N_CHUNKS, half * N_CHUNKS)]
            padding[...] = jnp.zeros_like(padding)

    @pl.when(i >= nused)
    def _():
        ys_ref[...] = jnp.zeros_like(ys_ref)


def _experts(block_expert, nused, block_valid, xs, w_gate, w_up, w_down, layer):
    n_blocks = xs.shape[0] // (TB_MOE * N_CHUNKS)
    d, de = w_gate.shape[2], w_gate.shape[3]

    def x_map(i, be, nu, bv):
        return (jnp.minimum(i, nu[0] - 1), 0)

    def w_map(i, be, nu, bv):
        return (layer, be[jnp.minimum(i, nu[0] - 1)], 0, 0)

    return pl.pallas_call(
        _experts_kernel,
        grid_spec=pltpu.PrefetchScalarGridSpec(
            num_scalar_prefetch=3,
            grid=(n_blocks,),
            in_specs=[
                pl.BlockSpec((TB_MOE * N_CHUNKS, LANES), x_map),
                pl.BlockSpec((None, None, d, de), w_map),
                pl.BlockSpec((None, None, d, de), w_map),
                pl.BlockSpec((None, None, de, d), w_map),
            ],
            out_specs=pl.BlockSpec((TB_MOE * N_CHUNKS, LANES), lambda i, be, nu, bv: (i, 0)),
            scratch_shapes=[
                pltpu.VMEM((d, de), BF16),
                pltpu.VMEM((d, de), BF16),
                pltpu.VMEM((de, d), BF16),
            ],
        ),
        out_shape=jax.ShapeDtypeStruct(xs.shape, F32),
        compiler_params=pltpu.CompilerParams(
            dimension_semantics=("arbitrary",), vmem_limit_bytes=VMEM_LIMIT),
        name="experts",
    )(block_expert, nused, block_valid, xs, w_gate, w_up, w_down)


def _combine_kernel(runs_ref, next_runs_ref, start_ref, h_ref, gate_ref, g2_ref, b2_ref, ys_ref,
                    out_ref, sorted_ref, sem, *, chunk_major):
    tm = TM_ROUTE
    i = pl.program_id(0)
    slot = i % 2

    @pl.when(i == 0)
    def _():
        _start_run_copies(runs_ref, start_ref, sorted_ref.at[0], ys_ref, sem.at[0], to_rows=False)

    @pl.when(i + 1 < pl.num_programs(0))
    def _():
        _start_run_copies(next_runs_ref, start_ref, sorted_ref.at[1 - slot], ys_ref,
                          sem.at[1 - slot], to_rows=False)

    _wait_run_copies(sorted_ref.at[slot], ys_ref, sem.at[slot])

    gate = gate_ref[...]
    p = lax.broadcasted_iota(jnp.int32, (tm, TOP_K * tm), 1)
    onehot = jnp.concatenate([p == gate[:, 2:3].astype(jnp.int32),
                              p == gate[:, 3:4].astype(jnp.int32)], 0)
    picked = _dot(jnp.where(onehot, 1.0, 0.0).astype(BF16),
                  _from_tiles(sorted_ref.at[slot], TOP_K * tm).astype(BF16))
    ffn = picked[:tm] * gate[:, 0:1] + picked[tm:] * gate[:, 1:2]
    out = _layer_norm(ALPHA * _from_chunks(h_ref) + ffn, g2_ref[...], b2_ref[...])
    if chunk_major:
        _to_chunks(out_ref, out)
    else:
        out_ref[...] = out


def _combine(runs, pad_starts, h_cm, gate, g2, b2, ys, chunk_major):
    batch, _, seq, _ = h_cm.shape
    tm = TM_ROUTE
    t = batch * seq
    nt = t // tm
    nts = seq // tm
    if chunk_major:
        out_spec = _chunk_tile_spec(tm, nts)
        out_shape = jax.ShapeDtypeStruct((batch, N_CHUNKS, seq, LANES), F32)
    else:
        out_spec = pl.BlockSpec((None, tm, D_MODEL), lambda i: (i // nts, i % nts, 0))
        out_shape = jax.ShapeDtypeStruct((batch, seq, D_MODEL), F32)
    return pl.pallas_call(
        functools.partial(_combine_kernel, chunk_major=chunk_major),
        grid=(nt,),
        in_specs=[
            pl.BlockSpec((None, N_EXPERTS, LANES), lambda i: (i, 0, 0), memory_space=pltpu.SMEM),
            pl.BlockSpec((None, N_EXPERTS, LANES), lambda i: (jnp.minimum(i + 1, nt - 1), 0, 0),
                         memory_space=pltpu.SMEM),
            pl.BlockSpec(memory_space=pltpu.SMEM),
            _chunk_tile_spec(tm, nts),
            pl.BlockSpec((tm, 2 * TOP_K), lambda i: (i, 0)),
            pl.BlockSpec((1, D_MODEL), lambda i: (0, 0)),
            pl.BlockSpec((1, D_MODEL), lambda i: (0, 0)),
            pl.BlockSpec(memory_space=pl.ANY),
        ],
        out_specs=out_spec,
        out_shape=out_shape,
        scratch_shapes=[
            pltpu.VMEM((2, TOP_K * tm * N_CHUNKS, LANES), F32),
            pltpu.SemaphoreType.DMA((2,)),
        ],
        compiler_params=pltpu.CompilerParams(
            dimension_semantics=("arbitrary",), vmem_limit_bytes=VMEM_LIMIT),
        name="combine",
    )(runs, runs, pad_starts, h_cm, gate, g2, b2, ys)


def _moe(h_cm, w_router, b_router, w_gate, w_up, w_down, g2, b2, layer, chunk_major):
    t = h_cm.shape[0] * h_cm.shape[2]
    pos, gate, runs, counts = _router(h_cm, w_router, b_router)
    cnt = counts[:, 0]
    padded = (cnt + TB_MOE - 1) // TB_MOE * TB_MOE
    pad_ends = jnp.cumsum(padded).astype(jnp.int32)
    pad_starts = pad_ends - padded
    n_rows = t * TOP_K + N_EXPERTS * TB_MOE
    n_blocks = n_rows // TB_MOE
    block_start = jnp.arange(n_blocks, dtype=jnp.int32) * TB_MOE
    block_expert = jnp.minimum(
        jnp.sum(block_start[:, None] >= pad_ends[None, :], -1), N_EXPERTS - 1).astype(jnp.int32)
    nused = (pad_ends[-1:] // TB_MOE).astype(jnp.int32)
    block_valid = jnp.clip((pad_starts + cnt)[block_expert] - block_start, 0, TB_MOE)
    xs = _dispatch(pos, runs, pad_starts, pad_ends, counts, h_cm, n_rows)
    ys = _experts(block_expert, nused, block_valid.astype(jnp.int32), xs, w_gate, w_up, w_down,
                  layer)
    return _combine(runs, pad_starts, h_cm, gate, g2, b2, ys, chunk_major)


def _residue_rows(x_ref, dilation):
    n = x_ref.shape[1]
    if dilation == 1:
        return _from_chunks(x_ref)
    per = n // dilation
    return jnp.concatenate(
        [jnp.concatenate([x_ref[c, pl.ds(r, per, stride=dilation), :] for r in range(dilation)], 0)
         for c in range(N_CHUNKS)], -1)


def _proj_b_kernel(x_ref, wqkv_ref, wqm_ref, qkv0_ref, qkv1_ref, qkv2_ref, qm_ref):
    ts = x_ref.shape[1]
    qscale = jnp.where(
        lax.broadcasted_iota(jnp.int32, (1, 3 * DIL_OUT_WIDTH), 1) < DIL_OUT_WIDTH, ATT_SCALE, 1.0)
    for g, (out_ref, (_, dilation)) in enumerate(zip((qkv0_ref, qkv1_ref, qkv2_ref), DIL_PAIRS)):
        xb = _residue_rows(x_ref, dilation).astype(BF16)
        qkv = (_dot(xb, wqkv_ref[g]) * qscale).astype(BF16)
        per = ts // dilation
        for r in range(dilation):
            out_ref[r] = qkv[r * per:(r + 1) * per]
    qm = _dot(_from_chunks(x_ref).astype(BF16), wqm_ref[...])
    qm_ref[...] = (qm * ATT_SCALE).astype(BF16)


def _proj_b(x_cm, wqkv, wqm):
    b, _, s, _ = x_cm.shape
    ts = TS_MIX
    width = 3 * DIL_OUT_WIDTH
    out_specs, out_shapes = [], []
    for _, dilation in DIL_PAIRS:
        out_specs.append(pl.BlockSpec((None, dilation, ts // dilation, width),
                                      lambda i, j: (i, 0, j, 0)))
        out_shapes.append(jax.ShapeDtypeStruct((b, dilation, s // dilation, width), BF16))
    out_specs.append(pl.BlockSpec((None, ts, MEM_WIDTH), lambda i, j: (i, j, 0)))
    out_shapes.append(jax.ShapeDtypeStruct((b, s, MEM_WIDTH), BF16))
    return pl.pallas_call(
        _proj_b_kernel,
        grid=(b, s // ts),
        in_specs=[
            pl.BlockSpec((None, N_CHUNKS, ts, LANES), lambda i, j: (i, 0, j, 0)),
            pl.BlockSpec(wqkv.shape, lambda i, j: (0, 0, 0)),
            pl.BlockSpec(wqm.shape, lambda i, j: (0, 0)),
        ],
        out_specs=out_specs,
        out_shape=out_shapes,
        compiler_params=pltpu.CompilerParams(vmem_limit_bytes=VMEM_LIMIT),
        name="proj_b",
    )(x_cm, wqkv, wqm)


def _dil_attn_kernel(qkv_ref, o_ref, lse_ref, *, dilation, slopes):
    nb = qkv_ref.shape[1] // CHUNK
    width = DIL_OUT_WIDTH
    heads = HEADS_PER_DIL
    row = lax.broadcasted_iota(jnp.int32, (heads * CHUNK, 2 * CHUNK), 0)
    jk = lax.broadcasted_iota(jnp.int32, (heads * CHUNK, 2 * CHUNK), 1)
    rel = CHUNK + row % CHUNK - jk
    slope = functools.reduce(
        lambda acc, h: jnp.where(row // CHUNK == h, slopes[h], acc), range(1, heads), slopes[0])
    bias = jnp.where((rel >= 0) & (rel <= CHUNK), -slope * (dilation * rel).astype(F32), -jnp.inf)
    before_start = jk < CHUNK
    head = lax.broadcasted_iota(jnp.int32, (1, width), 1) // HEAD_DIM

    def scores(idx):
        r = idx // nb
        jb = idx % nb
        cur = pl.ds(pl.multiple_of(jb * CHUNK, CHUNK), CHUNK)
        prev = pl.ds(pl.multiple_of(jnp.maximum(jb - 1, 0) * CHUNK, CHUNK), CHUNK)
        q = qkv_ref[r, cur, 0:width]
        kband = jnp.concatenate([qkv_ref[r, prev, width:2 * width],
                                 qkv_ref[r, cur, width:2 * width]], 0)
        stacked = jnp.concatenate(
            [jnp.where(head == h, q, jnp.zeros_like(q)) for h in range(heads)], 0)
        s = _dot_nt(stacked, kband) + bias
        return r, cur, prev, jnp.where(before_start & (jb == 0), -jnp.inf, s)

    def softmax(s):
        m = jnp.max(s, -1, keepdims=True)
        p = jnp.exp(s - m)
        l = jnp.sum(p, -1, keepdims=True)
        return (p * (1.0 / l)).astype(BF16), m + jnp.log(l)

    def finish(r, cur, prev, p, row_lse):
        vband = jnp.concatenate([qkv_ref[r, prev, 2 * width:3 * width],
                                 qkv_ref[r, cur, 2 * width:3 * width]], 0)
        o = _dot(p, vband)
        out = jnp.zeros((CHUNK, width), F32)
        lse = jnp.zeros((CHUNK, width), F32)
        for h in range(heads):
            rows = slice(h * CHUNK, (h + 1) * CHUNK)
            out = jnp.where(head == h, o[rows], out)
            lse = jnp.where(head == h, row_lse[rows], lse)
        o_ref[r, cur, :] = out
        lse_ref[r, cur, :] = lse

    def group(i, carry):
        blocks = [scores(DIL_INTERLEAVE * i + j) for j in range(DIL_INTERLEAVE)]
        probs = [softmax(s) for _, _, _, s in blocks]
        for (r, cur, prev, _), (p, row_lse) in zip(blocks, probs):
            finish(r, cur, prev, p, row_lse)
        return carry

    lax.fori_loop(0, dilation * nb // DIL_INTERLEAVE, group, 0)


def _dil_attn(qkv, group):
    b, dilation, length, _ = qkv.shape
    slopes = tuple(2.0 ** (-8.0 * (group * HEADS_PER_DIL + h + 1) / N_DIL_HEADS)
                   for h in range(HEADS_PER_DIL))
    out_spec = pl.BlockSpec((None, dilation, length, DIL_OUT_WIDTH), lambda i: (i, 0, 0, 0))
    out_shape = jax.ShapeDtypeStruct((b, dilation, length, DIL_OUT_WIDTH), F32)
    return pl.pallas_call(
        functools.partial(_dil_attn_kernel, dilation=dilation, slopes=slopes),
        grid=(b,),
        in_specs=[pl.BlockSpec((None, dilation, length, 3 * DIL_OUT_WIDTH),
                               lambda i: (i, 0, 0, 0))],
        out_specs=[out_spec, out_spec],
        out_shape=[out_shape, out_shape],
        compiler_params=pltpu.CompilerParams(vmem_limit_bytes=VMEM_LIMIT),
        name=f"dil_attn_{group}",
    )(qkv)


def _token_order(src_ref, scratch, dilation):
    if dilation == 1:
        return src_ref[0]
    per = src_ref.shape[1]
    halves = DIL_OUT_WIDTH // LANES
    for r in range(dilation):
        v = src_ref[r]
        for c in range(halves):
            scratch[c, pl.ds(r, per, stride=dilation), :] = v[:, c * LANES:(c + 1) * LANES]
    return jnp.concatenate([scratch[c] for c in range(halves)], -1)


def _mixer_b_kernel(x_ref, qm_ref, o0_ref, l0_ref, o1_ref, l1_ref, o2_ref, l2_ref,
                    mk_ref, mv_ref, wout_ref, g1_ref, b1_ref, h_ref,
                    so1, sl1, so2, sl2):
    outs = [o0_ref[0], _token_order(o1_ref, so1, DIL_PAIRS[1][1]),
            _token_order(o2_ref, so2, DIL_PAIRS[2][1])]
    lses = [l0_ref[0], _token_order(l1_ref, sl1, DIL_PAIRS[1][1]),
            _token_order(l2_ref, sl2, DIL_PAIRS[2][1])]
    top = functools.reduce(jnp.maximum, lses)
    es = [jnp.exp(l - top) for l in lses]
    inv = 1.0 / functools.reduce(jnp.add, es)
    mix = functools.reduce(jnp.add, [e * inv * o for e, o in zip(es, outs)])
    mo = _mem_attention(qm_ref[...], mk_ref[...], mv_ref[...])
    att = _dot(jnp.concatenate([mix, mo], -1).astype(BF16), wout_ref[...])
    _to_chunks(h_ref, _layer_norm(ALPHA * _from_chunks(x_ref) + att, g1_ref[...], b1_ref[...]))


def _mixer_b(x_cm, qm, attn, mk, mv, w_out, g1, b1):
    b, _, s, _ = x_cm.shape
    ts = TS_MIX
    nts = s // ts
    m = mk.shape[0] // b
    in_specs = [
        pl.BlockSpec((None, N_CHUNKS, ts, LANES), lambda i, j: (i, 0, j, 0)),
        pl.BlockSpec((None, ts, MEM_WIDTH), lambda i, j: (i, j, 0)),
    ]
    args = [x_cm, qm]
    for (o, lse), (_, dilation) in zip(attn, DIL_PAIRS):
        spec = pl.BlockSpec((None, dilation, ts // dilation, DIL_OUT_WIDTH),
                            lambda i, j: (i, 0, j, 0))
        in_specs += [spec, spec]
        args += [o, lse]
    in_specs += [
        pl.BlockSpec((m, MEM_WIDTH), lambda i, j: (i, 0)),
        pl.BlockSpec((m, MEM_WIDTH), lambda i, j: (i, 0)),
        pl.BlockSpec(w_out.shape, lambda i, j: (0, 0)),
        pl.BlockSpec((1, D_MODEL), lambda i, j: (0, 0)),
        pl.BlockSpec((1, D_MODEL), lambda i, j: (0, 0)),
    ]
    args += [mk, mv, w_out, g1, b1]
    halves = DIL_OUT_WIDTH // LANES
    return pl.pallas_call(
        _mixer_b_kernel,
        grid=(b, nts),
        in_specs=in_specs,
        out_specs=pl.BlockSpec((None, N_CHUNKS, ts, LANES), lambda i, j: (i, 0, j, 0)),
        out_shape=jax.ShapeDtypeStruct((b, N_CHUNKS, s, LANES), F32),
        scratch_shapes=[pltpu.VMEM((halves, ts, LANES), F32)] * 4,
        compiler_params=pltpu.CompilerParams(vmem_limit_bytes=VMEM_LIMIT),
        name="mixer_b",
    )(*args)


def kernel(x, mem, w_in_a, w_out_a, sg_ln_g, sg_ln_b, sg_w, sg_b, w_in_b, w_out_b, w_k_shared,
           w_v_shared, w_mem_k, w_mem_v, ln1_g, ln1_b, ln2_g, ln2_b, w_router, b_router,
           w_gate, w_up, w_down):
    batch, seq, d = x.shape
    mk, mv = _memkv(mem.reshape(-1, d), w_mem_k, w_mem_v)
    row = lambda v: v.reshape(1, -1)

    sg_bias = jnp.repeat(sg_b[0].T, HEAD_DIM, axis=1)
    h = _mixer_a(x, w_in_a[0].astype(BF16), w_out_a[0].astype(BF16), row(sg_ln_g[0]),
                 row(sg_ln_b[0]), sg_w[0], sg_bias, mk[0], mv[0], row(ln1_g[0]), row(ln1_b[0]))
    w_router_pad = jnp.pad(w_router, ((0, 0), (0, LANES - N_EXPERTS)))
    b_router_col = b_router.reshape(N_EXPERTS, 1)
    x_cm = _moe(h, w_router_pad, b_router_col, w_gate, w_up, w_down,
                row(ln2_g[0]), row(ln2_b[0]), layer=0, chunk_major=True)

    wqkv = jnp.stack([
        jnp.concatenate([w[:, g * DIL_OUT_WIDTH:(g + 1) * DIL_OUT_WIDTH]
                         for w in (w_in_b[0], w_k_shared, w_v_shared)], -1)
        for g in range(len(DIL_PAIRS))]).astype(BF16)
    wqm = w_in_b[0][:, DIL_Q_WIDTH:].astype(BF16)
    qkv0, qkv1, qkv2, qm = _proj_b(x_cm, wqkv, wqm)
    attn = [_dil_attn(qkv, g) for g, qkv in enumerate((qkv0, qkv1, qkv2))]
    h = _mixer_b(x_cm, qm, attn, mk[1], mv[1], w_out_b[0].astype(BF16),
                 row(ln1_g[1]), row(ln1_b[1]))
    return _moe(h, w_router_pad, b_router_col, w_gate, w_up, w_down,
                row(ln2_g[1]), row(ln2_b[1]), layer=1, chunk_major=False)
```

```python
import functools
import math

import jax
import jax.numpy as jnp
from jax import lax
from jax.experimental import pallas as pl
from jax.experimental.pallas import tpu as pltpu

D_MODEL = 1024
HEAD_DIM = 64
CHUNK = 128
N_SG = 12
SG_WIDTH = N_SG * HEAD_DIM
DIL_PAIRS = ((128, 1), (512, 4), (2048, 16))
HEADS_PER_DIL = 4
N_DIL_HEADS = HEADS_PER_DIL * len(DIL_PAIRS)
DIL_Q_WIDTH = N_DIL_HEADS * HEAD_DIM
DIL_OUT_WIDTH = HEADS_PER_DIL * HEAD_DIM
MEM_HEADS = 4
MEM_WIDTH = MEM_HEADS * HEAD_DIM
N_EXPERTS = 16
N_EXPERT_GROUPS = 4
EXPERTS_PER_GROUP = N_EXPERTS // N_EXPERT_GROUPS
TOP_K = 2
DEPTH = 2
ALPHA = (2 * DEPTH) ** 0.25
LN_EPS = 1e-5
ATT_SCALE = 1.0 / math.sqrt(HEAD_DIM)

LANES = 128
SUBLANES = 8
N_CHUNKS = D_MODEL // LANES

TS_MIX = 1024
TS_PROJ = 512
MXU_DEPTH = 256
TM_ROUTE = MXU_DEPTH
RUN_BITS = TM_ROUTE.bit_length()
TB_MOE = 512
HIDDEN_SLICES = 2
ROUTE_TILES = 8
DIL_INTERLEAVE = 4
VMEM_LIMIT = 56 * 1024 * 1024

F32 = jnp.float32
BF16 = jnp.bfloat16


def _dot(a, b):
    return jnp.dot(a, b, preferred_element_type=F32)


def _dot_nt(a, b):
    return lax.dot_general(a, b, (((1,), (1,)), ((), ())), preferred_element_type=F32)


def _layer_norm(x, g, b):
    mu = jnp.mean(x, -1, keepdims=True)
    xc = x - mu
    var = jnp.mean(xc * xc, -1, keepdims=True)
    return xc * lax.rsqrt(var + LN_EPS) * g + b


def _gelu(x):
    return 0.5 * x * (1.0 + lax.erf(x * (1.0 / math.sqrt(2.0))))


def _from_tiles(ref, n):
    return jnp.concatenate(
        [ref[pl.ds(c, n, stride=N_CHUNKS), :] for c in range(N_CHUNKS)], -1)


def _to_tiles(ref, val, n):
    for c in range(N_CHUNKS):
        ref[pl.ds(c, n, stride=N_CHUNKS), :] = val[:, c * LANES:(c + 1) * LANES]


def _from_chunks(ref):
    return jnp.concatenate([ref[c] for c in range(N_CHUNKS)], -1)


def _to_chunks(ref, val):
    for c in range(N_CHUNKS):
        ref[c] = val[:, c * LANES:(c + 1) * LANES]


def _chunk_tile_spec(rows, tiles_per_seq):
    return pl.BlockSpec((None, N_CHUNKS, rows, LANES),
                        lambda i: (i // tiles_per_seq, 0, i % tiles_per_seq, 0))


def _mem_attention(q, mk, mv):
    n = q.shape[0]
    head = lax.broadcasted_iota(jnp.int32, (1, MEM_WIDTH), 1) // HEAD_DIM
    stacked = jnp.concatenate(
        [jnp.where(head == h, q, jnp.zeros_like(q)) for h in range(MEM_HEADS)], 0)
    s = _dot_nt(stacked, mk)
    m = jnp.max(s, -1, keepdims=True)
    p = jnp.exp(s - m)
    l = jnp.sum(p, -1, keepdims=True)
    o = _dot((p * (1.0 / l)).astype(BF16), mv)
    out = jnp.zeros((n, MEM_WIDTH), F32)
    for h in range(MEM_HEADS):
        out = jnp.where(head == h, o[h * n:(h + 1) * n], out)
    return out


def _memkv_kernel(mem_ref, wk_ref, wv_ref, mk_ref, mv_ref):
    m = mem_ref[...].astype(BF16)
    mk_ref[...] = _dot(m, wk_ref[...].astype(BF16)).astype(BF16)
    mv_ref[...] = _dot(m, wv_ref[...].astype(BF16)).astype(BF16)


def _memkv(mem2d, w_mem_k, w_mem_v):
    n = mem2d.shape[0]
    return pl.pallas_call(
        _memkv_kernel,
        grid=(DEPTH,),
        in_specs=[
            pl.BlockSpec((n, D_MODEL), lambda l: (0, 0)),
            pl.BlockSpec((None, D_MODEL, MEM_WIDTH), lambda l: (l, 0, 0)),
            pl.BlockSpec((None, D_MODEL, MEM_WIDTH), lambda l: (l, 0, 0)),
        ],
        out_specs=[
            pl.BlockSpec((None, n, MEM_WIDTH), lambda l: (l, 0, 0)),
            pl.BlockSpec((None, n, MEM_WIDTH), lambda l: (l, 0, 0)),
        ],
        out_shape=[jax.ShapeDtypeStruct((DEPTH, n, MEM_WIDTH), BF16)] * 2,
        compiler_params=pltpu.CompilerParams(vmem_limit_bytes=VMEM_LIMIT),
        name="memkv",
    )(mem2d, w_mem_k, w_mem_v)


def _mixer_a_kernel(x_ref, win_ref, wout_ref, sgg_ref, sgb_ref, sgw_ref, sgbias_ref,
                    mk_ref, mv_ref, g1_ref, b1_ref, h_ref):
    ts = x_ref.shape[0]
    x = x_ref[...]
    proj = _dot(x.astype(BF16), win_ref[...])
    u = _gelu(proj[:, :SG_WIDTH])
    gv = _gelu(proj[:, SG_WIDTH:2 * SG_WIDTH])
    gv = _layer_norm(gv, sgg_ref[...], sgb_ref[...]).astype(BF16)

    row = lax.broadcasted_iota(jnp.int32, (CHUNK, CHUNK), 0)
    col = lax.broadcasted_iota(jnp.int32, (CHUNK, CHUNK), 1)
    ws = [jnp.where(row >= col, sgw_ref[g], 0.0).astype(BF16) for g in range(N_SG)]
    n_chunks = ts // CHUNK
    low_half = (lax.broadcasted_iota(jnp.int32, (CHUNK, n_chunks * LANES), 1) % LANES) < HEAD_DIM
    gated = []
    for j in range(N_SG // 2):
        slabs = jnp.concatenate(
            [gv[c * CHUNK:(c + 1) * CHUNK, j * LANES:(j + 1) * LANES] for c in range(n_chunks)], -1)
        both = _dot(jnp.concatenate([ws[2 * j], ws[2 * j + 1]], 0), slabs)
        gated.append(jnp.where(low_half, both[:CHUNK], both[CHUNK:]))
    rows = [jnp.concatenate([g[:, c * LANES:(c + 1) * LANES] for g in gated], -1)
            + sgbias_ref[...] for c in range(n_chunks)]
    mix = u * jnp.concatenate(rows, 0)

    qm = (proj[:, 2 * SG_WIDTH:] * ATT_SCALE).astype(BF16)
    mo = _mem_attention(qm, mk_ref[...], mv_ref[...])
    att = _dot(jnp.concatenate([mix, mo], -1).astype(BF16), wout_ref[...])
    _to_chunks(h_ref, _layer_norm(ALPHA * x + att, g1_ref[...], b1_ref[...]))


def _mixer_a(x, w_in, w_out, sg_g, sg_b, sg_w, sg_bias, mk, mv, g1, b1):
    b, s, d = x.shape
    nts = s // TS_MIX
    m = mk.shape[0] // b
    full = lambda *shape: pl.BlockSpec(shape, lambda i, j: (0,) * len(shape))
    return pl.pallas_call(
        _mixer_a_kernel,
        grid=(b, nts),
        in_specs=[
            pl.BlockSpec((None, TS_MIX, d), lambda i, j: (i, j, 0)),
            full(d, 2 * SG_WIDTH + MEM_WIDTH),
            full(SG_WIDTH + MEM_WIDTH, d),
            full(1, SG_WIDTH),
            full(1, SG_WIDTH),
            full(N_SG, CHUNK, CHUNK),
            full(CHUNK, SG_WIDTH),
            pl.BlockSpec((m, MEM_WIDTH), lambda i, j: (i, 0)),
            pl.BlockSpec((m, MEM_WIDTH), lambda i, j: (i, 0)),
            full(1, d),
            full(1, d),
        ],
        out_specs=pl.BlockSpec((None, N_CHUNKS, TS_MIX, LANES), lambda i, j: (i, 0, j, 0)),
        out_shape=jax.ShapeDtypeStruct((b, N_CHUNKS, s, LANES), F32),
        compiler_params=pltpu.CompilerParams(vmem_limit_bytes=VMEM_LIMIT),
        name="mixer_a",
    )(x, w_in, w_out, sg_g, sg_b, sg_w, sg_bias, mk, mv, g1, b1)


def _router_kernel(h_ref, wr_ref, br_ref, pos_ref, gate_ref, runs_ref, counts_ref, base_ref,
                   wsplit_ref):
    tm = TM_ROUTE

    @pl.when(pl.program_id(0) == 0)
    def _():
        base_ref[...] = jnp.zeros_like(base_ref)
        w = wr_ref[...]
        w_hi = w.astype(BF16)
        wsplit_ref[...] = jnp.concatenate([w_hi, (w - w_hi.astype(F32)).astype(BF16)], -1)

    eid = lax.broadcasted_iota(jnp.int32, (N_EXPERTS, tm), 0)
    group = eid // EXPERTS_PER_GROUP

    def probabilities(k):
        h = jnp.concatenate([h_ref[c, pl.ds(k * tm, tm), :] for c in range(N_CHUNKS)], -1)
        h_hi = h.astype(BF16)
        h_lo = (h - h_hi.astype(F32)).astype(BF16)
        prod = _dot(jnp.concatenate([h_hi, h_lo], 0), wsplit_ref[...])
        logits = prod[:tm, :LANES] + (prod[:tm, LANES:] + prod[tm:, :LANES])
        lt = logits.T[:N_EXPERTS]
        ex = jnp.exp(lt - jnp.max(lt, 0, keepdims=True))
        probs = ex / jnp.sum(ex, 0, keepdims=True)
        return probs, probs + br_ref[...]

    def top2(sel, mask):
        v = jnp.where(mask, sel, -jnp.inf)
        m1 = jnp.max(v, 0, keepdims=True)
        i1 = jnp.min(jnp.where(v == m1, eid, N_EXPERTS), 0, keepdims=True)
        v2 = jnp.where(eid == i1, -jnp.inf, v)
        m2 = jnp.max(v2, 0, keepdims=True)
        i2 = jnp.min(jnp.where(v2 == m2, eid, N_EXPERTS), 0, keepdims=True)
        return m1, i1, m2, i2

    def best_group(sel):
        scores = []
        for g in range(N_EXPERT_GROUPS):
            m1, _, m2, _ = top2(sel, group == g)
            scores.append(m1 + m2)
        best = functools.reduce(jnp.maximum, scores)
        g_idx = jnp.full((1, tm), N_EXPERT_GROUPS - 1, jnp.int32)
        for g in reversed(range(N_EXPERT_GROUPS - 1)):
            g_idx = jnp.where(scores[g] == best, g, g_idx)
        return g_idx

    def choose(probs, sel, g_idx):
        _, e0, _, e1 = top2(sel, group == g_idx)
        hot0 = eid == e0
        hot1 = eid == e1
        p0 = jnp.sum(jnp.where(hot0, probs, 0.0), 0, keepdims=True)
        p1 = jnp.sum(jnp.where(hot1, probs, 0.0), 0, keepdims=True)
        return hot0, hot1, p0, p1

    earlier = jnp.where(lax.broadcasted_iota(jnp.int32, (tm, tm), 0)
                        < lax.broadcasted_iota(jnp.int32, (tm, tm), 1), 1.0, 0.0).astype(BF16)
    below = jnp.where(lax.broadcasted_iota(jnp.int32, (N_EXPERTS, N_EXPERTS), 1)
                      < lax.broadcasted_iota(jnp.int32, (N_EXPERTS, N_EXPERTS), 0),
                      1.0, 0.0).astype(BF16)

    def positions(hot0, hot1):
        hot = jnp.where(hot0 | hot1, 1.0, 0.0)
        hot_bf = hot.astype(BF16)
        before = _dot(hot_bf, earlier)
        run_start = jnp.sum(_dot(below, hot_bf), 1, keepdims=True)
        local = run_start + before
        q0 = jnp.sum(jnp.where(hot0, local, 0.0), 0, keepdims=True)
        q1 = jnp.sum(jnp.where(hot1, local, 0.0), 0, keepdims=True)
        return q0, q1, jnp.sum(hot, 1, keepdims=True)

    tiles = range(ROUTE_TILES)
    scored = [probabilities(k) for k in tiles]
    groups = [best_group(sel) for _, sel in scored]
    chosen = [choose(probs, sel, g_idx) for (probs, sel), g_idx in zip(scored, groups)]
    placed = [positions(hot0, hot1) for hot0, hot1, _, _ in chosen]

    row = lax.broadcasted_iota(jnp.int32, (SUBLANES, tm), 0)
    grow = lax.broadcasted_iota(jnp.int32, (LANES, tm), 0)
    lane = lax.broadcasted_iota(jnp.int32, (N_EXPERTS, LANES), 1)
    base = base_ref[...]
    for k, ((_, _, p0, p1), (q0, q1, count)) in enumerate(zip(chosen, placed)):
        psum = p0 + p1
        pos_ref[k] = jnp.where(row == 0, q0, jnp.where(row == 1, q1, 0.0)).astype(jnp.int32)
        cols = jnp.where(grow == 0, p0 / psum, jnp.where(grow == 1, p1 / psum, jnp.where(
            grow == 2, q0, jnp.where(grow == 3, q1, 0.0))))
        gate_ref[pl.ds(k * tm, tm), :] = cols.T[:, :2 * TOP_K]
        runs_ref[k] = jnp.where(lane == 0, count, jnp.where(lane == 1, base, 0.0)).astype(jnp.int32)
        base = base + count
    base_ref[...] = base
    counts_ref[...] = base.astype(jnp.int32)


def _router(h_cm, w_router, b_router):
    batch, _, seq, _ = h_cm.shape
    t = batch * seq
    tm = TM_ROUTE
    rows = ROUTE_TILES * tm
    return pl.pallas_call(
        _router_kernel,
        grid=(t // rows,),
        in_specs=[
            _chunk_tile_spec(rows, seq // rows),
            pl.BlockSpec((D_MODEL, LANES), lambda i: (0, 0)),
            pl.BlockSpec((N_EXPERTS, 1), lambda i: (0, 0)),
        ],
        out_specs=[
            pl.BlockSpec((ROUTE_TILES, SUBLANES, tm), lambda i: (i, 0, 0)),
            pl.BlockSpec((rows, 2 * TOP_K), lambda i: (i, 0)),
            pl.BlockSpec((ROUTE_TILES, N_EXPERTS, LANES), lambda i: (i, 0, 0)),
            pl.BlockSpec((N_EXPERTS, 1), lambda i: (0, 0)),
        ],
        out_shape=[
            jax.ShapeDtypeStruct((t // tm, SUBLANES, tm), jnp.int32),
            jax.ShapeDtypeStruct((t, 2 * TOP_K), F32),
            jax.ShapeDtypeStruct((t // tm, N_EXPERTS, LANES), jnp.int32),
            jax.ShapeDtypeStruct((N_EXPERTS, 1), jnp.int32),
        ],
        scratch_shapes=[pltpu.VMEM((N_EXPERTS, 1), F32), pltpu.VMEM((D_MODEL, 2 * LANES), BF16)],
        compiler_params=pltpu.CompilerParams(
            dimension_semantics=("arbitrary",), vmem_limit_bytes=VMEM_LIMIT),
        name="router",
    )(h_cm, w_router, b_router)


def _start_run_copies(runs_ref, start_ref, tile_ref, rows_ref, sem, to_rows):
    offset = 0
    for e in range(N_EXPERTS):
        count = runs_ref[e, 0]
        first_row = start_ref[e] + runs_ref[e, 1]
        for bit in reversed(range(RUN_BITS)):
            size = (1 << bit) * N_CHUNKS
            done = (count >> (bit + 1)) << (bit + 1)

            @pl.when(((count >> bit) & 1) == 1)
            def _():
                in_tile = tile_ref.at[
                    pl.ds(pl.multiple_of((offset + done) * N_CHUNKS, N_CHUNKS), size)]
                in_rows = rows_ref.at[
                    pl.ds(pl.multiple_of((first_row + done) * N_CHUNKS, N_CHUNKS), size)]
                if to_rows:
                    pltpu.make_async_copy(in_tile, in_rows, sem).start(priority=bit % 2)
                else:
                    pltpu.make_async_copy(in_rows, in_tile, sem).start(priority=bit % 2)
        offset = offset + count


def _wait_run_copies(tile_ref, rows_ref, sem):
    pltpu.make_async_copy(tile_ref, rows_ref.at[pl.ds(0, tile_ref.shape[0])], sem).wait()


def _dispatch_kernel(pos_ref, runs_ref, start_ref, end_ref, cnt_ref, h_ref, xs_ref,
                     zero_ref, sorted_ref, sem, zsem):
    i = pl.program_id(0)
    tm = TM_ROUTE

    @pl.when(i == 0)
    def _():
        zero_ref[...] = jnp.zeros_like(zero_ref)

        def zero_copy(first_row):
            first = pl.multiple_of(first_row * N_CHUNKS, N_CHUNKS)
            return pltpu.make_async_copy(
                zero_ref, xs_ref.at[pl.ds(first, TB_MOE * N_CHUNKS)], zsem)

        for wait in (False, True):
            for e in range(N_EXPERTS):
                @pl.when(cnt_ref[e, 0] > 0)
                def _():
                    cp = zero_copy(end_ref[e] - TB_MOE)
                    cp.wait() if wait else cp.start()

        def zero_block(blk, carry):
            cp = zero_copy(blk * TB_MOE)
            cp.start()
            cp.wait()
            return carry

        n_blocks = xs_ref.shape[0] // (TB_MOE * N_CHUNKS)
        lax.fori_loop(end_ref[N_EXPERTS - 1] // TB_MOE, n_blocks, zero_block, 0)

    p = lax.broadcasted_iota(jnp.int32, (TOP_K * tm, tm), 0)
    onehot = (p == pos_ref[0:1, :]) | (p == pos_ref[1:2, :])
    sorted_rows = _dot(jnp.where(onehot, 1.0, 0.0).astype(BF16),
                       _from_chunks(h_ref).astype(BF16))

    slot = i % 2
    buf = sorted_ref.at[slot]

    @pl.when(i >= 2)
    def _():
        _wait_run_copies(buf, xs_ref, sem.at[slot])

    _to_tiles(buf, sorted_rows, TOP_K * tm)
    _start_run_copies(runs_ref, start_ref, buf, xs_ref, sem.at[slot], to_rows=True)

    @pl.when(i == pl.num_programs(0) - 1)
    def _():
        _wait_run_copies(buf, xs_ref, sem.at[slot])
        _wait_run_copies(sorted_ref.at[1 - slot], xs_ref, sem.at[1 - slot])


def _dispatch(pos, runs, pad_starts, pad_ends, counts, h_cm, n_rows):
    nt = pos.shape[0]
    tm = TM_ROUTE
    seq = h_cm.shape[2]
    smem_vec = pl.BlockSpec(memory_space=pltpu.SMEM)
    return pl.pallas_call(
        _dispatch_kernel,
        grid=(nt,),
        in_specs=[
            pl.BlockSpec((None, SUBLANES, tm), lambda i: (i, 0, 0)),
            pl.BlockSpec((None, N_EXPERTS, LANES), lambda i: (i, 0, 0), memory_space=pltpu.SMEM),
            smem_vec, smem_vec, smem_vec,
            _chunk_tile_spec(tm, seq // tm)],
        out_specs=pl.BlockSpec(memory_space=pl.ANY),
        out_shape=jax.ShapeDtypeStruct((n_rows * N_CHUNKS, LANES), F32),
        scratch_shapes=[
            pltpu.VMEM((TB_MOE * N_CHUNKS, LANES), F32),
            pltpu.VMEM((2, TOP_K * tm * N_CHUNKS, LANES), F32),
            pltpu.SemaphoreType.DMA((2,)),
            pltpu.SemaphoreType.DMA(()),
        ],
        compiler_params=pltpu.CompilerParams(
            dimension_semantics=("arbitrary",), has_side_effects=True,
            vmem_limit_bytes=VMEM_LIMIT),
        name="dispatch",
    )(pos, runs, pad_starts, pad_ends, counts, h_cm)


def _experts_kernel(be_ref, nused_ref, valid_ref, xs_ref, wg_ref, wu_ref, wd_ref, ys_ref,
                    wg_bf, wu_bf, wd_bf):
    i = pl.program_id(0)
    nused = nused_ref[0]
    half = TB_MOE // 2

    def ffn(nrows):
        x = _from_tiles(xs_ref, nrows).astype(BF16)
        width = wg_bf.shape[1] // HIDDEN_SLICES
        cols = [slice(k * width, (k + 1) * width) for k in range(HIDDEN_SLICES)]
        acts = []
        for c in cols:
            hg = _dot(x, wg_bf[:, c])
            acts.append((hg * jax.nn.sigmoid(hg) * _dot(x, wu_bf[:, c])).astype(BF16))
        y = functools.reduce(jnp.add, [_dot(act, wd_bf[c, :]) for act, c in zip(acts, cols)])
        _to_tiles(ys_ref, y, nrows)

    @pl.when(i < nused)
    def _():
        prev = be_ref[jnp.maximum(i - 1, 0)]

        @pl.when((i == 0) | (be_ref[i] != prev))
        def _():
            wg_bf[...] = wg_ref[...].astype(BF16)
            wu_bf[...] = wu_ref[...].astype(BF16)
            wd_bf[...] = wd_ref[...].astype(BF16)

        @pl.when(valid_ref[i] > half)
        def _():
            ffn(TB_MOE)

        @pl.when(valid_ref[i] <= half)
        def _():
            ffn(half)
            padding = ys_ref.at[pl.ds(half * N_CHUNKS, half * N_CHUNKS)]
            padding[...] = jnp.zeros_like(padding)

    @pl.when(i >= nused)
    def _():
        ys_ref[...] = jnp.zeros_like(ys_ref)


def _experts(block_expert, nused, block_valid, xs, w_gate, w_up, w_down, layer):
    n_blocks = xs.shape[0] // (TB_MOE * N_CHUNKS)
    d, de = w_gate.shape[2], w_gate.shape[3]

    def x_map(i, be, nu, bv):
        return (jnp.minimum(i, nu[0] - 1), 0)

    def w_map(i, be, nu, bv):
        return (layer, be[jnp.minimum(i, nu[0] - 1)], 0, 0)

    return pl.pallas_call(
        _experts_kernel,
        grid_spec=pltpu.PrefetchScalarGridSpec(
            num_scalar_prefetch=3,
            grid=(n_blocks,),
            in_specs=[
                pl.BlockSpec((TB_MOE * N_CHUNKS, LANES), x_map),
                pl.BlockSpec((None, None, d, de), w_map),
                pl.BlockSpec((None, None, d, de), w_map),
                pl.BlockSpec((None, None, de, d), w_map),
            ],
            out_specs=pl.BlockSpec((TB_MOE * N_CHUNKS, LANES), lambda i, be, nu, bv: (i, 0)),
            scratch_shapes=[
                pltpu.VMEM((d, de), BF16),
                pltpu.VMEM((d, de), BF16),
                pltpu.VMEM((de, d), BF16),
            ],
        ),
        out_shape=jax.ShapeDtypeStruct(xs.shape, F32),
        compiler_params=pltpu.CompilerParams(
            dimension_semantics=("arbitrary",), vmem_limit_bytes=VMEM_LIMIT),
        name="experts",
    )(block_expert, nused, block_valid, xs, w_gate, w_up, w_down)


def _combine_kernel(runs_ref, next_runs_ref, start_ref, h_ref, gate_ref, g2_ref, b2_ref, ys_ref,
                    out_ref, sorted_ref, sem, *, chunk_major):
    tm = TM_ROUTE
    i = pl.program_id(0)
    slot = i % 2

    @pl.when(i == 0)
    def _():
        _start_run_copies(runs_ref, start_ref, sorted_ref.at[0], ys_ref, sem.at[0], to_rows=False)

    @pl.when(i + 1 < pl.num_programs(0))
    def _():
        _start_run_copies(next_runs_ref, start_ref, sorted_ref.at[1 - slot], ys_ref,
                          sem.at[1 - slot], to_rows=False)

    _wait_run_copies(sorted_ref.at[slot], ys_ref, sem.at[slot])

    gate = gate_ref[...]
    p = lax.broadcasted_iota(jnp.int32, (tm, TOP_K * tm), 1)
    onehot = jnp.concatenate([p == gate[:, 2:3].astype(jnp.int32),
                              p == gate[:, 3:4].astype(jnp.int32)], 0)
    picked = _dot(jnp.where(onehot, 1.0, 0.0).astype(BF16),
                  _from_tiles(sorted_ref.at[slot], TOP_K * tm).astype(BF16))
    ffn = picked[:tm] * gate[:, 0:1] + picked[tm:] * gate[:, 1:2]
    out = _layer_norm(ALPHA * _from_chunks(h_ref) + ffn, g2_ref[...], b2_ref[...])
    if chunk_major:
        _to_chunks(out_ref, out)
    else:
        out_ref[...] = out


def _combine(runs, pad_starts, h_cm, gate, g2, b2, ys, chunk_major):
    batch, _, seq, _ = h_cm.shape
    tm = TM_ROUTE
    t = batch * seq
    nt = t // tm
    nts = seq // tm
    if chunk_major:
        out_spec = _chunk_tile_spec(tm, nts)
        out_shape = jax.ShapeDtypeStruct((batch, N_CHUNKS, seq, LANES), F32)
    else:
        out_spec = pl.BlockSpec((None, tm, D_MODEL), lambda i: (i // nts, i % nts, 0))
        out_shape = jax.ShapeDtypeStruct((batch, seq, D_MODEL), F32)
    return pl.pallas_call(
        functools.partial(_combine_kernel, chunk_major=chunk_major),
        grid=(nt,),
        in_specs=[
            pl.BlockSpec((None, N_EXPERTS, LANES), lambda i: (i, 0, 0), memory_space=pltpu.SMEM),
            pl.BlockSpec((None, N_EXPERTS, LANES), lambda i: (jnp.minimum(i + 1, nt - 1), 0, 0),
                         memory_space=pltpu.SMEM),
            pl.BlockSpec(memory_space=pltpu.SMEM),
            _chunk_tile_spec(tm, nts),
            pl.BlockSpec((tm, 2 * TOP_K), lambda i: (i, 0)),
            pl.BlockSpec((1, D_MODEL), lambda i: (0, 0)),
            pl.BlockSpec((1, D_MODEL), lambda i: (0, 0)),
            pl.BlockSpec(memory_space=pl.ANY),
        ],
        out_specs=out_spec,
        out_shape=out_shape,
        scratch_shapes=[
            pltpu.VMEM((2, TOP_K * tm * N_CHUNKS, LANES), F32),
            pltpu.SemaphoreType.DMA((2,)),
        ],
        compiler_params=pltpu.CompilerParams(
            dimension_semantics=("arbitrary",), vmem_limit_bytes=VMEM_LIMIT),
        name="combine",
    )(runs, runs, pad_starts, h_cm, gate, g2, b2, ys)


def _moe(h_cm, w_router, b_router, w_gate, w_up, w_down, g2, b2, layer, chunk_major):
    t = h_cm.shape[0] * h_cm.shape[2]
    pos, gate, runs, counts = _router(h_cm, w_router, b_router)
    cnt = counts[:, 0]
    padded = (cnt + TB_MOE - 1) // TB_MOE * TB_MOE
    pad_ends = jnp.cumsum(padded).astype(jnp.int32)
    pad_starts = pad_ends - padded
    n_rows = t * TOP_K + N_EXPERTS * TB_MOE
    n_blocks = n_rows // TB_MOE
    block_start = jnp.arange(n_blocks, dtype=jnp.int32) * TB_MOE
    block_expert = jnp.minimum(
        jnp.sum(block_start[:, None] >= pad_ends[None, :], -1), N_EXPERTS - 1).astype(jnp.int32)
    nused = (pad_ends[-1:] // TB_MOE).astype(jnp.int32)
    block_valid = jnp.clip((pad_starts + cnt)[block_expert] - block_start, 0, TB_MOE)
    xs = _dispatch(pos, runs, pad_starts, pad_ends, counts, h_cm, n_rows)
    ys = _experts(block_expert, nused, block_valid.astype(jnp.int32), xs, w_gate, w_up, w_down,
                  layer)
    return _combine(runs, pad_starts, h_cm, gate, g2, b2, ys, chunk_major)


def _residue_rows(x_ref, dilation):
    n = x_ref.shape[1]
    if dilation == 1:
        return _from_chunks(x_ref)
    per = n // dilation
    return jnp.concatenate(
        [jnp.concatenate([x_ref[c, pl.ds(r, per, stride=dilation), :] for r in range(dilation)], 0)
         for c in range(N_CHUNKS)], -1)


def _proj_b_kernel(x_ref, wqkv_ref, wqm_ref, qkv0_ref, qkv1_ref, qkv2_ref, qm_ref):
    ts = x_ref.shape[1]
    qscale = jnp.where(
        lax.broadcasted_iota(jnp.int32, (1, 3 * DIL_OUT_WIDTH), 1) < DIL_OUT_WIDTH, ATT_SCALE, 1.0)
    for g, (out_ref, (_, dilation)) in enumerate(zip((qkv0_ref, qkv1_ref, qkv2_ref), DIL_PAIRS)):
        xb = _residue_rows(x_ref, dilation).astype(BF16)
        qkv = (_dot(xb, wqkv_ref[g]) * qscale).astype(BF16)
        per = ts // dilation
        for r in range(dilation):
            out_ref[r] = qkv[r * per:(r + 1) * per]
    qm = _dot(_from_chunks(x_ref).astype(BF16), wqm_ref[...])
    qm_ref[...] = (qm * ATT_SCALE).astype(BF16)


def _proj_b(x_cm, wqkv, wqm):
    b, _, s, _ = x_cm.shape
    ts = TS_PROJ
    width = 3 * DIL_OUT_WIDTH
    out_specs, out_shapes = [], []
    for _, dilation in DIL_PAIRS:
        out_specs.append(pl.BlockSpec((None, dilation, ts // dilation, width),
                                      lambda i, j: (i, 0, j, 0)))
        out_shapes.append(jax.ShapeDtypeStruct((b, dilation, s // dilation, width), BF16))
    out_specs.append(pl.BlockSpec((None, ts, MEM_WIDTH), lambda i, j: (i, j, 0)))
    out_shapes.append(jax.ShapeDtypeStruct((b, s, MEM_WIDTH), BF16))
    return pl.pallas_call(
        _proj_b_kernel,
        grid=(b, s // ts),
        in_specs=[
            pl.BlockSpec((None, N_CHUNKS, ts, LANES), lambda i, j: (i, 0, j, 0)),
            pl.BlockSpec(wqkv.shape, lambda i, j: (0, 0, 0)),
            pl.BlockSpec(wqm.shape, lambda i, j: (0, 0)),
        ],
        out_specs=out_specs,
        out_shape=out_shapes,
        compiler_params=pltpu.CompilerParams(vmem_limit_bytes=VMEM_LIMIT),
        name="proj_b",
    )(x_cm, wqkv, wqm)


def _dil_attn_kernel(qkv_ref, o_ref, lse_ref, *, dilation, slopes):
    nb = qkv_ref.shape[1] // CHUNK
    width = DIL_OUT_WIDTH
    heads = HEADS_PER_DIL
    row = lax.broadcasted_iota(jnp.int32, (heads * CHUNK, 2 * CHUNK), 0)
    jk = lax.broadcasted_iota(jnp.int32, (heads * CHUNK, 2 * CHUNK), 1)
    rel = CHUNK + row % CHUNK - jk
    slope = functools.reduce(
        lambda acc, h: jnp.where(row // CHUNK == h, slopes[h], acc), range(1, heads), slopes[0])
    bias = jnp.where((rel >= 0) & (rel <= CHUNK), -slope * (dilation * rel).astype(F32), -jnp.inf)
    before_start = jk < CHUNK
    head = lax.broadcasted_iota(jnp.int32, (1, width), 1) // HEAD_DIM

    def scores(idx):
        r = idx // nb
        jb = idx % nb
        cur = pl.ds(pl.multiple_of(jb * CHUNK, CHUNK), CHUNK)
        prev = pl.ds(pl.multiple_of(jnp.maximum(jb - 1, 0) * CHUNK, CHUNK), CHUNK)
        q = qkv_ref[r, cur, 0:width]
        kband = jnp.concatenate([qkv_ref[r, prev, width:2 * width],
                                 qkv_ref[r, cur, width:2 * width]], 0)
        stacked = jnp.concatenate(
            [jnp.where(head == h, q, jnp.zeros_like(q)) for h in range(heads)], 0)
        s = _dot_nt(stacked, kband) + bias
        return r, cur, prev, jnp.where(before_start & (jb == 0), -jnp.inf, s)

    def softmax(s):
        m = jnp.max(s, -1, keepdims=True)
        p = jnp.exp(s - m)
        l = jnp.sum(p, -1, keepdims=True)
        return (p * (1.0 / l)).astype(BF16), m + jnp.log(l)

    def finish(r, cur, prev, p, row_lse):
        vband = jnp.concatenate([qkv_ref[r, prev, 2 * width:3 * width],
                                 qkv_ref[r, cur, 2 * width:3 * width]], 0)
        o = _dot(p, vband)
        out = jnp.zeros((CHUNK, width), F32)
        lse = jnp.zeros((CHUNK, width), F32)
        for h in range(heads):
            rows = slice(h * CHUNK, (h + 1) * CHUNK)
            out = jnp.where(head == h, o[rows], out)
            lse = jnp.where(head == h, row_lse[rows], lse)
        o_ref[r, cur, :] = out
        lse_ref[r, cur, :] = lse

    def group(i, carry):
        blocks = [scores(DIL_INTERLEAVE * i + j) for j in range(DIL_INTERLEAVE)]
        probs = [softmax(s) for _, _, _, s in blocks]
        for (r, cur, prev, _), (p, row_lse) in zip(blocks, probs):
            finish(r, cur, prev, p, row_lse)
        return carry

    lax.fori_loop(0, dilation * nb // DIL_INTERLEAVE, group, 0)


def _dil_attn(qkv, group):
    b, dilation, length, _ = qkv.shape
    slopes = tuple(2.0 ** (-8.0 * (group * HEADS_PER_DIL + h + 1) / N_DIL_HEADS)
                   for h in range(HEADS_PER_DIL))
    out_spec = pl.BlockSpec((None, dilation, length, DIL_OUT_WIDTH), lambda i: (i, 0, 0, 0))
    out_shape = jax.ShapeDtypeStruct((b, dilation, length, DIL_OUT_WIDTH), F32)
    return pl.pallas_call(
        functools.partial(_dil_attn_kernel, dilation=dilation, slopes=slopes),
        grid=(b,),
        in_specs=[pl.BlockSpec((None, dilation, length, 3 * DIL_OUT_WIDTH),
                               lambda i: (i, 0, 0, 0))],
        out_specs=[out_spec, out_spec],
        out_shape=[out_shape, out_shape],
        compiler_params=pltpu.CompilerParams(vmem_limit_bytes=VMEM_LIMIT),
        name=f"dil_attn_{group}",
    )(qkv)


def _token_order(src_ref, scratch, dilation):
    if dilation == 1:
        return src_ref[0]
    per = src_ref.shape[1]
    halves = DIL_OUT_WIDTH // LANES
    for r in range(dilation):
        v = src_ref[r]
        for c in range(halves):
            scratch[c, pl.ds(r, per, stride=dilation), :] = v[:, c * LANES:(c + 1) * LANES]
    return jnp.concatenate([scratch[c] for c in range(halves)], -1)


def _mixer_b_kernel(x_ref, qm_ref, o0_ref, l0_ref, o1_ref, l1_ref, o2_ref, l2_ref,
                    mk_ref, mv_ref, wout_ref, g1_ref, b1_ref, h_ref,
                    so1, sl1, so2, sl2):
    outs = [o0_ref[0], _token_order(o1_ref, so1, DIL_PAIRS[1][1]),
            _token_order(o2_ref, so2, DIL_PAIRS[2][1])]
    lses = [l0_ref[0], _token_order(l1_ref, sl1, DIL_PAIRS[1][1]),
            _token_order(l2_ref, sl2, DIL_PAIRS[2][1])]
    top = functools.reduce(jnp.maximum, lses)
    es = [jnp.exp(l - top) for l in lses]
    inv = 1.0 / functools.reduce(jnp.add, es)
    mix = functools.reduce(jnp.add, [e * inv * o for e, o in zip(es, outs)])
    mo = _mem_attention(qm_ref[...], mk_ref[...], mv_ref[...])
    att = _dot(jnp.concatenate([mix, mo], -1).astype(BF16), wout_ref[...])
    _to_chunks(h_ref, _layer_norm(ALPHA * _from_chunks(x_ref) + att, g1_ref[...], b1_ref[...]))


def _mixer_b(x_cm, qm, attn, mk, mv, w_out, g1, b1):
    b, _, s, _ = x_cm.shape
    ts = TS_MIX
    nts = s // ts
    m = mk.shape[0] // b
    in_specs = [
        pl.BlockSpec((None, N_CHUNKS, ts, LANES), lambda i, j: (i, 0, j, 0)),
        pl.BlockSpec((None, ts, MEM_WIDTH), lambda i, j: (i, j, 0)),
    ]
    args = [x_cm, qm]
    for (o, lse), (_, dilation) in zip(attn, DIL_PAIRS):
        spec = pl.BlockSpec((None, dilation, ts // dilation, DIL_OUT_WIDTH),
                            lambda i, j: (i, 0, j, 0))
        in_specs += [spec, spec]
        args += [o, lse]
    in_specs += [
        pl.BlockSpec((m, MEM_WIDTH), lambda i, j: (i, 0)),
        pl.BlockSpec((m, MEM_WIDTH), lambda i, j: (i, 0)),
        pl.BlockSpec(w_out.shape, lambda i, j: (0, 0)),
        pl.BlockSpec((1, D_MODEL), lambda i, j: (0, 0)),
        pl.BlockSpec((1, D_MODEL), lambda i, j: (0, 0)),
    ]
    args += [mk, mv, w_out, g1, b1]
    halves = DIL_OUT_WIDTH // LANES
    return pl.pallas_call(
        _mixer_b_kernel,
        grid=(b, nts),
        in_specs=in_specs,
        out_specs=pl.BlockSpec((None, N_CHUNKS, ts, LANES), lambda i, j: (i, 0, j, 0)),
        out_shape=jax.ShapeDtypeStruct((b, N_CHUNKS, s, LANES), F32),
        scratch_shapes=[pltpu.VMEM((halves, ts, LANES), F32)] * 4,
        compiler_params=pltpu.CompilerParams(vmem_limit_bytes=VMEM_LIMIT),
        name="mixer_b",
    )(*args)


def kernel(x, mem, w_in_a, w_out_a, sg_ln_g, sg_ln_b, sg_w, sg_b, w_in_b, w_out_b, w_k_shared,
           w_v_shared, w_mem_k, w_mem_v, ln1_g, ln1_b, ln2_g, ln2_b, w_router, b_router,
           w_gate, w_up, w_down):
    batch, seq, d = x.shape
    mk, mv = _memkv(mem.reshape(-1, d), w_mem_k, w_mem_v)
    row = lambda v: v.reshape(1, -1)

    sg_bias = jnp.repeat(sg_b[0].T, HEAD_DIM, axis=1)
    h = _mixer_a(x, w_in_a[0].astype(BF16), w_out_a[0].astype(BF16), row(sg_ln_g[0]),
                 row(sg_ln_b[0]), sg_w[0], sg_bias, mk[0], mv[0], row(ln1_g[0]), row(ln1_b[0]))
    w_router_pad = jnp.pad(w_router, ((0, 0), (0, LANES - N_EXPERTS)))
    b_router_col = b_router.reshape(N_EXPERTS, 1)
    x_cm = _moe(h, w_router_pad, b_router_col, w_gate, w_up, w_down,
                row(ln2_g[0]), row(ln2_b[0]), layer=0, chunk_major=True)

    wqkv = jnp.stack([
        jnp.concatenate([w[:, g * DIL_OUT_WIDTH:(g + 1) * DIL_OUT_WIDTH]
                         for w in (w_in_b[0], w_k_shared, w_v_shared)], -1)
        for g in range(len(DIL_PAIRS))]).astype(BF16)
    wqm = w_in_b[0][:, DIL_Q_WIDTH:].astype(BF16)
    qkv0, qkv1, qkv2, qm = _proj_b(x_cm, wqkv, wqm)
    attn = [_dil_attn(qkv, g) for g, qkv in enumerate((qkv0, qkv1, qkv2))]
    h = _mixer_b(x_cm, qm, attn, mk[1], mv[1], w_out_b[0].astype(BF16),
                 row(ln1_g[1]), row(ln1_b[1]))
    return _moe(h, w_router_pad, b_router_col, w_gate, w_up, w_down,
                row(ln2_g[1]), row(ln2_b[1]), layer=1, chunk_major=False)
```

```python
import functools
import math

import jax
import jax.numpy as jnp
from jax import lax
from jax.experimental import pallas as pl
from jax.experimental.pallas import tpu as pltpu

D_MODEL = 1024
HEAD_DIM = 64
CHUNK = 128
N_SG = 12
SG_WIDTH = N_SG * HEAD_DIM
DIL_PAIRS = ((128, 1), (512, 4), (2048, 16))
HEADS_PER_DIL = 4
N_DIL_HEADS = HEADS_PER_DIL * len(DIL_PAIRS)
DIL_Q_WIDTH = N_DIL_HEADS * HEAD_DIM
DIL_OUT_WIDTH = HEADS_PER_DIL * HEAD_DIM
MEM_HEADS = 4
MEM_WIDTH = MEM_HEADS * HEAD_DIM
N_EXPERTS = 16
N_EXPERT_GROUPS = 4
EXPERTS_PER_GROUP = N_EXPERTS // N_EXPERT_GROUPS
TOP_K = 2
DEPTH = 2
ALPHA = (2 * DEPTH) ** 0.25
LN_EPS = 1e-5
ATT_SCALE = 1.0 / math.sqrt(HEAD_DIM)

LANES = 128
SUBLANES = 8
N_CHUNKS = D_MODEL // LANES

TS_MIX = 1024
TS_PROJ = 512
MXU_DEPTH = 256
TM_ROUTE = MXU_DEPTH
RUN_BITS = TM_ROUTE.bit_length()
LONG_RUN_BIT = 6
TB_MOE = 512
HIDDEN_SLICES = 2
ROUTE_TILES = 8
DIL_INTERLEAVE = 4
VMEM_LIMIT = 56 * 1024 * 1024

F32 = jnp.float32
BF16 = jnp.bfloat16


def _dot(a, b):
    return jnp.dot(a, b, preferred_element_type=F32)


def _dot_nt(a, b):
    return lax.dot_general(a, b, (((1,), (1,)), ((), ())), preferred_element_type=F32)


def _layer_norm(x, g, b):
    mu = jnp.mean(x, -1, keepdims=True)
    xc = x - mu
    var = jnp.mean(xc * xc, -1, keepdims=True)
    return xc * lax.rsqrt(var + LN_EPS) * g + b


def _gelu(x):
    return 0.5 * x * (1.0 + lax.erf(x * (1.0 / math.sqrt(2.0))))


def _from_tiles(ref, n):
    return jnp.concatenate(
        [ref[pl.ds(c, n, stride=N_CHUNKS), :] for c in range(N_CHUNKS)], -1)


def _to_tiles(ref, val, n):
    for c in range(N_CHUNKS):
        ref[pl.ds(c, n, stride=N_CHUNKS), :] = val[:, c * LANES:(c + 1) * LANES]


def _from_chunks(ref):
    return jnp.concatenate([ref[c] for c in range(N_CHUNKS)], -1)


def _to_chunks(ref, val):
    for c in range(N_CHUNKS):
        ref[c] = val[:, c * LANES:(c + 1) * LANES]


def _chunk_tile_spec(rows, tiles_per_seq):
    return pl.BlockSpec((None, N_CHUNKS, rows, LANES),
                        lambda i: (i // tiles_per_seq, 0, i % tiles_per_seq, 0))


def _mem_attention(q, mk, mv):
    n = q.shape[0]
    head = lax.broadcasted_iota(jnp.int32, (1, MEM_WIDTH), 1) // HEAD_DIM
    stacked = jnp.concatenate(
        [jnp.where(head == h, q, jnp.zeros_like(q)) for h in range(MEM_HEADS)], 0)
    s = _dot_nt(stacked, mk)
    m = jnp.max(s, -1, keepdims=True)
    p = jnp.exp(s - m)
    l = jnp.sum(p, -1, keepdims=True)
    o = _dot((p * (1.0 / l)).astype(BF16), mv)
    out = jnp.zeros((n, MEM_WIDTH), F32)
    for h in range(MEM_HEADS):
        out = jnp.where(head == h, o[h * n:(h + 1) * n], out)
    return out


def _memkv_kernel(mem_ref, wk_ref, wv_ref, mk_ref, mv_ref):
    m = mem_ref[...].astype(BF16)
    mk_ref[...] = _dot(m, wk_ref[...].astype(BF16)).astype(BF16)
    mv_ref[...] = _dot(m, wv_ref[...].astype(BF16)).astype(BF16)


def _memkv(mem2d, w_mem_k, w_mem_v):
    n = mem2d.shape[0]
    return pl.pallas_call(
        _memkv_kernel,
        grid=(DEPTH,),
        in_specs=[
            pl.BlockSpec((n, D_MODEL), lambda l: (0, 0)),
            pl.BlockSpec((None, D_MODEL, MEM_WIDTH), lambda l: (l, 0, 0)),
            pl.BlockSpec((None, D_MODEL, MEM_WIDTH), lambda l: (l, 0, 0)),
        ],
        out_specs=[
            pl.BlockSpec((None, n, MEM_WIDTH), lambda l: (l, 0, 0)),
            pl.BlockSpec((None, n, MEM_WIDTH), lambda l: (l, 0, 0)),
        ],
        out_shape=[jax.ShapeDtypeStruct((DEPTH, n, MEM_WIDTH), BF16)] * 2,
        compiler_params=pltpu.CompilerParams(vmem_limit_bytes=VMEM_LIMIT),
        name="memkv",
    )(mem2d, w_mem_k, w_mem_v)


def _mixer_a_kernel(x_ref, win_ref, wout_ref, sgg_ref, sgb_ref, sgw_ref, sgbias_ref,
                    mk_ref, mv_ref, g1_ref, b1_ref, h_ref):
    ts = x_ref.shape[0]
    x = x_ref[...]
    proj = _dot(x.astype(BF16), win_ref[...])
    u = _gelu(proj[:, :SG_WIDTH])
    gv = _gelu(proj[:, SG_WIDTH:2 * SG_WIDTH])
    gv = _layer_norm(gv, sgg_ref[...], sgb_ref[...]).astype(BF16)

    row = lax.broadcasted_iota(jnp.int32, (CHUNK, CHUNK), 0)
    col = lax.broadcasted_iota(jnp.int32, (CHUNK, CHUNK), 1)
    ws = [jnp.where(row >= col, sgw_ref[g], 0.0).astype(BF16) for g in range(N_SG)]
    n_chunks = ts // CHUNK
    low_half = (lax.broadcasted_iota(jnp.int32, (CHUNK, n_chunks * LANES), 1) % LANES) < HEAD_DIM
    gated = []
    for j in range(N_SG // 2):
        slabs = jnp.concatenate(
            [gv[c * CHUNK:(c + 1) * CHUNK, j * LANES:(j + 1) * LANES] for c in range(n_chunks)], -1)
        both = _dot(jnp.concatenate([ws[2 * j], ws[2 * j + 1]], 0), slabs)
        gated.append(jnp.where(low_half, both[:CHUNK], both[CHUNK:]))
    rows = [jnp.concatenate([g[:, c * LANES:(c + 1) * LANES] for g in gated], -1)
            + sgbias_ref[...] for c in range(n_chunks)]
    mix = u * jnp.concatenate(rows, 0)

    qm = (proj[:, 2 * SG_WIDTH:] * ATT_SCALE).astype(BF16)
    mo = _mem_attention(qm, mk_ref[...], mv_ref[...])
    att = _dot(jnp.concatenate([mix, mo], -1).astype(BF16), wout_ref[...])
    _to_chunks(h_ref, _layer_norm(ALPHA * x + att, g1_ref[...], b1_ref[...]))


def _mixer_a(x, w_in, w_out, sg_g, sg_b, sg_w, sg_bias, mk, mv, g1, b1):
    b, s, d = x.shape
    nts = s // TS_MIX
    m = mk.shape[0] // b
    full = lambda *shape: pl.BlockSpec(shape, lambda i, j: (0,) * len(shape))
    return pl.pallas_call(
        _mixer_a_kernel,
        grid=(b, nts),
        in_specs=[
            pl.BlockSpec((None, TS_MIX, d), lambda i, j: (i, j, 0)),
            full(d, 2 * SG_WIDTH + MEM_WIDTH),
            full(SG_WIDTH + MEM_WIDTH, d),
            full(1, SG_WIDTH),
            full(1, SG_WIDTH),
            full(N_SG, CHUNK, CHUNK),
            full(CHUNK, SG_WIDTH),
            pl.BlockSpec((m, MEM_WIDTH), lambda i, j: (i, 0)),
            pl.BlockSpec((m, MEM_WIDTH), lambda i, j: (i, 0)),
            full(1, d),
            full(1, d),
        ],
        out_specs=pl.BlockSpec((None, N_CHUNKS, TS_MIX, LANES), lambda i, j: (i, 0, j, 0)),
        out_shape=jax.ShapeDtypeStruct((b, N_CHUNKS, s, LANES), F32),
        compiler_params=pltpu.CompilerParams(vmem_limit_bytes=VMEM_LIMIT),
        name="mixer_a",
    )(x, w_in, w_out, sg_g, sg_b, sg_w, sg_bias, mk, mv, g1, b1)


def _router_kernel(h_ref, wr_ref, br_ref, pos_ref, gate_ref, runs_ref, counts_ref, base_ref,
                   wsplit_ref):
    tm = TM_ROUTE

    @pl.when(pl.program_id(0) == 0)
    def _():
        base_ref[...] = jnp.zeros_like(base_ref)
        w = wr_ref[...]
        w_hi = w.astype(BF16)
        wsplit_ref[...] = jnp.concatenate([w_hi, (w - w_hi.astype(F32)).astype(BF16)], -1)

    eid = lax.broadcasted_iota(jnp.int32, (N_EXPERTS, tm), 0)
    group = eid // EXPERTS_PER_GROUP

    def probabilities(k):
        h = jnp.concatenate([h_ref[c, pl.ds(k * tm, tm), :] for c in range(N_CHUNKS)], -1)
        h_hi = h.astype(BF16)
        h_lo = (h - h_hi.astype(F32)).astype(BF16)
        prod = _dot(jnp.concatenate([h_hi, h_lo], 0), wsplit_ref[...])
        logits = prod[:tm, :LANES] + (prod[:tm, LANES:] + prod[tm:, :LANES])
        lt = logits.T[:N_EXPERTS]
        ex = jnp.exp(lt - jnp.max(lt, 0, keepdims=True))
        probs = ex / jnp.sum(ex, 0, keepdims=True)
        return probs, probs + br_ref[...]

    def top2(sel, mask):
        v = jnp.where(mask, sel, -jnp.inf)
        m1 = jnp.max(v, 0, keepdims=True)
        i1 = jnp.min(jnp.where(v == m1, eid, N_EXPERTS), 0, keepdims=True)
        v2 = jnp.where(eid == i1, -jnp.inf, v)
        m2 = jnp.max(v2, 0, keepdims=True)
        i2 = jnp.min(jnp.where(v2 == m2, eid, N_EXPERTS), 0, keepdims=True)
        return m1, i1, m2, i2

    def best_group(sel):
        scores = []
        for g in range(N_EXPERT_GROUPS):
            m1, _, m2, _ = top2(sel, group == g)
            scores.append(m1 + m2)
        best = functools.reduce(jnp.maximum, scores)
        g_idx = jnp.full((1, tm), N_EXPERT_GROUPS - 1, jnp.int32)
        for g in reversed(range(N_EXPERT_GROUPS - 1)):
            g_idx = jnp.where(scores[g] == best, g, g_idx)
        return g_idx

    def choose(probs, sel, g_idx):
        _, e0, _, e1 = top2(sel, group == g_idx)
        hot0 = eid == e0
        hot1 = eid == e1
        p0 = jnp.sum(jnp.where(hot0, probs, 0.0), 0, keepdims=True)
        p1 = jnp.sum(jnp.where(hot1, probs, 0.0), 0, keepdims=True)
        return hot0, hot1, p0, p1

    earlier = jnp.where(lax.broadcasted_iota(jnp.int32, (tm, tm), 0)
                        < lax.broadcasted_iota(jnp.int32, (tm, tm), 1), 1.0, 0.0).astype(BF16)
    below = jnp.where(lax.broadcasted_iota(jnp.int32, (N_EXPERTS, N_EXPERTS), 1)
                      < lax.broadcasted_iota(jnp.int32, (N_EXPERTS, N_EXPERTS), 0),
                      1.0, 0.0).astype(BF16)

    def positions(hot0, hot1):
        hot = jnp.where(hot0 | hot1, 1.0, 0.0)
        hot_bf = hot.astype(BF16)
        before = _dot(hot_bf, earlier)
        run_start = jnp.sum(_dot(below, hot_bf), 1, keepdims=True)
        local = run_start + before
        q0 = jnp.sum(jnp.where(hot0, local, 0.0), 0, keepdims=True)
        q1 = jnp.sum(jnp.where(hot1, local, 0.0), 0, keepdims=True)
        return q0, q1, jnp.sum(hot, 1, keepdims=True)

    tiles = range(ROUTE_TILES)
    scored = [probabilities(k) for k in tiles]
    groups = [best_group(sel) for _, sel in scored]
    chosen = [choose(probs, sel, g_idx) for (probs, sel), g_idx in zip(scored, groups)]
    placed = [positions(hot0, hot1) for hot0, hot1, _, _ in chosen]

    row = lax.broadcasted_iota(jnp.int32, (SUBLANES, tm), 0)
    grow = lax.broadcasted_iota(jnp.int32, (LANES, tm), 0)
    lane = lax.broadcasted_iota(jnp.int32, (N_EXPERTS, LANES), 1)
    base = base_ref[...]
    for k, ((_, _, p0, p1), (q0, q1, count)) in enumerate(zip(chosen, placed)):
        psum = p0 + p1
        pos_ref[k] = jnp.where(row == 0, q0, jnp.where(row == 1, q1, 0.0)).astype(jnp.int32)
        cols = jnp.where(grow == 0, p0 / psum, jnp.where(grow == 1, p1 / psum, jnp.where(
            grow == 2, q0, jnp.where(grow == 3, q1, 0.0))))
        gate_ref[pl.ds(k * tm, tm), :] = cols.T[:, :2 * TOP_K]
        runs_ref[k] = jnp.where(lane == 0, count, jnp.where(lane == 1, base, 0.0)).astype(jnp.int32)
        base = base + count
    base_ref[...] = base
    counts_ref[...] = base.astype(jnp.int32)


def _router(h_cm, w_router, b_router):
    batch, _, seq, _ = h_cm.shape
    t = batch * seq
    tm = TM_ROUTE
    rows = ROUTE_TILES * tm
    return pl.pallas_call(
        _router_kernel,
        grid=(t // rows,),
        in_specs=[
            _chunk_tile_spec(rows, seq // rows),
            pl.BlockSpec((D_MODEL, LANES), lambda i: (0, 0)),
            pl.BlockSpec((N_EXPERTS, 1), lambda i: (0, 0)),
        ],
        out_specs=[
            pl.BlockSpec((ROUTE_TILES, SUBLANES, tm), lambda i: (i, 0, 0)),
            pl.BlockSpec((rows, 2 * TOP_K), lambda i: (i, 0)),
            pl.BlockSpec((ROUTE_TILES, N_EXPERTS, LANES), lambda i: (i, 0, 0)),
            pl.BlockSpec((N_EXPERTS, 1), lambda i: (0, 0)),
        ],
        out_shape=[
            jax.ShapeDtypeStruct((t // tm, SUBLANES, tm), jnp.int32),
            jax.ShapeDtypeStruct((t, 2 * TOP_K), F32),
            jax.ShapeDtypeStruct((t // tm, N_EXPERTS, LANES), jnp.int32),
            jax.ShapeDtypeStruct((N_EXPERTS, 1), jnp.int32),
        ],
        scratch_shapes=[pltpu.VMEM((N_EXPERTS, 1), F32), pltpu.VMEM((D_MODEL, 2 * LANES), BF16)],
        compiler_params=pltpu.CompilerParams(
            dimension_semantics=("arbitrary",), vmem_limit_bytes=VMEM_LIMIT),
        name="router",
    )(h_cm, w_router, b_router)


def _start_run_copies(runs_ref, start_ref, tile_ref, rows_ref, sem, to_rows):
    offset = 0
    for e in range(N_EXPERTS):
        count = runs_ref[e, 0]
        first_row = start_ref[e] + runs_ref[e, 1]

        def copy_piece(bit, count=count, first_row=first_row, offset=offset):
            size = (1 << bit) * N_CHUNKS
            done = (count >> (bit + 1)) << (bit + 1)

            @pl.when(((count >> bit) & 1) == 1)
            def _():
                in_tile = tile_ref.at[
                    pl.ds(pl.multiple_of((offset + done) * N_CHUNKS, N_CHUNKS), size)]
                in_rows = rows_ref.at[
                    pl.ds(pl.multiple_of((first_row + done) * N_CHUNKS, N_CHUNKS), size)]
                if to_rows:
                    pltpu.make_async_copy(in_tile, in_rows, sem).start()
                else:
                    pltpu.make_async_copy(in_rows, in_tile, sem).start()

        @pl.when(count >= (1 << LONG_RUN_BIT))
        def _():
            for bit in reversed(range(LONG_RUN_BIT, RUN_BITS)):
                copy_piece(bit)

        for bit in reversed(range(LONG_RUN_BIT)):
            copy_piece(bit)
        offset = offset + count


def _wait_run_copies(tile_ref, rows_ref, sem):
    pltpu.make_async_copy(tile_ref, rows_ref.at[pl.ds(0, tile_ref.shape[0])], sem).wait()


def _dispatch_kernel(pos_ref, runs_ref, start_ref, end_ref, cnt_ref, h_ref, xs_ref,
                     zero_ref, sorted_ref, sem, zsem):
    i = pl.program_id(0)
    tm = TM_ROUTE

    @pl.when(i == 0)
    def _():
        zero_ref[...] = jnp.zeros_like(zero_ref)

        def zero_copy(first_row):
            first = pl.multiple_of(first_row * N_CHUNKS, N_CHUNKS)
            return pltpu.make_async_copy(
                zero_ref, xs_ref.at[pl.ds(first, TB_MOE * N_CHUNKS)], zsem)

        for wait in (False, True):
            for e in range(N_EXPERTS):
                @pl.when(cnt_ref[e, 0] > 0)
                def _():
                    cp = zero_copy(end_ref[e] - TB_MOE)
                    cp.wait() if wait else cp.start()

        def zero_block(blk, carry):
            cp = zero_copy(blk * TB_MOE)
            cp.start()
            cp.wait()
            return carry

        n_blocks = xs_ref.shape[0] // (TB_MOE * N_CHUNKS)
        lax.fori_loop(end_ref[N_EXPERTS - 1] // TB_MOE, n_blocks, zero_block, 0)

    p = lax.broadcasted_iota(jnp.int32, (TOP_K * tm, tm), 0)
    onehot = (p == pos_ref[0:1, :]) | (p == pos_ref[1:2, :])
    sorted_rows = _dot(jnp.where(onehot, 1.0, 0.0).astype(BF16),
                       _from_chunks(h_ref).astype(BF16))

    slot = i % 2
    buf = sorted_ref.at[slot]

    @pl.when(i >= 2)
    def _():
        _wait_run_copies(buf, xs_ref, sem.at[slot])

    _to_tiles(buf, sorted_rows, TOP_K * tm)
    _start_run_copies(runs_ref, start_ref, buf, xs_ref, sem.at[slot], to_rows=True)

    @pl.when(i == pl.num_programs(0) - 1)
    def _():
        _wait_run_copies(buf, xs_ref, sem.at[slot])
        _wait_run_copies(sorted_ref.at[1 - slot], xs_ref, sem.at[1 - slot])


def _dispatch(pos, runs, pad_starts, pad_ends, counts, h_cm, n_rows):
    nt = pos.shape[0]
    tm = TM_ROUTE
    seq = h_cm.shape[2]
    smem_vec = pl.BlockSpec(memory_space=pltpu.SMEM)
    return pl.pallas_call(
        _dispatch_kernel,
        grid=(nt,),
        in_specs=[
            pl.BlockSpec((None, SUBLANES, tm), lambda i: (i, 0, 0)),
            pl.BlockSpec((None, N_EXPERTS, LANES), lambda i: (i, 0, 0), memory_space=pltpu.SMEM),
            smem_vec, smem_vec, smem_vec,
            _chunk_tile_spec(tm, seq // tm)],
        out_specs=pl.BlockSpec(memory_space=pl.ANY),
        out_shape=jax.ShapeDtypeStruct((n_rows * N_CHUNKS, LANES), F32),
        scratch_shapes=[
            pltpu.VMEM((TB_MOE * N_CHUNKS, LANES), F32),
            pltpu.VMEM((2, TOP_K * tm * N_CHUNKS, LANES), F32),
            pltpu.SemaphoreType.DMA((2,)),
            pltpu.SemaphoreType.DMA(()),
        ],
        compiler_params=pltpu.CompilerParams(
            dimension_semantics=("arbitrary",), has_side_effects=True,
            vmem_limit_bytes=VMEM_LIMIT),
        name="dispatch",
    )(pos, runs, pad_starts, pad_ends, counts, h_cm)


def _experts_kernel(be_ref, nused_ref, valid_ref, xs_ref, wg_ref, wu_ref, wd_ref, ys_ref,
                    wg_bf, wu_bf, wd_bf):
    i = pl.program_id(0)
    nused = nused_ref[0]
    half = TB_MOE // 2

    def ffn(nrows):
        x = _from_tiles(xs_ref, nrows).astype(BF16)
        width = wg_bf.shape[1] // HIDDEN_SLICES
        cols = [slice(k * width, (k + 1) * width) for k in range(HIDDEN_SLICES)]
        acts = []
        for c in cols:
            hg = _dot(x, wg_bf[:, c])
            acts.append((hg * jax.nn.sigmoid(hg) * _dot(x, wu_bf[:, c])).astype(BF16))
        y = functools.reduce(jnp.add, [_dot(act, wd_bf[c, :]) for act, c in zip(acts, cols)])
        _to_tiles(ys_ref, y, nrows)

    @pl.when(i < nused)
    def _():
        prev = be_ref[jnp.maximum(i - 1, 0)]

        @pl.when((i == 0) | (be_ref[i] != prev))
        def _():
            wg_bf[...] = wg_ref[...].astype(BF16)
            wu_bf[...] = wu_ref[...].astype(BF16)
            wd_bf[...] = wd_ref[...].astype(BF16)

        @pl.when(valid_ref[i] > half)
        def _():
            ffn(TB_MOE)

        @pl.when(valid_ref[i] <= half)
        def _():
            ffn(half)
            padding = ys_ref.at[pl.ds(half * N_CHUNKS, half * N_CHUNKS)]
            padding[...] = jnp.zeros_like(padding)

    @pl.when(i >= nused)
    def _():
        ys_ref[...] = jnp.zeros_like(ys_ref)


def _experts(block_expert, nused, block_valid, xs, w_gate, w_up, w_down, layer):
    n_blocks = xs.shape[0] // (TB_MOE * N_CHUNKS)
    d, de = w_gate.shape[2], w_gate.shape[3]

    def x_map(i, be, nu, bv):
        return (jnp.minimum(i, nu[0] - 1), 0)

    def w_map(i, be, nu, bv):
        return (layer, be[jnp.minimum(i, nu[0] - 1)], 0, 0)

    return pl.pallas_call(
        _experts_kernel,
        grid_spec=pltpu.PrefetchScalarGridSpec(
            num_scalar_prefetch=3,
            grid=(n_blocks,),
            in_specs=[
                pl.BlockSpec((TB_MOE * N_CHUNKS, LANES), x_map),
                pl.BlockSpec((None, None, d, de), w_map),
                pl.BlockSpec((None, None, d, de), w_map),
                pl.BlockSpec((None, None, de, d), w_map),
            ],
            out_specs=pl.BlockSpec((TB_MOE * N_CHUNKS, LANES), lambda i, be, nu, bv: (i, 0)),
            scratch_shapes=[
                pltpu.VMEM((d, de), BF16),
                pltpu.VMEM((d, de), BF16),
                pltpu.VMEM((de, d), BF16),
            ],
        ),
        out_shape=jax.ShapeDtypeStruct(xs.shape, F32),
        compiler_params=pltpu.CompilerParams(
            dimension_semantics=("arbitrary",), vmem_limit_bytes=VMEM_LIMIT),
        name="experts",
    )(block_expert, nused, block_valid, xs, w_gate, w_up, w_down)


def _combine_kernel(runs_ref, next_runs_ref, start_ref, h_ref, gate_ref, g2_ref, b2_ref, ys_ref,
                    out_ref, sorted_ref, sem, *, chunk_major):
    tm = TM_ROUTE
    i = pl.program_id(0)
    slot = i % 2

    @pl.when(i == 0)
    def _():
        _start_run_copies(runs_ref, start_ref, sorted_ref.at[0], ys_ref, sem.at[0], to_rows=False)

    @pl.when(i + 1 < pl.num_programs(0))
    def _():
        _start_run_copies(next_runs_ref, start_ref, sorted_ref.at[1 - slot], ys_ref,
                          sem.at[1 - slot], to_rows=False)

    _wait_run_copies(sorted_ref.at[slot], ys_ref, sem.at[slot])

    gate = gate_ref[...]
    p = lax.broadcasted_iota(jnp.int32, (tm, TOP_K * tm), 1)
    onehot = jnp.concatenate([p == gate[:, 2:3].astype(jnp.int32),
                              p == gate[:, 3:4].astype(jnp.int32)], 0)
    picked = _dot(jnp.where(onehot, 1.0, 0.0).astype(BF16),
                  _from_tiles(sorted_ref.at[slot], TOP_K * tm).astype(BF16))
    ffn = picked[:tm] * gate[:, 0:1] + picked[tm:] * gate[:, 1:2]
    out = _layer_norm(ALPHA * _from_chunks(h_ref) + ffn, g2_ref[...], b2_ref[...])
    if chunk_major:
        _to_chunks(out_ref, out)
    else:
        out_ref[...] = out


def _combine(runs, pad_starts, h_cm, gate, g2, b2, ys, chunk_major):
    batch, _, seq, _ = h_cm.shape
    tm = TM_ROUTE
    t = batch * seq
    nt = t // tm
    nts = seq // tm
    if chunk_major:
        out_spec = _chunk_tile_spec(tm, nts)
        out_shape = jax.ShapeDtypeStruct((batch, N_CHUNKS, seq, LANES), F32)
    else:
        out_spec = pl.BlockSpec((None, tm, D_MODEL), lambda i: (i // nts, i % nts, 0))
        out_shape = jax.ShapeDtypeStruct((batch, seq, D_MODEL), F32)
    return pl.pallas_call(
        functools.partial(_combine_kernel, chunk_major=chunk_major),
        grid=(nt,),
        in_specs=[
            pl.BlockSpec((None, N_EXPERTS, LANES), lambda i: (i, 0, 0), memory_space=pltpu.SMEM),
            pl.BlockSpec((None, N_EXPERTS, LANES), lambda i: (jnp.minimum(i + 1, nt - 1), 0, 0),
                         memory_space=pltpu.SMEM),
            pl.BlockSpec(memory_space=pltpu.SMEM),
            _chunk_tile_spec(tm, nts),
            pl.BlockSpec((tm, 2 * TOP_K), lambda i: (i, 0)),
            pl.BlockSpec((1, D_MODEL), lambda i: (0, 0)),
            pl.BlockSpec((1, D_MODEL), lambda i: (0, 0)),
            pl.BlockSpec(memory_space=pl.ANY),
        ],
        out_specs=out_spec,
        out_shape=out_shape,
        scratch_shapes=[
            pltpu.VMEM((2, TOP_K * tm * N_CHUNKS, LANES), F32),
            pltpu.SemaphoreType.DMA((2,)),
        ],
        compiler_params=pltpu.CompilerParams(
            dimension_semantics=("arbitrary",), vmem_limit_bytes=VMEM_LIMIT),
        name="combine",
    )(runs, runs, pad_starts, h_cm, gate, g2, b2, ys)


def _moe(h_cm, w_router, b_router, w_gate, w_up, w_down, g2, b2, layer, chunk_major):
    t = h_cm.shape[0] * h_cm.shape[2]
    pos, gate, runs, counts = _router(h_cm, w_router, b_router)
    cnt = counts[:, 0]
    padded = (cnt + TB_MOE - 1) // TB_MOE * TB_MOE
    pad_ends = jnp.cumsum(padded).astype(jnp.int32)
    pad_starts = pad_ends - padded
    n_rows = t * TOP_K + N_EXPERTS * TB_MOE
    n_blocks = n_rows // TB_MOE
    block_start = jnp.arange(n_blocks, dtype=jnp.int32) * TB_MOE
    block_expert = jnp.minimum(
        jnp.sum(block_start[:, None] >= pad_ends[None, :], -1), N_EXPERTS - 1).astype(jnp.int32)
    nused = (pad_ends[-1:] // TB_MOE).astype(jnp.int32)
    block_valid = jnp.clip((pad_starts + cnt)[block_expert] - block_start, 0, TB_MOE)
    xs = _dispatch(pos, runs, pad_starts, pad_ends, counts, h_cm, n_rows)
    ys = _experts(block_expert, nused, block_valid.astype(jnp.int32), xs, w_gate, w_up, w_down,
                  layer)
    return _combine(runs, pad_starts, h_cm, gate, g2, b2, ys, chunk_major)


def _residue_rows(x_ref, dilation):
    n = x_ref.shape[1]
    if dilation == 1:
        return _from_chunks(x_ref)
    per = n // dilation
    return jnp.concatenate(
        [jnp.concatenate([x_ref[c, pl.ds(r, per, stride=dilation), :] for r in range(dilation)], 0)
         for c in range(N_CHUNKS)], -1)


def _proj_b_kernel(x_ref, wqkv_ref, wqm_ref, qkv0_ref, qkv1_ref, qkv2_ref, qm_ref):
    ts = x_ref.shape[1]
    qscale = jnp.where(
        lax.broadcasted_iota(jnp.int32, (1, 3 * DIL_OUT_WIDTH), 1) < DIL_OUT_WIDTH, ATT_SCALE, 1.0)
    for g, (out_ref, (_, dilation)) in enumerate(zip((qkv0_ref, qkv1_ref, qkv2_ref), DIL_PAIRS)):
        xb = _residue_rows(x_ref, dilation).astype(BF16)
        qkv = (_dot(xb, wqkv_ref[g]) * qscale).astype(BF16)
        per = ts // dilation
        for r in range(dilation):
            out_ref[r] = qkv[r * per:(r + 1) * per]
    qm = _dot(_from_chunks(x_ref).astype(BF16), wqm_ref[...])
    qm_ref[...] = (qm * ATT_SCALE).astype(BF16)


def _proj_b(x_cm, wqkv, wqm):
    b, _, s, _ = x_cm.shape
    ts = TS_PROJ
    width = 3 * DIL_OUT_WIDTH
    out_specs, out_shapes = [], []
    for _, dilation in DIL_PAIRS:
        out_specs.append(pl.BlockSpec((None, dilation, ts // dilation, width),
                                      lambda i, j: (i, 0, j, 0)))
        out_shapes.append(jax.ShapeDtypeStruct((b, dilation, s // dilation, width), BF16))
    out_specs.append(pl.BlockSpec((None, ts, MEM_WIDTH), lambda i, j: (i, j, 0)))
    out_shapes.append(jax.ShapeDtypeStruct((b, s, MEM_WIDTH), BF16))
    return pl.pallas_call(
        _proj_b_kernel,
        grid=(b, s // ts),
        in_specs=[
            pl.BlockSpec((None, N_CHUNKS, ts, LANES), lambda i, j: (i, 0, j, 0)),
            pl.BlockSpec(wqkv.shape, lambda i, j: (0, 0, 0)),
            pl.BlockSpec(wqm.shape, lambda i, j: (0, 0)),
        ],
        out_specs=out_specs,
        out_shape=out_shapes,
        compiler_params=pltpu.CompilerParams(vmem_limit_bytes=VMEM_LIMIT),
        name="proj_b",
    )(x_cm, wqkv, wqm)


def _dil_attn_kernel(qkv_ref, o_ref, lse_ref, *, dilation, slopes):
    nb = qkv_ref.shape[1] // CHUNK
    width = DIL_OUT_WIDTH
    heads = HEADS_PER_DIL
    row = lax.broadcasted_iota(jnp.int32, (heads * CHUNK, 2 * CHUNK), 0)
    jk = lax.broadcasted_iota(jnp.int32, (heads * CHUNK, 2 * CHUNK), 1)
    rel = CHUNK + row % CHUNK - jk
    slope = functools.reduce(
        lambda acc, h: jnp.where(row // CHUNK == h, slopes[h], acc), range(1, heads), slopes[0])
    bias = jnp.where((rel >= 0) & (rel <= CHUNK), -slope * (dilation * rel).astype(F32), -jnp.inf)
    before_start = jk < CHUNK
    head = lax.broadcasted_iota(jnp.int32, (1, width), 1) // HEAD_DIM

    def scores(idx, may_be_first):
        r = idx // nb
        jb = idx % nb
        cur = pl.ds(pl.multiple_of(jb * CHUNK, CHUNK), CHUNK)
        prev = pl.ds(pl.multiple_of(jnp.maximum(jb - 1, 0) * CHUNK, CHUNK), CHUNK)
        q = qkv_ref[r, cur, 0:width]
        kband = jnp.concatenate([qkv_ref[r, prev, width:2 * width],
                                 qkv_ref[r, cur, width:2 * width]], 0)
        stacked = jnp.concatenate(
            [jnp.where(head == h, q, jnp.zeros_like(q)) for h in range(heads)], 0)
        s = _dot_nt(stacked, kband) + bias
        if may_be_first:
            s = jnp.where(before_start & (jb == 0), -jnp.inf, s)
        return r, cur, prev, s

    def softmax(s):
        m = jnp.max(s, -1, keepdims=True)
        p = jnp.exp(s - m)
        l = jnp.sum(p, -1, keepdims=True)
        return (p * (1.0 / l)).astype(BF16), m + jnp.log(l)

    def finish(r, cur, prev, p, row_lse):
        vband = jnp.concatenate([qkv_ref[r, prev, 2 * width:3 * width],
                                 qkv_ref[r, cur, 2 * width:3 * width]], 0)
        o = _dot(p, vband)
        out = jnp.zeros((CHUNK, width), F32)
        lse = jnp.zeros((CHUNK, width), F32)
        for h in range(heads):
            rows = slice(h * CHUNK, (h + 1) * CHUNK)
            out = jnp.where(head == h, o[rows], out)
            lse = jnp.where(head == h, row_lse[rows], lse)
        o_ref[r, cur, :] = out
        lse_ref[r, cur, :] = lse

    def group(i, carry):
        blocks = [scores(DIL_INTERLEAVE * i + j, j % nb == 0) for j in range(DIL_INTERLEAVE)]
        probs = [softmax(s) for _, _, _, s in blocks]
        for (r, cur, prev, _), (p, row_lse) in zip(blocks, probs):
            finish(r, cur, prev, p, row_lse)
        return carry

    lax.fori_loop(0, dilation * nb // DIL_INTERLEAVE, group, 0)


def _dil_attn(qkv, group):
    b, dilation, length, _ = qkv.shape
    slopes = tuple(2.0 ** (-8.0 * (group * HEADS_PER_DIL + h + 1) / N_DIL_HEADS)
                   for h in range(HEADS_PER_DIL))
    out_spec = pl.BlockSpec((None, dilation, length, DIL_OUT_WIDTH), lambda i: (i, 0, 0, 0))
    out_shape = jax.ShapeDtypeStruct((b, dilation, length, DIL_OUT_WIDTH), F32)
    return pl.pallas_call(
        functools.partial(_dil_attn_kernel, dilation=dilation, slopes=slopes),
        grid=(b,),
        in_specs=[pl.BlockSpec((None, dilation, length, 3 * DIL_OUT_WIDTH),
                               lambda i: (i, 0, 0, 0))],
        out_specs=[out_spec, out_spec],
        out_shape=[out_shape, out_shape],
        compiler_params=pltpu.CompilerParams(vmem_limit_bytes=VMEM_LIMIT),
        name=f"dil_attn_{group}",
    )(qkv)


def _token_order(src_ref, scratch, dilation):
    if dilation == 1:
        return src_ref[0]
    per = src_ref.shape[1]
    halves = DIL_OUT_WIDTH // LANES
    for r in range(dilation):
        v = src_ref[r]
        for c in range(halves):
            scratch[c, pl.ds(r, per, stride=dilation), :] = v[:, c * LANES:(c + 1) * LANES]
    return jnp.concatenate([scratch[c] for c in range(halves)], -1)


def _mixer_b_kernel(x_ref, qm_ref, o0_ref, l0_ref, o1_ref, l1_ref, o2_ref, l2_ref,
                    mk_ref, mv_ref, wout_ref, g1_ref, b1_ref, h_ref,
                    so1, sl1, so2, sl2):
    outs = [o0_ref[0], _token_order(o1_ref, so1, DIL_PAIRS[1][1]),
            _token_order(o2_ref, so2, DIL_PAIRS[2][1])]
    lses = [l0_ref[0], _token_order(l1_ref, sl1, DIL_PAIRS[1][1]),
            _token_order(l2_ref, sl2, DIL_PAIRS[2][1])]
    top = functools.reduce(jnp.maximum, lses)
    es = [jnp.exp(l - top) for l in lses]
    inv = 1.0 / functools.reduce(jnp.add, es)
    mix = functools.reduce(jnp.add, [e * inv * o for e, o in zip(es, outs)])
    mo = _mem_attention(qm_ref[...], mk_ref[...], mv_ref[...])
    att = _dot(jnp.concatenate([mix, mo], -1).astype(BF16), wout_ref[...])
    _to_chunks(h_ref, _layer_norm(ALPHA * _from_chunks(x_ref) + att, g1_ref[...], b1_ref[...]))


def _mixer_b(x_cm, qm, attn, mk, mv, w_out, g1, b1):
    b, _, s, _ = x_cm.shape
    ts = TS_MIX
    nts = s // ts
    m = mk.shape[0] // b
    in_specs = [
        pl.BlockSpec((None, N_CHUNKS, ts, LANES), lambda i, j: (i, 0, j, 0)),
        pl.BlockSpec((None, ts, MEM_WIDTH), lambda i, j: (i, j, 0)),
    ]
    args = [x_cm, qm]
    for (o, lse), (_, dilation) in zip(attn, DIL_PAIRS):
        spec = pl.BlockSpec((None, dilation, ts // dilation, DIL_OUT_WIDTH),
                            lambda i, j: (i, 0, j, 0))
        in_specs += [spec, spec]
        args += [o, lse]
    in_specs += [
        pl.BlockSpec((m, MEM_WIDTH), lambda i, j: (i, 0)),
        pl.BlockSpec((m, MEM_WIDTH), lambda i, j: (i, 0)),
        pl.BlockSpec(w_out.shape, lambda i, j: (0, 0)),
        pl.BlockSpec((1, D_MODEL), lambda i, j: (0, 0)),
        pl.BlockSpec((1, D_MODEL), lambda i, j: (0, 0)),
    ]
    args += [mk, mv, w_out, g1, b1]
    halves = DIL_OUT_WIDTH // LANES
    return pl.pallas_call(
        _mixer_b_kernel,
        grid=(b, nts),
        in_specs=in_specs,
        out_specs=pl.BlockSpec((None, N_CHUNKS, ts, LANES), lambda i, j: (i, 0, j, 0)),
        out_shape=jax.ShapeDtypeStruct((b, N_CHUNKS, s, LANES), F32),
        scratch_shapes=[pltpu.VMEM((halves, ts, LANES), F32)] * 4,
        compiler_params=pltpu.CompilerParams(vmem_limit_bytes=VMEM_LIMIT),
        name="mixer_b",
    )(*args)


def kernel(x, mem, w_in_a, w_out_a, sg_ln_g, sg_ln_b, sg_w, sg_b, w_in_b, w_out_b, w_k_shared,
           w_v_shared, w_mem_k, w_mem_v, ln1_g, ln1_b, ln2_g, ln2_b, w_router, b_router,
           w_gate, w_up, w_down):
    batch, seq, d = x.shape
    mk, mv = _memkv(mem.reshape(-1, d), w_mem_k, w_mem_v)
    row = lambda v: v.reshape(1, -1)

    sg_bias = jnp.repeat(sg_b[0].T, HEAD_DIM, axis=1)
    h = _mixer_a(x, w_in_a[0].astype(BF16), w_out_a[0].astype(BF16), row(sg_ln_g[0]),
                 row(sg_ln_b[0]), sg_w[0], sg_bias, mk[0], mv[0], row(ln1_g[0]), row(ln1_b[0]))
    w_router_pad = jnp.pad(w_router, ((0, 0), (0, LANES - N_EXPERTS)))
    b_router_col = b_router.reshape(N_EXPERTS, 1)
    x_cm = _moe(h, w_router_pad, b_router_col, w_gate, w_up, w_down,
                row(ln2_g[0]), row(ln2_b[0]), layer=0, chunk_major=True)

    wqkv = jnp.stack([
        jnp.concatenate([w[:, g * DIL_OUT_WIDTH:(g + 1) * DIL_OUT_WIDTH]
                         for w in (w_in_b[0], w_k_shared, w_v_shared)], -1)
        for g in range(len(DIL_PAIRS))]).astype(BF16)
    wqm = w_in_b[0][:, DIL_Q_WIDTH:].astype(BF16)
    qkv0, qkv1, qkv2, qm = _proj_b(x_cm, wqkv, wqm)
    attn = [_dil_attn(qkv, g) for g, qkv in enumerate((qkv0, qkv1, qkv2))]
    h = _mixer_b(x_cm, qm, attn, mk[1], mv[1], w_out_b[0].astype(BF16),
                 row(ln1_g[1]), row(ln1_b[1]))
    return _moe(h, w_router_pad, b_router_col, w_gate, w_up, w_down,
                row(ln2_g[1]), row(ln2_b[1]), layer=1, chunk_major=False)
```

```python
import functools
import math

import jax
import jax.numpy as jnp
from jax import lax
from jax.experimental import pallas as pl
from jax.experimental.pallas import tpu as pltpu

D_MODEL = 1024
HEAD_DIM = 64
CHUNK = 128
N_SG = 12
SG_WIDTH = N_SG * HEAD_DIM
DIL_PAIRS = ((128, 1), (512, 4), (2048, 16))
HEADS_PER_DIL = 4
N_DIL_HEADS = HEADS_PER_DIL * len(DIL_PAIRS)
DIL_Q_WIDTH = N_DIL_HEADS * HEAD_DIM
DIL_OUT_WIDTH = HEADS_PER_DIL * HEAD_DIM
MEM_HEADS = 4
MEM_WIDTH = MEM_HEADS * HEAD_DIM
N_EXPERTS = 16
N_EXPERT_GROUPS = 4
EXPERTS_PER_GROUP = N_EXPERTS // N_EXPERT_GROUPS
TOP_K = 2
DEPTH = 2
ALPHA = (2 * DEPTH) ** 0.25
LN_EPS = 1e-5
ATT_SCALE = 1.0 / math.sqrt(HEAD_DIM)

LANES = 128
SUBLANES = 8
N_CHUNKS = D_MODEL // LANES

TS_MIX = 1024
TS_PROJ = 512
MXU_DEPTH = 256
TM_ROUTE = MXU_DEPTH
RUN_BITS = TM_ROUTE.bit_length()
LONG_RUN_BIT = 6
TB_MOE = 512
HIDDEN_SLICES = 4
ROUTE_TILES = 8
DISPATCH_TILES = 4
COMBINE_TILES = 4
DIL_INTERLEAVE = 4
VMEM_LIMIT = 56 * 1024 * 1024

F32 = jnp.float32
BF16 = jnp.bfloat16


def _dot(a, b):
    return jnp.dot(a, b, preferred_element_type=F32)


def _dot_nt(a, b):
    return lax.dot_general(a, b, (((1,), (1,)), ((), ())), preferred_element_type=F32)


def _layer_norm(x, g, b):
    mu = jnp.mean(x, -1, keepdims=True)
    xc = x - mu
    var = jnp.mean(xc * xc, -1, keepdims=True)
    return xc * lax.rsqrt(var + LN_EPS) * g + b


def _gelu(x):
    return 0.5 * x * (1.0 + lax.erf(x * (1.0 / math.sqrt(2.0))))


def _from_tiles(ref, n):
    return jnp.concatenate(
        [ref[pl.ds(c, n, stride=N_CHUNKS), :] for c in range(N_CHUNKS)], -1)


def _to_tiles(ref, val, n):
    for c in range(N_CHUNKS):
        ref[pl.ds(c, n, stride=N_CHUNKS), :] = val[:, c * LANES:(c + 1) * LANES]


def _from_chunks(ref):
    return jnp.concatenate([ref[c] for c in range(N_CHUNKS)], -1)


def _to_chunks(ref, val):
    for c in range(N_CHUNKS):
        ref[c] = val[:, c * LANES:(c + 1) * LANES]


def _chunk_tile_spec(rows, tiles_per_seq):
    return pl.BlockSpec((None, N_CHUNKS, rows, LANES),
                        lambda i: (i // tiles_per_seq, 0, i % tiles_per_seq, 0))


def _mem_attention(q, mk, mv):
    n = q.shape[0]
    head = lax.broadcasted_iota(jnp.int32, (1, MEM_WIDTH), 1) // HEAD_DIM
    stacked = jnp.concatenate(
        [jnp.where(head == h, q, jnp.zeros_like(q)) for h in range(MEM_HEADS)], 0)
    s = _dot_nt(stacked, mk)
    m = jnp.max(s, -1, keepdims=True)
    p = jnp.exp(s - m)
    l = jnp.sum(p, -1, keepdims=True)
    o = _dot((p * (1.0 / l)).astype(BF16), mv)
    out = jnp.zeros((n, MEM_WIDTH), F32)
    for h in range(MEM_HEADS):
        out = jnp.where(head == h, o[h * n:(h + 1) * n], out)
    return out


def _memkv_kernel(mem_ref, wk_ref, wv_ref, mk_ref, mv_ref):
    m = mem_ref[...].astype(BF16)
    mk_ref[...] = _dot(m, wk_ref[...].astype(BF16)).astype(BF16)
    mv_ref[...] = _dot(m, wv_ref[...].astype(BF16)).astype(BF16)


def _memkv(mem2d, w_mem_k, w_mem_v):
    n = mem2d.shape[0]
    return pl.pallas_call(
        _memkv_kernel,
        grid=(DEPTH,),
        in_specs=[
            pl.BlockSpec((n, D_MODEL), lambda l: (0, 0)),
            pl.BlockSpec((None, D_MODEL, MEM_WIDTH), lambda l: (l, 0, 0)),
            pl.BlockSpec((None, D_MODEL, MEM_WIDTH), lambda l: (l, 0, 0)),
        ],
        out_specs=[
            pl.BlockSpec((None, n, MEM_WIDTH), lambda l: (l, 0, 0)),
            pl.BlockSpec((None, n, MEM_WIDTH), lambda l: (l, 0, 0)),
        ],
        out_shape=[jax.ShapeDtypeStruct((DEPTH, n, MEM_WIDTH), BF16)] * 2,
        compiler_params=pltpu.CompilerParams(vmem_limit_bytes=VMEM_LIMIT),
        name="memkv",
    )(mem2d, w_mem_k, w_mem_v)


def _mixer_a_kernel(x_ref, win_ref, wout_ref, sgg_ref, sgb_ref, sgw_ref, sgbias_ref,
                    mk_ref, mv_ref, g1_ref, b1_ref, h_ref):
    ts = x_ref.shape[0]
    x = x_ref[...]
    proj = _dot(x.astype(BF16), win_ref[...])
    u = _gelu(proj[:, :SG_WIDTH])
    gv = _gelu(proj[:, SG_WIDTH:2 * SG_WIDTH])
    gv = _layer_norm(gv, sgg_ref[...], sgb_ref[...]).astype(BF16)

    row = lax.broadcasted_iota(jnp.int32, (CHUNK, CHUNK), 0)
    col = lax.broadcasted_iota(jnp.int32, (CHUNK, CHUNK), 1)
    ws = [jnp.where(row >= col, sgw_ref[g], 0.0).astype(BF16) for g in range(N_SG)]
    n_chunks = ts // CHUNK
    low_half = (lax.broadcasted_iota(jnp.int32, (CHUNK, n_chunks * LANES), 1) % LANES) < HEAD_DIM
    gated = []
    for j in range(N_SG // 2):
        slabs = jnp.concatenate(
            [gv[c * CHUNK:(c + 1) * CHUNK, j * LANES:(j + 1) * LANES] for c in range(n_chunks)], -1)
        both = _dot(jnp.concatenate([ws[2 * j], ws[2 * j + 1]], 0), slabs)
        gated.append(jnp.where(low_half, both[:CHUNK], both[CHUNK:]))
    rows = [jnp.concatenate([g[:, c * LANES:(c + 1) * LANES] for g in gated], -1)
            + sgbias_ref[...] for c in range(n_chunks)]
    mix = u * jnp.concatenate(rows, 0)

    qm = (proj[:, 2 * SG_WIDTH:] * ATT_SCALE).astype(BF16)
    mo = _mem_attention(qm, mk_ref[...], mv_ref[...])
    att = _dot(jnp.concatenate([mix, mo], -1).astype(BF16), wout_ref[...])
    _to_chunks(h_ref, _layer_norm(ALPHA * x + att, g1_ref[...], b1_ref[...]))


def _mixer_a(x, w_in, w_out, sg_g, sg_b, sg_w, sg_bias, mk, mv, g1, b1):
    b, s, d = x.shape
    nts = s // TS_MIX
    m = mk.shape[0] // b
    full = lambda *shape: pl.BlockSpec(shape, lambda i, j: (0,) * len(shape))
    return pl.pallas_call(
        _mixer_a_kernel,
        grid=(b, nts),
        in_specs=[
            pl.BlockSpec((None, TS_MIX, d), lambda i, j: (i, j, 0)),
            full(d, 2 * SG_WIDTH + MEM_WIDTH),
            full(SG_WIDTH + MEM_WIDTH, d),
            full(1, SG_WIDTH),
            full(1, SG_WIDTH),
            full(N_SG, CHUNK, CHUNK),
            full(CHUNK, SG_WIDTH),
            pl.BlockSpec((m, MEM_WIDTH), lambda i, j: (i, 0)),
            pl.BlockSpec((m, MEM_WIDTH), lambda i, j: (i, 0)),
            full(1, d),
            full(1, d),
        ],
        out_specs=pl.BlockSpec((None, N_CHUNKS, TS_MIX, LANES), lambda i, j: (i, 0, j, 0)),
        out_shape=jax.ShapeDtypeStruct((b, N_CHUNKS, s, LANES), F32),
        compiler_params=pltpu.CompilerParams(vmem_limit_bytes=VMEM_LIMIT),
        name="mixer_a",
    )(x, w_in, w_out, sg_g, sg_b, sg_w, sg_bias, mk, mv, g1, b1)


def _router_kernel(h_ref, wr_ref, br_ref, pos_ref, gate_ref, runs_ref, counts_ref, base_ref,
                   wsplit_ref):
    tm = TM_ROUTE

    @pl.when(pl.program_id(0) == 0)
    def _():
        base_ref[...] = jnp.zeros_like(base_ref)
        w = wr_ref[...]
        w_hi = w.astype(BF16)
        wsplit_ref[...] = jnp.concatenate([w_hi, (w - w_hi.astype(F32)).astype(BF16)], -1)

    eid = lax.broadcasted_iota(jnp.int32, (N_EXPERTS, tm), 0)
    group = eid // EXPERTS_PER_GROUP

    def probabilities(k):
        h = jnp.concatenate([h_ref[c, pl.ds(k * tm, tm), :] for c in range(N_CHUNKS)], -1)
        h_hi = h.astype(BF16)
        h_lo = (h - h_hi.astype(F32)).astype(BF16)
        prod = _dot(jnp.concatenate([h_hi, h_lo], 0), wsplit_ref[...])
        logits = prod[:tm, :LANES] + (prod[:tm, LANES:] + prod[tm:, :LANES])
        lt = logits.T[:N_EXPERTS]
        ex = jnp.exp(lt - jnp.max(lt, 0, keepdims=True))
        probs = ex / jnp.sum(ex, 0, keepdims=True)
        return probs, probs + br_ref[...]

    def top2(sel, mask):
        v = jnp.where(mask, sel, -jnp.inf)
        m1 = jnp.max(v, 0, keepdims=True)
        i1 = jnp.min(jnp.where(v == m1, eid, N_EXPERTS), 0, keepdims=True)
        v2 = jnp.where(eid == i1, -jnp.inf, v)
        m2 = jnp.max(v2, 0, keepdims=True)
        i2 = jnp.min(jnp.where(v2 == m2, eid, N_EXPERTS), 0, keepdims=True)
        return m1, i1, m2, i2

    def best_group(sel):
        scores = []
        for g in range(N_EXPERT_GROUPS):
            m1, _, m2, _ = top2(sel, group == g)
            scores.append(m1 + m2)
        best = functools.reduce(jnp.maximum, scores)
        g_idx = jnp.full((1, tm), N_EXPERT_GROUPS - 1, jnp.int32)
        for g in reversed(range(N_EXPERT_GROUPS - 1)):
            g_idx = jnp.where(scores[g] == best, g, g_idx)
        return g_idx

    def choose(probs, sel, g_idx):
        _, e0, _, e1 = top2(sel, group == g_idx)
        hot0 = eid == e0
        hot1 = eid == e1
        p0 = jnp.sum(jnp.where(hot0, probs, 0.0), 0, keepdims=True)
        p1 = jnp.sum(jnp.where(hot1, probs, 0.0), 0, keepdims=True)
        return hot0, hot1, p0, p1

    earlier = jnp.where(lax.broadcasted_iota(jnp.int32, (tm, tm), 0)
                        < lax.broadcasted_iota(jnp.int32, (tm, tm), 1), 1.0, 0.0).astype(BF16)
    below = jnp.where(lax.broadcasted_iota(jnp.int32, (N_EXPERTS, N_EXPERTS), 1)
                      < lax.broadcasted_iota(jnp.int32, (N_EXPERTS, N_EXPERTS), 0),
                      1.0, 0.0).astype(BF16)

    def positions(hot0, hot1):
        hot = jnp.where(hot0 | hot1, 1.0, 0.0)
        hot_bf = hot.astype(BF16)
        before = _dot(hot_bf, earlier)
        run_start = jnp.sum(_dot(below, hot_bf), 1, keepdims=True)
        local = run_start + before
        q0 = jnp.sum(jnp.where(hot0, local, 0.0), 0, keepdims=True)
        q1 = jnp.sum(jnp.where(hot1, local, 0.0), 0, keepdims=True)
        return q0, q1, jnp.sum(hot, 1, keepdims=True)

    tiles = range(ROUTE_TILES)
    scored = [probabilities(k) for k in tiles]
    groups = [best_group(sel) for _, sel in scored]
    chosen = [choose(probs, sel, g_idx) for (probs, sel), g_idx in zip(scored, groups)]
    placed = [positions(hot0, hot1) for hot0, hot1, _, _ in chosen]

    row = lax.broadcasted_iota(jnp.int32, (SUBLANES, tm), 0)
    grow = lax.broadcasted_iota(jnp.int32, (LANES, tm), 0)
    lane = lax.broadcasted_iota(jnp.int32, (N_EXPERTS, LANES), 1)
    base = base_ref[...]
    for k, ((_, _, p0, p1), (q0, q1, count)) in enumerate(zip(chosen, placed)):
        psum = p0 + p1
        pos_ref[k] = jnp.where(row == 0, q0, jnp.where(row == 1, q1, 0.0)).astype(jnp.int32)
        cols = jnp.where(grow == 0, p0 / psum, jnp.where(grow == 1, p1 / psum, jnp.where(
            grow == 2, q0, jnp.where(grow == 3, q1, 0.0))))
        gate_ref[pl.ds(k * tm, tm), :] = cols.T[:, :2 * TOP_K]
        runs_ref[k] = jnp.where(lane == 0, count, jnp.where(lane == 1, base, 0.0)).astype(jnp.int32)
        base = base + count
    base_ref[...] = base
    counts_ref[...] = base.astype(jnp.int32)


def _router(h_cm, w_router, b_router):
    batch, _, seq, _ = h_cm.shape
    t = batch * seq
    tm = TM_ROUTE
    rows = ROUTE_TILES * tm
    return pl.pallas_call(
        _router_kernel,
        grid=(t // rows,),
        in_specs=[
            _chunk_tile_spec(rows, seq // rows),
            pl.BlockSpec((D_MODEL, LANES), lambda i: (0, 0)),
            pl.BlockSpec((N_EXPERTS, 1), lambda i: (0, 0)),
        ],
        out_specs=[
            pl.BlockSpec((ROUTE_TILES, SUBLANES, tm), lambda i: (i, 0, 0)),
            pl.BlockSpec((rows, 2 * TOP_K), lambda i: (i, 0)),
            pl.BlockSpec((ROUTE_TILES, N_EXPERTS, LANES), lambda i: (i, 0, 0)),
            pl.BlockSpec((N_EXPERTS, 1), lambda i: (0, 0)),
        ],
        out_shape=[
            jax.ShapeDtypeStruct((t // tm, SUBLANES, tm), jnp.int32),
            jax.ShapeDtypeStruct((t, 2 * TOP_K), F32),
            jax.ShapeDtypeStruct((t // tm, N_EXPERTS, LANES), jnp.int32),
            jax.ShapeDtypeStruct((N_EXPERTS, 1), jnp.int32),
        ],
        scratch_shapes=[pltpu.VMEM((N_EXPERTS, 1), F32), pltpu.VMEM((D_MODEL, 2 * LANES), BF16)],
        compiler_params=pltpu.CompilerParams(
            dimension_semantics=("arbitrary",), vmem_limit_bytes=VMEM_LIMIT),
        name="router",
    )(h_cm, w_router, b_router)


def _start_run_copies(runs_ref, start_ref, tile_ref, rows_ref, sem, to_rows):
    offset = 0
    for e in range(N_EXPERTS):
        count = runs_ref[e, 0]
        first_row = start_ref[e] + runs_ref[e, 1]

        def copy_piece(bit, count=count, first_row=first_row, offset=offset):
            size = (1 << bit) * N_CHUNKS
            done = (count >> (bit + 1)) << (bit + 1)

            @pl.when(((count >> bit) & 1) == 1)
            def _():
                in_tile = tile_ref.at[
                    pl.ds(pl.multiple_of((offset + done) * N_CHUNKS, N_CHUNKS), size)]
                in_rows = rows_ref.at[
                    pl.ds(pl.multiple_of((first_row + done) * N_CHUNKS, N_CHUNKS), size)]
                if to_rows:
                    pltpu.make_async_copy(in_tile, in_rows, sem).start()
                else:
                    pltpu.make_async_copy(in_rows, in_tile, sem).start()

        @pl.when(count >= (1 << LONG_RUN_BIT))
        def _():
            for bit in reversed(range(LONG_RUN_BIT, RUN_BITS)):
                copy_piece(bit)

        for bit in reversed(range(LONG_RUN_BIT)):
            copy_piece(bit)
        offset = offset + count


def _wait_run_copies(tile_ref, rows_ref, sem):
    pltpu.make_async_copy(tile_ref, rows_ref.at[pl.ds(0, tile_ref.shape[0])], sem).wait()


def _dispatch_kernel(pos_ref, runs_ref, start_ref, end_ref, cnt_ref, h_ref, xs_ref,
                     zero_ref, sorted_ref, sem, zsem):
    i = pl.program_id(0)
    tm = TM_ROUTE

    @pl.when(i == 0)
    def _():
        zero_ref[...] = jnp.zeros_like(zero_ref)

        def zero_copy(first_row):
            first = pl.multiple_of(first_row * N_CHUNKS, N_CHUNKS)
            return pltpu.make_async_copy(
                zero_ref, xs_ref.at[pl.ds(first, TB_MOE * N_CHUNKS)], zsem)

        for wait in (False, True):
            for e in range(N_EXPERTS):
                @pl.when(cnt_ref[e, 0] > 0)
                def _():
                    cp = zero_copy(end_ref[e] - TB_MOE)
                    cp.wait() if wait else cp.start()

        def zero_block(blk, carry):
            cp = zero_copy(blk * TB_MOE)
            cp.start()
            cp.wait()
            return carry

        n_blocks = xs_ref.shape[0] // (TB_MOE * N_CHUNKS)
        lax.fori_loop(end_ref[N_EXPERTS - 1] // TB_MOE, n_blocks, zero_block, 0)

    p = lax.broadcasted_iota(jnp.int32, (TOP_K * tm, tm), 0)
    for k in range(DISPATCH_TILES):
        onehot = (p == pos_ref[k, 0:1, :]) | (p == pos_ref[k, 1:2, :])
        h = jnp.concatenate([h_ref[c, pl.ds(k * tm, tm), :] for c in range(N_CHUNKS)], -1)
        sorted_rows = _dot(jnp.where(onehot, 1.0, 0.0).astype(BF16), h.astype(BF16))
        buf = sorted_ref.at[k]

        @pl.when(i >= 1)
        def _():
            _wait_run_copies(buf, xs_ref, sem.at[k])

        _to_tiles(buf, sorted_rows, TOP_K * tm)
        _start_run_copies(runs_ref.at[k], start_ref, buf, xs_ref, sem.at[k], to_rows=True)

    @pl.when(i == pl.num_programs(0) - 1)
    def _():
        for k in range(DISPATCH_TILES):
            _wait_run_copies(sorted_ref.at[k], xs_ref, sem.at[k])


def _dispatch(pos, runs, pad_starts, pad_ends, counts, h_cm, n_rows):
    nt = pos.shape[0]
    tm = TM_ROUTE
    seq = h_cm.shape[2]
    tiles = DISPATCH_TILES
    smem_vec = pl.BlockSpec(memory_space=pltpu.SMEM)
    return pl.pallas_call(
        _dispatch_kernel,
        grid=(nt // tiles,),
        in_specs=[
            pl.BlockSpec((tiles, SUBLANES, tm), lambda i: (i, 0, 0)),
            pl.BlockSpec((tiles, N_EXPERTS, LANES), lambda i: (i, 0, 0), memory_space=pltpu.SMEM),
            smem_vec, smem_vec, smem_vec,
            _chunk_tile_spec(tiles * tm, seq // (tiles * tm))],
        out_specs=pl.BlockSpec(memory_space=pl.ANY),
        out_shape=jax.ShapeDtypeStruct((n_rows * N_CHUNKS, LANES), F32),
        scratch_shapes=[
            pltpu.VMEM((TB_MOE * N_CHUNKS, LANES), F32),
            pltpu.VMEM((tiles, TOP_K * tm * N_CHUNKS, LANES), F32),
            pltpu.SemaphoreType.DMA((tiles,)),
            pltpu.SemaphoreType.DMA(()),
        ],
        compiler_params=pltpu.CompilerParams(
            dimension_semantics=("arbitrary",), has_side_effects=True,
            vmem_limit_bytes=VMEM_LIMIT),
        name="dispatch",
    )(pos, runs, pad_starts, pad_ends, counts, h_cm)


def _experts_kernel(be_ref, nused_ref, valid_ref, xs_ref, wg_ref, wu_ref, wd_ref, ys_ref,
                    wg_bf, wu_bf, wd_bf):
    i = pl.program_id(0)
    nused = nused_ref[0]
    half = TB_MOE // 2

    def ffn(nrows):
        x = _from_tiles(xs_ref, nrows).astype(BF16)
        width = wg_bf.shape[1] // HIDDEN_SLICES
        cols = [slice(k * width, (k + 1) * width) for k in range(HIDDEN_SLICES)]
        acts = []
        for c in cols:
            hg = _dot(x, wg_bf[:, c])
            acts.append((hg * jax.nn.sigmoid(hg) * _dot(x, wu_bf[:, c])).astype(BF16))
        y = functools.reduce(jnp.add, [_dot(act, wd_bf[c, :]) for act, c in zip(acts, cols)])
        _to_tiles(ys_ref, y, nrows)

    @pl.when(i < nused)
    def _():
        prev = be_ref[jnp.maximum(i - 1, 0)]

        @pl.when((i == 0) | (be_ref[i] != prev))
        def _():
            wg_bf[...] = wg_ref[...].astype(BF16)
            wu_bf[...] = wu_ref[...].astype(BF16)
            wd_bf[...] = wd_ref[...].astype(BF16)

        @pl.when(valid_ref[i] > half)
        def _():
            ffn(TB_MOE)

        @pl.when(valid_ref[i] <= half)
        def _():
            ffn(half)
            padding = ys_ref.at[pl.ds(half * N_CHUNKS, half * N_CHUNKS)]
            padding[...] = jnp.zeros_like(padding)

    @pl.when(i >= nused)
    def _():
        ys_ref[...] = jnp.zeros_like(ys_ref)


def _experts(block_expert, nused, block_valid, xs, w_gate, w_up, w_down, layer):
    n_blocks = xs.shape[0] // (TB_MOE * N_CHUNKS)
    d, de = w_gate.shape[2], w_gate.shape[3]

    def x_map(i, be, nu, bv):
        return (jnp.minimum(i, nu[0] - 1), 0)

    def w_map(i, be, nu, bv):
        return (layer, be[jnp.minimum(i, nu[0] - 1)], 0, 0)

    return pl.pallas_call(
        _experts_kernel,
        grid_spec=pltpu.PrefetchScalarGridSpec(
            num_scalar_prefetch=3,
            grid=(n_blocks,),
            in_specs=[
                pl.BlockSpec((TB_MOE * N_CHUNKS, LANES), x_map),
                pl.BlockSpec((None, None, d, de), w_map),
                pl.BlockSpec((None, None, d, de), w_map),
                pl.BlockSpec((None, None, de, d), w_map),
            ],
            out_specs=pl.BlockSpec((TB_MOE * N_CHUNKS, LANES), lambda i, be, nu, bv: (i, 0)),
            scratch_shapes=[
                pltpu.VMEM((d, de), BF16),
                pltpu.VMEM((d, de), BF16),
                pltpu.VMEM((de, d), BF16),
            ],
        ),
        out_shape=jax.ShapeDtypeStruct(xs.shape, F32),
        compiler_params=pltpu.CompilerParams(
            dimension_semantics=("arbitrary",), vmem_limit_bytes=VMEM_LIMIT),
        name="experts",
    )(block_expert, nused, block_valid, xs, w_gate, w_up, w_down)


def _combine_kernel(runs_ref, next_runs_ref, start_ref, h_ref, gate_ref, g2_ref, b2_ref, ys_ref,
                    out_ref, sorted_ref, sem, *, chunk_major):
    tm = TM_ROUTE
    tiles = COMBINE_TILES
    i = pl.program_id(0)
    slot = i % 2

    @pl.when(i == 0)
    def _():
        for k in range(tiles):
            _start_run_copies(runs_ref.at[k], start_ref, sorted_ref.at[0, k], ys_ref,
                              sem.at[0, k], to_rows=False)

    @pl.when(i + 1 < pl.num_programs(0))
    def _():
        for k in range(tiles):
            _start_run_copies(next_runs_ref.at[k], start_ref, sorted_ref.at[1 - slot, k], ys_ref,
                              sem.at[1 - slot, k], to_rows=False)

    p = lax.broadcasted_iota(jnp.int32, (tm, TOP_K * tm), 1)
    for k in range(tiles):
        rows = pl.ds(k * tm, tm)
        _wait_run_copies(sorted_ref.at[slot, k], ys_ref, sem.at[slot, k])

        gate = gate_ref[rows, :]
        onehot = jnp.concatenate([p == gate[:, 2:3].astype(jnp.int32),
                                  p == gate[:, 3:4].astype(jnp.int32)], 0)
        picked = _dot(jnp.where(onehot, 1.0, 0.0).astype(BF16),
                      _from_tiles(sorted_ref.at[slot, k], TOP_K * tm).astype(BF16))
        ffn = picked[:tm] * gate[:, 0:1] + picked[tm:] * gate[:, 1:2]
        h = jnp.concatenate([h_ref[c, rows, :] for c in range(N_CHUNKS)], -1)
        out = _layer_norm(ALPHA * h + ffn, g2_ref[...], b2_ref[...])
        if chunk_major:
            for c in range(N_CHUNKS):
                out_ref[c, rows, :] = out[:, c * LANES:(c + 1) * LANES]
        else:
            out_ref[rows, :] = out


def _combine(runs, pad_starts, h_cm, gate, g2, b2, ys, chunk_major):
    batch, _, seq, _ = h_cm.shape
    tm = TM_ROUTE
    tiles = COMBINE_TILES
    rows = tiles * tm
    steps = batch * seq // rows
    per_seq = seq // rows
    if chunk_major:
        out_spec = _chunk_tile_spec(rows, per_seq)
        out_shape = jax.ShapeDtypeStruct((batch, N_CHUNKS, seq, LANES), F32)
    else:
        out_spec = pl.BlockSpec((None, rows, D_MODEL), lambda i: (i // per_seq, i % per_seq, 0))
        out_shape = jax.ShapeDtypeStruct((batch, seq, D_MODEL), F32)
    return pl.pallas_call(
        functools.partial(_combine_kernel, chunk_major=chunk_major),
        grid=(steps,),
        in_specs=[
            pl.BlockSpec((tiles, N_EXPERTS, LANES), lambda i: (i, 0, 0), memory_space=pltpu.SMEM),
            pl.BlockSpec((tiles, N_EXPERTS, LANES),
                         lambda i: (jnp.minimum(i + 1, steps - 1), 0, 0), memory_space=pltpu.SMEM),
            pl.BlockSpec(memory_space=pltpu.SMEM),
            _chunk_tile_spec(rows, per_seq),
            pl.BlockSpec((rows, 2 * TOP_K), lambda i: (i, 0)),
            pl.BlockSpec((1, D_MODEL), lambda i: (0, 0)),
            pl.BlockSpec((1, D_MODEL), lambda i: (0, 0)),
            pl.BlockSpec(memory_space=pl.ANY),
        ],
        out_specs=out_spec,
        out_shape=out_shape,
        scratch_shapes=[
            pltpu.VMEM((2, tiles, TOP_K * tm * N_CHUNKS, LANES), F32),
            pltpu.SemaphoreType.DMA((2, tiles)),
        ],
        compiler_params=pltpu.CompilerParams(
            dimension_semantics=("arbitrary",), vmem_limit_bytes=VMEM_LIMIT),
        name="combine",
    )(runs, runs, pad_starts, h_cm, gate, g2, b2, ys)


def _moe(h_cm, w_router, b_router, w_gate, w_up, w_down, g2, b2, layer, chunk_major):
    t = h_cm.shape[0] * h_cm.shape[2]
    pos, gate, runs, counts = _router(h_cm, w_router, b_router)
    cnt = counts[:, 0]
    padded = (cnt + TB_MOE - 1) // TB_MOE * TB_MOE
    pad_ends = jnp.cumsum(padded).astype(jnp.int32)
    pad_starts = pad_ends - padded
    n_rows = t * TOP_K + N_EXPERTS * TB_MOE
    n_blocks = n_rows // TB_MOE
    block_start = jnp.arange(n_blocks, dtype=jnp.int32) * TB_MOE
    block_expert = jnp.minimum(
        jnp.sum(block_start[:, None] >= pad_ends[None, :], -1), N_EXPERTS - 1).astype(jnp.int32)
    nused = (pad_ends[-1:] // TB_MOE).astype(jnp.int32)
    block_valid = jnp.clip((pad_starts + cnt)[block_expert] - block_start, 0, TB_MOE)
    xs = _dispatch(pos, runs, pad_starts, pad_ends, counts, h_cm, n_rows)
    ys = _experts(block_expert, nused, block_valid.astype(jnp.int32), xs, w_gate, w_up, w_down,
                  layer)
    return _combine(runs, pad_starts, h_cm, gate, g2, b2, ys, chunk_major)


def _residue_rows(x_ref, dilation):
    n = x_ref.shape[1]
    if dilation == 1:
        return _from_chunks(x_ref)
    per = n // dilation
    return jnp.concatenate(
        [jnp.concatenate([x_ref[c, pl.ds(r, per, stride=dilation), :] for r in range(dilation)], 0)
         for c in range(N_CHUNKS)], -1)


def _proj_b_kernel(x_ref, wqkv_ref, wqm_ref, qkv0_ref, qkv1_ref, qkv2_ref, qm_ref):
    ts = x_ref.shape[1]
    qscale = jnp.where(
        lax.broadcasted_iota(jnp.int32, (1, 3 * DIL_OUT_WIDTH), 1) < DIL_OUT_WIDTH, ATT_SCALE, 1.0)
    for g, (out_ref, (_, dilation)) in enumerate(zip((qkv0_ref, qkv1_ref, qkv2_ref), DIL_PAIRS)):
        xb = _residue_rows(x_ref, dilation).astype(BF16)
        qkv = (_dot(xb, wqkv_ref[g]) * qscale).astype(BF16)
        per = ts // dilation
        for r in range(dilation):
            out_ref[r] = qkv[r * per:(r + 1) * per]
    qm = _dot(_from_chunks(x_ref).astype(BF16), wqm_ref[...])
    qm_ref[...] = (qm * ATT_SCALE).astype(BF16)


def _proj_b(x_cm, wqkv, wqm):
    b, _, s, _ = x_cm.shape
    ts = TS_PROJ
    width = 3 * DIL_OUT_WIDTH
    out_specs, out_shapes = [], []
    for _, dilation in DIL_PAIRS:
        out_specs.append(pl.BlockSpec((None, dilation, ts // dilation, width),
                                      lambda i, j: (i, 0, j, 0)))
        out_shapes.append(jax.ShapeDtypeStruct((b, dilation, s // dilation, width), BF16))
    out_specs.append(pl.BlockSpec((None, ts, MEM_WIDTH), lambda i, j: (i, j, 0)))
    out_shapes.append(jax.ShapeDtypeStruct((b, s, MEM_WIDTH), BF16))
    return pl.pallas_call(
        _proj_b_kernel,
        grid=(b, s // ts),
        in_specs=[
            pl.BlockSpec((None, N_CHUNKS, ts, LANES), lambda i, j: (i, 0, j, 0)),
            pl.BlockSpec(wqkv.shape, lambda i, j: (0, 0, 0)),
            pl.BlockSpec(wqm.shape, lambda i, j: (0, 0)),
        ],
        out_specs=out_specs,
        out_shape=out_shapes,
        compiler_params=pltpu.CompilerParams(vmem_limit_bytes=VMEM_LIMIT),
        name="proj_b",
    )(x_cm, wqkv, wqm)


def _dil_attn_kernel(qkv_ref, o_ref, lse_ref, *, dilation, slopes):
    nb = qkv_ref.shape[1] // CHUNK
    width = DIL_OUT_WIDTH
    heads = HEADS_PER_DIL
    row = lax.broadcasted_iota(jnp.int32, (heads * CHUNK, 2 * CHUNK), 0)
    jk = lax.broadcasted_iota(jnp.int32, (heads * CHUNK, 2 * CHUNK), 1)
    rel = CHUNK + row % CHUNK - jk
    slope = functools.reduce(
        lambda acc, h: jnp.where(row // CHUNK == h, slopes[h], acc), range(1, heads), slopes[0])
    bias = jnp.where((rel >= 0) & (rel <= CHUNK), -slope * (dilation * rel).astype(F32), -jnp.inf)
    before_start = jk < CHUNK
    head = lax.broadcasted_iota(jnp.int32, (1, width), 1) // HEAD_DIM

    def scores(idx, may_be_first):
        r = idx // nb
        jb = idx % nb
        cur = pl.ds(pl.multiple_of(jb * CHUNK, CHUNK), CHUNK)
        prev = pl.ds(pl.multiple_of(jnp.maximum(jb - 1, 0) * CHUNK, CHUNK), CHUNK)
        q = qkv_ref[r, cur, 0:width]
        kband = jnp.concatenate([qkv_ref[r, prev, width:2 * width],
                                 qkv_ref[r, cur, width:2 * width]], 0)
        stacked = jnp.concatenate(
            [jnp.where(head == h, q, jnp.zeros_like(q)) for h in range(heads)], 0)
        s = _dot_nt(stacked, kband) + bias
        if may_be_first:
            s = jnp.where(before_start & (jb == 0), -jnp.inf, s)
        return r, cur, prev, s

    def softmax(s):
        m = jnp.max(s, -1, keepdims=True)
        p = jnp.exp(s - m)
        l = jnp.sum(p, -1, keepdims=True)
        return (p * (1.0 / l)).astype(BF16), m + jnp.log(l)

    def finish(r, cur, prev, p, row_lse):
        vband = jnp.concatenate([qkv_ref[r, prev, 2 * width:3 * width],
                                 qkv_ref[r, cur, 2 * width:3 * width]], 0)
        o = _dot(p, vband)
        out = jnp.zeros((CHUNK, width), F32)
        lse = jnp.zeros((CHUNK, width), F32)
        for h in range(heads):
            rows = slice(h * CHUNK, (h + 1) * CHUNK)
            out = jnp.where(head == h, o[rows], out)
            lse = jnp.where(head == h, row_lse[rows], lse)
        o_ref[r, cur, :] = out
        lse_ref[r, cur, :] = lse

    def group(i, carry):
        blocks = [scores(DIL_INTERLEAVE * i + j, j % nb == 0) for j in range(DIL_INTERLEAVE)]
        probs = [softmax(s) for _, _, _, s in blocks]
        for (r, cur, prev, _), (p, row_lse) in zip(blocks, probs):
            finish(r, cur, prev, p, row_lse)
        return carry

    lax.fori_loop(0, dilation * nb // DIL_INTERLEAVE, group, 0)


def _dil_attn(qkv, group):
    b, dilation, length, _ = qkv.shape
    slopes = tuple(2.0 ** (-8.0 * (group * HEADS_PER_DIL + h + 1) / N_DIL_HEADS)
                   for h in range(HEADS_PER_DIL))
    out_spec = pl.BlockSpec((None, dilation, length, DIL_OUT_WIDTH), lambda i: (i, 0, 0, 0))
    out_shape = jax.ShapeDtypeStruct((b, dilation, length, DIL_OUT_WIDTH), F32)
    return pl.pallas_call(
        functools.partial(_dil_attn_kernel, dilation=dilation, slopes=slopes),
        grid=(b,),
        in_specs=[pl.BlockSpec((None, dilation, length, 3 * DIL_OUT_WIDTH),
                               lambda i: (i, 0, 0, 0))],
        out_specs=[out_spec, out_spec],
        out_shape=[out_shape, out_shape],
        compiler_params=pltpu.CompilerParams(vmem_limit_bytes=VMEM_LIMIT),
        name=f"dil_attn_{group}",
    )(qkv)


def _token_order(src_ref, scratch, dilation):
    if dilation == 1:
        return src_ref[0]
    per = src_ref.shape[1]
    halves = DIL_OUT_WIDTH // LANES
    for r in range(dilation):
        v = src_ref[r]
        for c in range(halves):
            scratch[c, pl.ds(r, per, stride=dilation), :] = v[:, c * LANES:(c + 1) * LANES]
    return jnp.concatenate([scratch[c] for c in range(halves)], -1)


def _mixer_b_kernel(x_ref, qm_ref, o0_ref, l0_ref, o1_ref, l1_ref, o2_ref, l2_ref,
                    mk_ref, mv_ref, wout_ref, g1_ref, b1_ref, h_ref,
                    so1, sl1, so2, sl2):
    outs = [o0_ref[0], _token_order(o1_ref, so1, DIL_PAIRS[1][1]),
            _token_order(o2_ref, so2, DIL_PAIRS[2][1])]
    lses = [l0_ref[0], _token_order(l1_ref, sl1, DIL_PAIRS[1][1]),
            _token_order(l2_ref, sl2, DIL_PAIRS[2][1])]
    top = functools.reduce(jnp.maximum, lses)
    es = [jnp.exp(l - top) for l in lses]
    inv = 1.0 / functools.reduce(jnp.add, es)
    mix = functools.reduce(jnp.add, [e * inv * o for e, o in zip(es, outs)])
    mo = _mem_attention(qm_ref[...], mk_ref[...], mv_ref[...])
    att = _dot(jnp.concatenate([mix, mo], -1).astype(BF16), wout_ref[...])
    _to_chunks(h_ref, _layer_norm(ALPHA * _from_chunks(x_ref) + att, g1_ref[...], b1_ref[...]))


def _mixer_b(x_cm, qm, attn, mk, mv, w_out, g1, b1):
    b, _, s, _ = x_cm.shape
    ts = TS_MIX
    nts = s // ts
    m = mk.shape[0] // b
    in_specs = [
        pl.BlockSpec((None, N_CHUNKS, ts, LANES), lambda i, j: (i, 0, j, 0)),
        pl.BlockSpec((None, ts, MEM_WIDTH), lambda i, j: (i, j, 0)),
    ]
    args = [x_cm, qm]
    for (o, lse), (_, dilation) in zip(attn, DIL_PAIRS):
        spec = pl.BlockSpec((None, dilation, ts // dilation, DIL_OUT_WIDTH),
                            lambda i, j: (i, 0, j, 0))
        in_specs += [spec, spec]
        args += [o, lse]
    in_specs += [
        pl.BlockSpec((m, MEM_WIDTH), lambda i, j: (i, 0)),
        pl.BlockSpec((m, MEM_WIDTH), lambda i, j: (i, 0)),
        pl.BlockSpec(w_out.shape, lambda i, j: (0, 0)),
        pl.BlockSpec((1, D_MODEL), lambda i, j: (0, 0)),
        pl.BlockSpec((1, D_MODEL), lambda i, j: (0, 0)),
    ]
    args += [mk, mv, w_out, g1, b1]
    halves = DIL_OUT_WIDTH // LANES
    return pl.pallas_call(
        _mixer_b_kernel,
        grid=(b, nts),
        in_specs=in_specs,
        out_specs=pl.BlockSpec((None, N_CHUNKS, ts, LANES), lambda i, j: (i, 0, j, 0)),
        out_shape=jax.ShapeDtypeStruct((b, N_CHUNKS, s, LANES), F32),
        scratch_shapes=[pltpu.VMEM((halves, ts, LANES), F32)] * 4,
        compiler_params=pltpu.CompilerParams(vmem_limit_bytes=VMEM_LIMIT),
        name="mixer_b",
    )(*args)


def kernel(x, mem, w_in_a, w_out_a, sg_ln_g, sg_ln_b, sg_w, sg_b, w_in_b, w_out_b, w_k_shared,
           w_v_shared, w_mem_k, w_mem_v, ln1_g, ln1_b, ln2_g, ln2_b, w_router, b_router,
           w_gate, w_up, w_down):
    batch, seq, d = x.shape
    mk, mv = _memkv(mem.reshape(-1, d), w_mem_k, w_mem_v)
    row = lambda v: v.reshape(1, -1)

    sg_bias = jnp.repeat(sg_b[0].T, HEAD_DIM, axis=1)
    h = _mixer_a(x, w_in_a[0].astype(BF16), w_out_a[0].astype(BF16), row(sg_ln_g[0]),
                 row(sg_ln_b[0]), sg_w[0], sg_bias, mk[0], mv[0], row(ln1_g[0]), row(ln1_b[0]))
    w_router_pad = jnp.pad(w_router, ((0, 0), (0, LANES - N_EXPERTS)))
    b_router_col = b_router.reshape(N_EXPERTS, 1)
    x_cm = _moe(h, w_router_pad, b_router_col, w_gate, w_up, w_down,
                row(ln2_g[0]), row(ln2_b[0]), layer=0, chunk_major=True)

    wqkv = jnp.stack([
        jnp.concatenate([w[:, g * DIL_OUT_WIDTH:(g + 1) * DIL_OUT_WIDTH]
                         for w in (w_in_b[0], w_k_shared, w_v_shared)], -1)
        for g in range(len(DIL_PAIRS))]).astype(BF16)
    wqm = w_in_b[0][:, DIL_Q_WIDTH:].astype(BF16)
    qkv0, qkv1, qkv2, qm = _proj_b(x_cm, wqkv, wqm)
    attn = [_dil_attn(qkv, g) for g, qkv in enumerate((qkv0, qkv1, qkv2))]
    h = _mixer_b(x_cm, qm, attn, mk[1], mv[1], w_out_b[0].astype(BF16),
                 row(ln1_g[1]), row(ln1_b[1]))
    return _moe(h, w_router_pad, b_router_col, w_gate, w_up, w_down,
                row(ln2_g[1]), row(ln2_b[1]), layer=1, chunk_major=False)
```
